```python
import jax, jax.numpy as jnp
from jax import lax
import numpy as np

D_MODEL = 1024
BATCH = 2
SEQ = 8192
DEPTH = 4
DEC_BATCH = 32
DEC_SEQ = 1
PAST_LEN = 8192
PAGE_SIZE = 128

N_A_LAYERS = DEPTH // 2
N_B_LAYERS = DEPTH - N_A_LAYERS
D_FF = 4 * D_MODEL
CHUNK = 128
A_GROUP_DIM = 128
A_GROUPS = D_MODEL // A_GROUP_DIM
HEAD_DIM = 64
N_HEADS = D_MODEL // HEAD_DIM
N_KV = 4
Q_PER_KV = N_HEADS // N_KV
CMP_STRIDE = 16
CMP_LEN = 2 * CMP_STRIDE
CMP_HIDDEN = 4 * HEAD_DIM
SEL_LEN = 64
N_SEL = 16
WINDOW = 512
Q_BLOCK = 128
EPS = 1e-6
NEG_INF = -1e30
FORCE_SCORE = 1e6

kernel_name = 'yoco_gmlp_nsa_decoder_step'


def rmsnorm(x, g):
    x32 = x.astype(jnp.float32)
    y = x32 * lax.rsqrt(jnp.mean(x32 * x32, axis=-1, keepdims=True) + EPS)
    return (y * g.astype(jnp.float32)).astype(x.dtype)


def sq_relu_mlp(x, w_up, w_down):
    return jnp.square(jax.nn.relu(x @ w_up)) @ w_down


def masked_softmax(s, mask):
    s = jnp.where(mask, s.astype(jnp.float32), NEG_INF)
    return jnp.where(mask, jax.nn.softmax(s, axis=-1), 0.0)


def chunk_gmlp(xn, w_uv, v_norm_g, w_s, b_s, w_o):
    b, t, _ = xn.shape
    uv = xn @ w_uv
    u, v = uv[..., :D_MODEL], rmsnorm(uv[..., D_MODEL:], v_norm_g)
    n_chunk = -(-t // CHUNK)
    vp = jnp.pad(v, ((0, 0), (0, n_chunk * CHUNK - t), (0, 0)))
    vp = vp.reshape(b, n_chunk, CHUNK, A_GROUPS, A_GROUP_DIM)
    s = jnp.einsum('gts,bnsgc->bntgc', jnp.tril(w_s), vp) + b_s.T[None, None, :, :, None]
    s = s.reshape(b, n_chunk * CHUNK, D_MODEL)[:, :t]
    return (u * s) @ w_o, v


def shared_kv(h, kv_norm_g, w_kv):
    b, t, _ = h.shape
    kv = (rmsnorm(h, kv_norm_g) @ w_kv).reshape(b, t, 3, 2, N_KV, HEAD_DIM)
    return kv[:, :, 0], kv[:, :, 1], kv[:, :, 2]


def compress_rows(k, pe, w1, w2):
    b, t = k.shape[:2]
    n_cmp = (t - CMP_LEN) // CMP_STRIDE + 1
    kr = k[:, :(n_cmp + 1) * CMP_STRIDE].reshape(b, n_cmp + 1, CMP_STRIDE, N_KV, HEAD_DIM)
    blocks = jnp.concatenate([kr[:, :-1], kr[:, 1:]], axis=2) + pe[:, None, :]
    flat = jnp.swapaxes(blocks, 2, 3).reshape(b, n_cmp, N_KV, CMP_LEN * HEAD_DIM)
    return jax.nn.gelu(flat @ w1) @ w2


def selection_map(n_cmp, n_blk):
    i = np.arange(n_cmp)[:, None]
    j = np.arange(n_blk)[None, :]
    lo = np.maximum(i * CMP_STRIDE, j * SEL_LEN)
    hi = np.minimum(i * CMP_STRIDE + CMP_LEN, (j + 1) * SEL_LEN)
    return jnp.asarray(np.maximum(hi - lo, 0) // CMP_STRIDE, dtype=jnp.float32)


def nsa_key_side(cmp_kv, sel_kv, pe_k, w1_k, w2_k, pe_v, w1_v, w2_v):
    b, t = cmp_kv.shape[:2]
    kc = compress_rows(cmp_kv[:, :, 0], pe_k, w1_k, w2_k)
    vc = compress_rows(cmp_kv[:, :, 1], pe_v, w1_v, w2_v)
    n_cmp = kc.shape[1]
    cend = jnp.arange(n_cmp, dtype=jnp.int32) * CMP_STRIDE + (CMP_LEN - 1)
    n_blk = -(-t // SEL_LEN)
    sel = jnp.pad(sel_kv, ((0, 0), (0, n_blk * SEL_LEN - t), (0, 0), (0, 0), (0, 0)))
    sel = sel.reshape(b, n_blk, SEL_LEN, 2, N_KV, HEAD_DIM).transpose(3, 0, 4, 1, 2, 5)
    return kc, vc, cend, sel[0], sel[1], selection_map(n_cmp, n_blk)


def nsa_attend(q, gates, qpos, kc, vc, cend, ks_blk, vs_blk, smap, kw, vw, wpos):
    b, nq = q.shape[:2]
    scale = HEAD_DIM ** -0.5
    qg = q.reshape(b, nq, N_KV, Q_PER_KV, HEAD_DIM)
    s_c = jnp.einsum('bqgrd,bcgd->bqgrc', qg, kc) * scale
    m_c = (cend[None, :] <= qpos[:, None])[None, :, None, None, :]
    p_c = masked_softmax(s_c, m_c)
    o_c = jnp.einsum('bqgrc,bcgd->bqgrd', p_c.astype(vc.dtype), vc)
    n_blk = smap.shape[1]
    imp = jnp.einsum('bqgrc,cj->bqgj', p_c, smap)
    blk = jnp.arange(n_blk, dtype=jnp.int32)[None, :]
    cur = (qpos // SEL_LEN)[:, None]
    valid = blk <= cur
    forced = (blk == 0) | (blk == cur) | (blk == cur - 1)
    score = jnp.where(forced[None, :, None, :], FORCE_SCORE,
                      jnp.where(valid[None, :, None, :], imp, -1.0))
    top_s, top_i = lax.top_k(score, min(N_SEL, n_blk))
    n_top = top_i.shape[-1]
    ids = jnp.swapaxes(top_i, 1, 2).reshape(b, N_KV, nq * n_top)
    gather = jax.vmap(jax.vmap(lambda blocks, i: blocks[i]))
    k_sel = gather(ks_blk, ids).reshape(b, N_KV, nq, n_top, SEL_LEN, HEAD_DIM)
    v_sel = gather(vs_blk, ids).reshape(b, N_KV, nq, n_top, SEL_LEN, HEAD_DIM)
    tpos = top_i[..., None] * SEL_LEN + jnp.arange(SEL_LEN, dtype=jnp.int32)
    m_s = (tpos <= qpos[None, :, None, None, None]) & (top_s >= 0.0)[..., None]
    s_s = jnp.einsum('bqgrd,bgqkld->bqgrkl', qg, k_sel) * scale
    p_s = masked_softmax(s_s.reshape(b, nq, N_KV, Q_PER_KV, n_top * SEL_LEN),
                         m_s.reshape(b, nq, N_KV, 1, n_top * SEL_LEN))
    o_s = jnp.einsum('bqgrkl,bgqkld->bqgrd', p_s.reshape(s_s.shape).astype(v_sel.dtype), v_sel)
    s_w = jnp.einsum('bqgrd,bwgd->bqgrw', qg, kw) * scale
    dist = qpos[:, None] - wpos[None, :]
    m_w = ((dist >= 0) & (dist < WINDOW) & (wpos >= 0)[None, :])[None, :, None, None, :]
    p_w = masked_softmax(s_w, m_w)
    o_w = jnp.einsum('bqgrw,bwgd->bqgrd', p_w.astype(vw.dtype), vw)
    g = gates.reshape(b, nq, N_KV, Q_PER_KV, 3)
    o = g[..., 0:1] * o_c + g[..., 1:2] * o_s + g[..., 2:3] * o_w
    return o.reshape(b, nq, N_HEADS * HEAD_DIM)


def nsa_prompt(q, gates, ctx, win_pad):
    b, s = q.shape[:2]

    def one_block(n):
        s0 = n * Q_BLOCK
        qpos = s0 + jnp.arange(Q_BLOCK, dtype=jnp.int32)
        wpos = s0 - WINDOW + jnp.arange(WINDOW + Q_BLOCK, dtype=jnp.int32)
        win = lax.dynamic_slice_in_dim(win_pad, s0, WINDOW + Q_BLOCK, axis=1)
        return nsa_attend(lax.dynamic_slice_in_dim(q, s0, Q_BLOCK, axis=1),
                          lax.dynamic_slice_in_dim(gates, s0, Q_BLOCK, axis=1),
                          qpos, *ctx, win[:, :, 0], win[:, :, 1], wpos)

    out = lax.map(one_block, jnp.arange(s // Q_BLOCK, dtype=jnp.int32))
    return jnp.swapaxes(out, 0, 1).reshape(b, s, N_HEADS * HEAD_DIM)


def query_side(h, norm_g, w_qg, b_g):
    b, t, _ = h.shape
    z = rmsnorm(h, norm_g) @ w_qg
    q = z[..., :N_HEADS * HEAD_DIM].reshape(b, t, N_HEADS, HEAD_DIM)
    gates = jax.nn.sigmoid(z[..., N_HEADS * HEAD_DIM:] + b_g).reshape(b, t, N_HEADS, 3)
    return q, gates


def setup_inputs(seed: int = 0) -> dict:
    key = jax.random.key(seed)
    ks = iter(jax.random.split(key, 40))

    def nrm(shape, scale):
        return scale * jax.random.normal(next(ks), shape, jnp.float32)

    def gain(shape):
        return 1.0 + 0.02 * jax.random.normal(next(ks), shape, jnp.float32)

    n_pages = PAST_LEN // PAGE_SIZE
    n_used = DEC_BATCH * n_pages
    n_pool = n_used + max(1, n_used // 4)
    win_buf = min(WINDOW, PAST_LEN)
    kv_width = 6 * N_KV * HEAD_DIM
    qg_width = N_HEADS * HEAD_DIM + 3 * N_HEADS
    page_table = jax.random.permutation(next(ks), n_pool)[:n_used].reshape(DEC_BATCH, n_pages).astype(jnp.int32)
    return {
        'x_prompt': nrm((BATCH, SEQ, D_MODEL), 1.0),
        'x_sample': nrm((DEC_BATCH, DEC_SEQ, D_MODEL), 1.0),
        'cache_cmp_kv': nrm((n_pool, PAGE_SIZE, 2, N_KV, HEAD_DIM), 1.0),
        'cache_sel_kv': nrm((n_pool, PAGE_SIZE, 2, N_KV, HEAD_DIM), 1.0),
        'state_win_kv': nrm((DEC_BATCH, win_buf, 2, N_KV, HEAD_DIM), 1.0),
        'page_table': page_table,
        'a_norm_g': gain((N_A_LAYERS, D_MODEL)),
        'a_w_uv': nrm((N_A_LAYERS, D_MODEL, 2 * D_MODEL), D_MODEL ** -0.5),
        'a_v_norm_g': gain((N_A_LAYERS, D_MODEL)),
        'a_w_s': nrm((N_A_LAYERS, A_GROUPS, CHUNK, CHUNK), CHUNK ** -0.5),
        'a_b_s': gain((N_A_LAYERS, A_GROUPS, CHUNK)),
        'a_w_o': nrm((N_A_LAYERS, D_MODEL, D_MODEL), D_MODEL ** -0.5),
        'kv_norm_g': gain((D_MODEL,)),
        'w_kv': nrm((D_MODEL, kv_width), D_MODEL ** -0.5),
        'cmp_pe_k': nrm((CMP_LEN, HEAD_DIM), 0.1),
        'cmp_w1_k': nrm((CMP_LEN * HEAD_DIM, CMP_HIDDEN), (CMP_LEN * HEAD_DIM) ** -0.5),
        'cmp_w2_k': nrm((CMP_HIDDEN, HEAD_DIM), 1.5 * CMP_HIDDEN ** -0.5),
        'cmp_pe_v': nrm((CMP_LEN, HEAD_DIM), 0.1),
        'cmp_w1_v': nrm((CMP_LEN * HEAD_DIM, CMP_HIDDEN), (CMP_LEN * HEAD_DIM) ** -0.5),
        'cmp_w2_v': nrm((CMP_HIDDEN, HEAD_DIM), 1.5 * CMP_HIDDEN ** -0.5),
        'b_norm_g': gain((N_B_LAYERS, D_MODEL)),
        'b_w_qg': nrm((N_B_LAYERS, D_MODEL, qg_width), D_MODEL ** -0.5),
        'b_b_g': nrm((N_B_LAYERS, 3 * N_HEADS), 0.02),
        'b_w_o': nrm((N_B_LAYERS, N_HEADS * HEAD_DIM, D_MODEL), (N_HEADS * HEAD_DIM) ** -0.5),
        'mlp_norm_g': gain((DEPTH, D_MODEL)),
        'mlp_w_up': nrm((DEPTH, D_MODEL, D_FF), D_MODEL ** -0.5),
        'mlp_w_down': nrm((DEPTH, D_FF, D_MODEL), D_FF ** -0.5),
        'final_norm_g': gain((D_MODEL,)),
    }


def reference(x_prompt, x_sample, cache_cmp_kv, cache_sel_kv, state_win_kv, page_table,
              a_norm_g, a_w_uv, a_v_norm_g, a_w_s, a_b_s, a_w_o,
              kv_norm_g, w_kv, cmp_pe_k, cmp_w1_k, cmp_w2_k, cmp_pe_v, cmp_w1_v, cmp_w2_v,
              b_norm_g, b_w_qg, b_b_g, b_w_o,
              mlp_norm_g, mlp_w_up, mlp_w_down, final_norm_g):
    n_seq, n_pages = page_table.shape
    past_len = n_pages * cache_cmp_kv.shape[1]
    win_buf = state_win_kv.shape[1]
    seq = x_prompt.shape[1]
    dec_seq = x_sample.shape[1]
    cmp_w = (cmp_pe_k, cmp_w1_k, cmp_w2_k, cmp_pe_v, cmp_w1_v, cmp_w2_v)
    hp, hs = x_prompt, x_sample
    a_v_rows = []
    for layer in range(DEPTH):
        if layer == N_A_LAYERS:
            cmp_p, sel_p, win_p = shared_kv(hp, kv_norm_g, w_kv)
            cmp_s, sel_s, win_s = shared_kv(hs, kv_norm_g, w_kv)
            ctx_p = nsa_key_side(cmp_p, sel_p, *cmp_w)
            win_pad_p = jnp.pad(win_p, ((0, 0), (WINDOW, 0), (0, 0), (0, 0), (0, 0)))
            past_cmp = cache_cmp_kv[page_table].reshape(n_seq, past_len, 2, N_KV, HEAD_DIM)
            past_sel = cache_sel_kv[page_table].reshape(n_seq, past_len, 2, N_KV, HEAD_DIM)
            ctx_s = nsa_key_side(jnp.concatenate([past_cmp, cmp_s], axis=1),
                                 jnp.concatenate([past_sel, sel_s], axis=1), *cmp_w)
            win_full_s = jnp.concatenate([state_win_kv, win_s], axis=1)
            wpos_s = past_len - win_buf + jnp.arange(win_buf + dec_seq, dtype=jnp.int32)
            qpos_s = past_len + jnp.arange(dec_seq, dtype=jnp.int32)
        if layer < N_A_LAYERS:
            i = layer
            mix_p, _ = chunk_gmlp(rmsnorm(hp, a_norm_g[i]), a_w_uv[i], a_v_norm_g[i], a_w_s[i], a_b_s[i], a_w_o[i])
            mix_s, v_s = chunk_gmlp(rmsnorm(hs, a_norm_g[i]), a_w_uv[i], a_v_norm_g[i], a_w_s[i], a_b_s[i], a_w_o[i])
            a_v_rows.append(v_s)
        else:
            j = layer - N_A_LAYERS
            q_p, g_p = query_side(hp, b_norm_g[j], b_w_qg[j], b_b_g[j])
            q_s, g_s = query_side(hs, b_norm_g[j], b_w_qg[j], b_b_g[j])
            mix_p = nsa_prompt(q_p, g_p, ctx_p, win_pad_p) @ b_w_o[j]
            mix_s = nsa_attend(q_s, g_s, qpos_s, *ctx_s, win_full_s[:, :, 0], win_full_s[:, :, 1], wpos_s) @ b_w_o[j]
        hp = hp + mix_p
        hs = hs + mix_s
        hp = hp + sq_relu_mlp(rmsnorm(hp, mlp_norm_g[layer]), mlp_w_up[layer], mlp_w_down[layer])
        hs = hs + sq_relu_mlp(rmsnorm(hs, mlp_norm_g[layer]), mlp_w_up[layer], mlp_w_down[layer])
    y_prompt = rmsnorm(hp, final_norm_g)
    y_sample = rmsnorm(hs, final_norm_g)
    win_kv_prompt = win_p[:, seq - min(WINDOW, seq):]
    win_kv_sample = win_full_s[:, dec_seq:]
    a_v_sample = jnp.stack(a_v_rows, axis=0)
    return (y_prompt, y_sample, cmp_p, sel_p, win_kv_prompt, cmp_s, sel_s, win_kv_sample, a_v_sample)
```

```python
import functools

import numpy as np
import jax
import jax.numpy as jnp
from jax import lax
from jax.experimental import pallas as pl
from jax.experimental.pallas import tpu as pltpu

F32 = jnp.float32
BF16 = jnp.bfloat16

D_MODEL = 1024
D_FF = 4 * D_MODEL
CHUNK = 128
A_GROUP_DIM = 128
A_GROUPS = D_MODEL // A_GROUP_DIM
HEAD_DIM = 64
N_HEADS = D_MODEL // HEAD_DIM
N_KV = 4
Q_PER_KV = N_HEADS // N_KV
KV_COLS = N_KV * HEAD_DIM
CMP_STRIDE = 16
CMP_LEN = 2 * CMP_STRIDE
CMP_HIDDEN = 4 * HEAD_DIM
SEL_LEN = 64
N_SEL = 16
WINDOW = 512
Q_BLOCK = 128
PAGE_SIZE = 128
EPS = 1e-6
NEG_INF = -1e30
FORCE_SCORE = 1e6
SCALE = HEAD_DIM ** -0.5
LANES = 128
SEL_TILE = 512
PAGES_PER_STEP = 8
VMEM_LIMIT = 56 * 1024 * 1024


def _params(sem):
    return pltpu.CompilerParams(dimension_semantics=sem, vmem_limit_bytes=VMEM_LIMIT)


def _rms(x, g):
    return x * lax.rsqrt(jnp.mean(x * x, axis=-1, keepdims=True) + EPS) * g


def _dot(a, b):
    return jnp.dot(a, b, preferred_element_type=F32)


def _dot_nt(a, b):
    return lax.dot_general(a, b, (((1,), (1,)), ((), ())), preferred_element_type=F32)


def _gmlp_kernel(x_ref, g_ref, wuv_ref, gv_ref, ws_ref, bias_ref, wo_ref, out_ref, *rest, chunked, tm):
    x = x_ref[...]
    xn = _rms(x, g_ref[...]).astype(BF16)
    uv = _dot(xn, wuv_ref[...])
    u = uv[:, :D_MODEL]
    v = _rms(uv[:, D_MODEL:], gv_ref[...])
    if chunked:
        gated_ref = rest[0]
        row = lax.broadcasted_iota(jnp.int32, (CHUNK, CHUNK), 0)
        col = lax.broadcasted_iota(jnp.int32, (CHUNK, CHUNK), 1)
        for g in range(A_GROUPS):
            cs = slice(g * A_GROUP_DIM, (g + 1) * A_GROUP_DIM)
            w = jnp.where(row >= col, ws_ref[g], 0.0).astype(BF16)
            for c in range(tm // CHUNK):
                rs = slice(c * CHUNK, (c + 1) * CHUNK)
                s = _dot(w, v[rs, cs].astype(BF16)) + bias_ref[:, cs]
                gated_ref[rs, cs] = (u[rs, cs] * s).astype(BF16)
        gated = gated_ref[...]
    else:
        v_ref = rest[0]
        v_ref[...] = v
        gated = (u * (v * ws_ref[...] + bias_ref[...])).astype(BF16)
    out_ref[...] = x + _dot(gated, wo_ref[...])


def _gmlp_layer(h, g, w_uv, gv, w_s, b_s, w_o, *, chunked):
    m = h.shape[0]
    full = lambda shape: pl.BlockSpec(shape, lambda i: (0,) * len(shape))
    if chunked:
        tm = 256
        ws = w_s
        bias = jnp.repeat(b_s.T, A_GROUP_DIM, axis=1)
        ws_spec = full((A_GROUPS, CHUNK, CHUNK))
        bias_spec = full((CHUNK, D_MODEL))
        out_shape = jax.ShapeDtypeStruct((m, D_MODEL), F32)
        out_specs = pl.BlockSpec((tm, D_MODEL), lambda i: (i, 0))
        scratch = [pltpu.VMEM((tm, D_MODEL), BF16)]
    else:
        tm = m
        ws = jnp.repeat(w_s[:, 0, 0], A_GROUP_DIM)[None, :]
        bias = jnp.repeat(b_s[:, 0], A_GROUP_DIM)[None, :]
        ws_spec = full((1, D_MODEL))
        bias_spec = full((1, D_MODEL))
        out_shape = (jax.ShapeDtypeStruct((m, D_MODEL), F32), jax.ShapeDtypeStruct((m, D_MODEL), F32))
        out_specs = (pl.BlockSpec((tm, D_MODEL), lambda i: (i, 0)), pl.BlockSpec((tm, D_MODEL), lambda i: (i, 0)))
        scratch = []
    assert m % tm == 0
    return pl.pallas_call(
        functools.partial(_gmlp_kernel, chunked=chunked, tm=tm),
        grid=(m // tm,),
        in_specs=[pl.BlockSpec((tm, D_MODEL), lambda i: (i, 0)), full((1, D_MODEL)),
                  full((D_MODEL, 2 * D_MODEL)), full((1, D_MODEL)), ws_spec, bias_spec,
                  full((D_MODEL, D_MODEL))],
        out_specs=out_specs, out_shape=out_shape, scratch_shapes=scratch,
        compiler_params=_params(("parallel",)),
        name="gmlp_prompt" if chunked else "gmlp_sample",
    )(h, g[None, :], w_uv.astype(BF16), gv[None, :], ws, bias, w_o.astype(BF16))


def _mlp_kernel(*refs, has_attn, has_final):
    refs = list(refs)
    x_ref = refs.pop(0)
    attn_ref = refs.pop(0) if has_attn else None
    wo_ref = refs.pop(0) if has_attn else None
    g_ref, wup_ref, wdn_ref = refs.pop(0), refs.pop(0), refs.pop(0)
    fg_ref = refs.pop(0) if has_final else None
    out_ref, h_ref, xn_ref, acc_ref = refs
    j = pl.program_id(1)

    @pl.when(j == 0)
    def _():
        h = x_ref[...]
        if has_attn:
            h = h + _dot(attn_ref[...], wo_ref[...])
        h_ref[...] = h
        xn_ref[...] = _rms(h, g_ref[...]).astype(BF16)
        acc_ref[...] = jnp.zeros_like(acc_ref)

    a = _dot(xn_ref[...], wup_ref[...])
    a = jnp.square(jnp.maximum(a, 0.0)).astype(BF16)
    acc_ref[...] += _dot(a, wdn_ref[...])

    @pl.when(j == pl.num_programs(1) - 1)
    def _():
        h = h_ref[...] + acc_ref[...]
        if has_final:
            h = _rms(h, fg_ref[...])
        out_ref[...] = h


def _mlp_layer(h, g, w_up, w_down, *, attn=None, w_o=None, final_g=None):
    m = h.shape[0]
    tm = min(m, 1024)
    tf = 512
    assert m % tm == 0 and D_FF % tf == 0
    has_attn, has_final = attn is not None, final_g is not None
    row = pl.BlockSpec((tm, D_MODEL), lambda i, j: (i, 0))
    vec = pl.BlockSpec((1, D_MODEL), lambda i, j: (0, 0))
    args, specs = [h], [row]
    if has_attn:
        args += [attn, w_o.astype(BF16)]
        specs += [row, pl.BlockSpec((D_MODEL, D_MODEL), lambda i, j: (0, 0))]
    args += [g[None, :], w_up.astype(BF16), w_down.astype(BF16)]
    specs += [vec, pl.BlockSpec((D_MODEL, tf), lambda i, j: (0, j)), pl.BlockSpec((tf, D_MODEL), lambda i, j: (j, 0))]
    if has_final:
        args.append(final_g[None, :])
        specs.append(vec)
    return pl.pallas_call(
        functools.partial(_mlp_kernel, has_attn=has_attn, has_final=has_final),
        grid=(m // tm, D_FF // tf),
        in_specs=specs, out_specs=row,
        out_shape=jax.ShapeDtypeStruct((m, D_MODEL), F32),
        scratch_shapes=[pltpu.VMEM((tm, D_MODEL), F32), pltpu.VMEM((tm, D_MODEL), BF16),
                        pltpu.VMEM((tm, D_MODEL), F32)],
        compiler_params=_params(("parallel", "arbitrary")),
        name="mlp",
    )(*args)


def _norm_matmul_kernel(x_ref, g_ref, w_ref, *out_refs):
    xn = _rms(x_ref[...], g_ref[...]).astype(BF16)
    y = _dot(xn, w_ref[...])
    wd = y.shape[1] // len(out_refs)
    for k, o_ref in enumerate(out_refs):
        o_ref[...] = y[:, k * wd:(k + 1) * wd]


def _norm_matmul(h, g, w, n_out):
    m, n = h.shape[0], w.shape[1]
    tm = min(m, 512)
    wd = n // n_out
    assert m % tm == 0 and n % n_out == 0 and wd % LANES == 0
    outs = pl.pallas_call(
        _norm_matmul_kernel,
        grid=(m // tm,),
        in_specs=[pl.BlockSpec((tm, D_MODEL), lambda i: (i, 0)), pl.BlockSpec((1, D_MODEL), lambda i: (0, 0)),
                  pl.BlockSpec((D_MODEL, n), lambda i: (0, 0))],
        out_specs=tuple(pl.BlockSpec((tm, wd), lambda i: (i, 0)) for _ in range(n_out)),
        out_shape=tuple(jax.ShapeDtypeStruct((m, wd), F32) for _ in range(n_out)),
        compiler_params=_params(("parallel",)),
        name="norm_matmul",
    )(h, g[None, :], w.astype(BF16))
    return outs


def _compress_kernel(pt_ref, *refs, npg):
    page_refs = refs[:npg]
    w1_ref, pe_ref, w2_ref, out_ref, z_ref, aprev_ref = refs[npg:]
    j = pl.program_id(1)
    grp = PAGE_SIZE // CMP_STRIDE
    sec = grp * npg
    rows = N_KV * sec

    @pl.when(j == 0)
    def _():
        aprev_ref[...] = jnp.zeros_like(aprev_ref)

    ncol = 2 * KV_COLS // LANES
    for t in range(npg):
        for q in range(CMP_STRIDE // 2):
            for c in range(ncol):
                xa = page_refs[t][0, pl.ds(ncol * (2 * q) + c, grp, stride=ncol * CMP_STRIDE), :]
                xb = page_refs[t][0, pl.ds(ncol * (2 * q + 1) + c, grp, stride=ncol * CMP_STRIDE), :]
                kv = c // (ncol // 2)
                for half in range(LANES // HEAD_DIM):
                    g = (c % (ncol // 2)) * (LANES // HEAD_DIM) + half
                    hs = slice(half * HEAD_DIM, (half + 1) * HEAD_DIM)
                    piece = jnp.concatenate([xa[:, hs], xb[:, hs]], axis=1)
                    z_ref[kv, g * sec + t * grp:g * sec + (t + 1) * grp, q * LANES:(q + 1) * LANES] = piece

    rid = lax.broadcasted_iota(jnp.int32, (rows, CMP_HIDDEN), 0) % sec
    for kv in range(2):
        z = z_ref[kv]
        za = (z + pe_ref[kv, 0:1, :]).astype(BF16)
        zb = (z + pe_ref[kv, 1:2, :]).astype(BF16)
        a = _dot(za, w1_ref[kv, :, :CMP_HIDDEN])
        b = _dot(zb, w1_ref[kv, :, CMP_HIDDEN:])
        a_prev = jnp.where(rid == 0, pltpu.roll(aprev_ref[kv], rows - (sec - 1), 0), pltpu.roll(a, 1, 0))
        aprev_ref[kv] = a
        hid = jax.nn.gelu(a_prev + b).astype(BF16)
        o = _dot(hid, w2_ref[kv])
        out_ref[0, kv] = jnp.concatenate([o[g * sec:(g + 1) * sec] for g in range(N_KV)], axis=1)


def _compress(pages, page_table, pe_k, w1_k, w2_k, pe_v, w1_v, w2_v):
    n_seq, n_pages = page_table.shape
    npg = PAGES_PER_STEP
    assert n_pages % npg == 0
    grp = PAGE_SIZE // CMP_STRIDE
    half = CMP_STRIDE * HEAD_DIM

    def split(w1):
        return jnp.concatenate([w1[:half], w1[half:]], axis=1)

    w1 = jnp.stack([split(w1_k), split(w1_v)]).astype(BF16)
    pe = jnp.stack([pe_k.reshape(2, half), pe_v.reshape(2, half)])
    w2 = jnp.stack([w2_k, w2_v]).astype(BF16)

    ncol = 2 * KV_COLS // LANES
    pages = pages.reshape(pages.shape[0], ncol * PAGE_SIZE, LANES)

    def page_spec(t):
        return pl.BlockSpec((1, ncol * PAGE_SIZE, LANES), lambda s, j, pt: (pt[s * n_pages + j * npg + t], 0, 0))

    const = lambda shape: pl.BlockSpec(shape, lambda s, j, pt: (0,) * len(shape))
    return pl.pallas_call(
        functools.partial(_compress_kernel, npg=npg),
        grid_spec=pltpu.PrefetchScalarGridSpec(
            num_scalar_prefetch=1,
            grid=(n_seq, n_pages // npg),
            in_specs=[page_spec(t) for t in range(npg)] + [const((2, half, 2 * CMP_HIDDEN)), const((2, 2, half)),
                                                           const((2, CMP_HIDDEN, HEAD_DIM))],
            out_specs=pl.BlockSpec((1, 2, grp * npg, KV_COLS), lambda s, j, pt: (s, 0, j, 0)),
            scratch_shapes=[pltpu.VMEM((2, N_KV * grp * npg, half), F32),
                            pltpu.VMEM((2, N_KV * grp * npg, CMP_HIDDEN), F32)]),
        out_shape=jax.ShapeDtypeStruct((n_seq, 2, n_pages * grp, KV_COLS), F32),
        compiler_params=_params(("parallel", "arbitrary")),
        name="compress",
    )(page_table.reshape(-1), *([pages] * npg), w1, pe, w2)


def _selection_map(n_rows, n_blk, n_cols):
    i = np.arange(n_rows)[:, None] - 1
    j = np.arange(n_cols)[None, :]
    lo = np.maximum(i * CMP_STRIDE, j * SEL_LEN)
    hi = np.minimum(i * CMP_STRIDE + CMP_LEN, (j + 1) * SEL_LEN)
    m = np.maximum(hi - lo, 0) // CMP_STRIDE
    m = np.where((i >= 0) & (j < n_blk), m, 0)
    return jnp.asarray(m, dtype=BF16)


def _top_blocks_along_rows(score, n_rows):
    ridx = lax.broadcasted_iota(jnp.int32, score.shape, 0)
    work = score
    sel = jnp.zeros(score.shape, F32)
    for _ in range(N_SEL):
        mx = jnp.max(work, axis=0, keepdims=True)
        first = jnp.min(jnp.where(work == mx, ridx, n_rows), axis=0, keepdims=True)
        hit = ridx == first
        sel = jnp.where(hit, 1.0, sel)
        work = jnp.where(hit, -3e38, work)
    return sel


def _nsa_prompt_kernel(zq_ref, zg_ref, bg_ref, kct_ref, vca_ref, smap_ref, ksa_ref, vsa_ref, kwt_ref, vwa_ref,
                       out_ref):
    n = pl.program_id(2)
    s0 = n * Q_BLOCK
    rows = Q_PER_KV * Q_BLOCK
    q = zq_ref[...] * SCALE
    qs = jnp.concatenate([q[:, r * HEAD_DIM:(r + 1) * HEAD_DIM] for r in range(Q_PER_KV)], axis=0).astype(BF16)
    i_row = lax.broadcasted_iota(jnp.int32, (rows, 1), 0) & (Q_BLOCK - 1)
    qpos = s0 + i_row

    n_c = kct_ref.shape[3]
    sc = _dot(qs, kct_ref[0, 0])
    cidx = lax.broadcasted_iota(jnp.int32, (rows, n_c), 1)
    mc = (cidx >= 1) & (cidx * CMP_STRIDE + (CMP_STRIDE - 1) <= qpos)
    scm = jnp.where(mc, sc, NEG_INF)
    p = jnp.where(mc, jnp.exp(scm - jnp.max(scm, axis=1, keepdims=True)), 0.0)
    acc_c = _dot(p.astype(BF16), vca_ref[0, 0])
    l_c = acc_c[:, HEAD_DIM:HEAD_DIM + 1]
    o_c = acc_c[:, :HEAD_DIM] * jnp.where(l_c > 0.0, 1.0 / l_c, 0.0)
    l32 = jnp.sum(p, axis=1, keepdims=True)
    pn = p * jnp.where(l32 > 0.0, 1.0 / l32, 0.0)
    psum = pn[0:Q_BLOCK]
    for r in range(1, Q_PER_KV):
        psum = psum + pn[r * Q_BLOCK:(r + 1) * Q_BLOCK]
    p_hi = psum.astype(BF16)
    p_lo = (psum - p_hi.astype(F32)).astype(BF16)
    imp = _dot(p_hi, smap_ref[...]) + _dot(p_lo, smap_ref[...])

    imp_t = imp.T
    nb = imp_t.shape[0]
    jt = lax.broadcasted_iota(jnp.int32, (nb, Q_BLOCK), 0)
    cur = (s0 + lax.broadcasted_iota(jnp.int32, (nb, Q_BLOCK), 1)) // SEL_LEN
    forced = (jt == 0) | (jt == cur) | (jt == cur - 1)
    score = jnp.where(forced, FORCE_SCORE, jnp.where(jt <= cur, imp_t, -1.0))
    sel = _top_blocks_along_rows(score, nb)
    bias_t = jnp.where((sel > 0.0) & (score >= 0.0), 0.0, NEG_INF)
    bias = bias_t.T.astype(BF16)
    qaug = jnp.concatenate([jnp.concatenate([bias] * Q_PER_KV, axis=0), qs], axis=1)

    def tile(t, carry, causal):
        m, acc = carry
        k0 = pl.multiple_of(t * SEL_TILE, SEL_TILE)
        s = _dot(qaug, ksa_ref[0, 0, :, pl.ds(k0, SEL_TILE)])
        if causal:
            kpos = k0 + lax.broadcasted_iota(jnp.int32, (rows, SEL_TILE), 1)
            s = jnp.where(kpos <= qpos, s, NEG_INF)
        m_new = jnp.maximum(m, jnp.max(s, axis=1, keepdims=True))
        pt = jnp.exp(s - m_new).astype(BF16)
        acc = jnp.exp(m - m_new) * acc + _dot(pt, vsa_ref[0, 0, pl.ds(k0, SEL_TILE), :])
        return m_new, acc

    t_last = s0 // SEL_TILE
    carry = (jnp.full((rows, 1), NEG_INF, F32), jnp.zeros((rows, LANES), F32))
    carry = lax.fori_loop(0, t_last, lambda t, c: tile(t, c, False), carry)
    _, acc_s = tile(t_last, carry, True)
    o_s = acc_s[:, :HEAD_DIM] / acc_s[:, HEAD_DIM:HEAD_DIM + 1]

    wk = WINDOW + Q_BLOCK
    w0 = pl.multiple_of(s0, Q_BLOCK)
    sw = _dot(qs, kwt_ref[0, 0, :, pl.ds(w0, wk)])
    rel = lax.broadcasted_iota(jnp.int32, (rows, wk), 1) - WINDOW
    mw = (rel <= i_row) & (rel > i_row - WINDOW) & (rel + s0 >= 0)
    swm = jnp.where(mw, sw, NEG_INF)
    pw = jnp.exp(swm - jnp.max(swm, axis=1, keepdims=True)).astype(BF16)
    acc_w = _dot(pw, vwa_ref[0, 0, pl.ds(w0, wk), :])
    o_w = acc_w[:, :HEAD_DIM] / acc_w[:, HEAD_DIM:HEAD_DIM + 1]

    gate = 1.0 / (1.0 + jnp.exp(-(zg_ref[...] + bg_ref[...])))
    outs = []
    for r in range(Q_PER_KV):
        rs = slice(r * Q_BLOCK, (r + 1) * Q_BLOCK)
        outs.append(gate[:, r:r + 1] * o_c[rs] + gate[:, Q_PER_KV + r:Q_PER_KV + r + 1] * o_s[rs]
                    + gate[:, 2 * Q_PER_KV + r:2 * Q_PER_KV + r + 1] * o_w[rs])
    out_ref[...] = jnp.concatenate(outs, axis=1).astype(BF16)


def _kv_layouts(rows, b, t, left_pad):
    k = rows[:, :KV_COLS].reshape(b, t, N_KV, HEAD_DIM).astype(BF16)
    v = rows[:, KV_COLS:].reshape(b, t, N_KV, HEAD_DIM).astype(BF16)
    kt = jnp.transpose(k, (0, 2, 3, 1))
    va = jnp.concatenate([jnp.transpose(v, (0, 2, 1, 3)), jnp.ones((b, N_KV, t, 1), BF16),
                          jnp.zeros((b, N_KV, t, LANES - HEAD_DIM - 1), BF16)], axis=3)
    if left_pad:
        kt = jnp.pad(kt, ((0, 0), (0, 0), (0, 0), (left_pad, 0)))
        va = jnp.pad(va, ((0, 0), (0, 0), (left_pad, 0), (0, 0)))
    return kt, va


def _nsa_prompt(z, bg_pad, kvc, sel_rows, win_rows, b, t):
    nq = t // Q_BLOCK
    n_c = kvc.shape[2]
    n_blk = t // SEL_LEN
    assert t % SEL_TILE == 0 and N_SEL <= n_blk <= LANES
    kc = kvc[:, 0].reshape(b, n_c, N_KV, HEAD_DIM).astype(BF16)
    vc = kvc[:, 1].reshape(b, n_c, N_KV, HEAD_DIM).astype(BF16)
    kct = jnp.transpose(kc, (0, 2, 3, 1))
    vca = jnp.concatenate([jnp.transpose(vc, (0, 2, 1, 3)), jnp.ones((b, N_KV, n_c, 1), BF16),
                           jnp.zeros((b, N_KV, n_c, LANES - HEAD_DIM - 1), BF16)], axis=3)
    smap = _selection_map(n_c, n_blk, LANES)
    kst, vsa = _kv_layouts(sel_rows, b, t, 0)
    blk_of_key = np.arange(t)[None, :] // SEL_LEN == np.arange(LANES)[:, None]
    ksa = jnp.concatenate([jnp.broadcast_to(jnp.asarray(blk_of_key, BF16), (b, N_KV, LANES, t)), kst], axis=2)
    kwt, vwa = _kv_layouts(win_rows, b, t, WINDOW)
    per_bg = lambda shape: pl.BlockSpec((1, 1) + shape, lambda bb, g, n: (bb, g, 0, 0))
    return pl.pallas_call(
        _nsa_prompt_kernel,
        grid=(b, N_KV, nq),
        in_specs=[pl.BlockSpec((Q_BLOCK, Q_PER_KV * HEAD_DIM), lambda bb, g, n: (bb * nq + n, g)),
                  pl.BlockSpec((Q_BLOCK, LANES), lambda bb, g, n: (bb * nq + n, D_MODEL // LANES + g)),
                  pl.BlockSpec((1, LANES), lambda bb, g, n: (0, g)),
                  per_bg((HEAD_DIM, n_c)), per_bg((n_c, LANES)),
                  pl.BlockSpec((n_c, LANES), lambda bb, g, n: (0, 0)),
                  per_bg((HEAD_DIM + LANES, t)), per_bg((t, LANES)),
                  per_bg((HEAD_DIM, WINDOW + t)), per_bg((WINDOW + t, LANES))],
        out_specs=pl.BlockSpec((Q_BLOCK, Q_PER_KV * HEAD_DIM), lambda bb, g, n: (bb * nq + n, g)),
        out_shape=jax.ShapeDtypeStruct((b * t, D_MODEL), BF16),
        compiler_params=_params(("parallel", "parallel", "arbitrary")),
        name="nsa_prompt",
    )(z, z, bg_pad, kct, vca, smap, ksa, vsa, kwt, vwa)


SROWS = 32
SBLK = 256


def _samp_cmpwin_kernel(qbd_ref, kvc_ref, smap_ref, win_ref, wnew_ref, oc_ref, ow_ref, imp_ref, *, qpos):
    qf = qbd_ref[0] * SCALE
    qb = qf.astype(BF16)
    kc = kvc_ref[0, 0].astype(BF16)
    vc = kvc_ref[0, 1].astype(BF16)
    n_c = kc.shape[0]
    sc = _dot_nt(qb, kc)
    cidx = lax.broadcasted_iota(jnp.int32, (SROWS, n_c), 1)
    mc = (cidx >= 1) & (cidx * CMP_STRIDE + (CMP_STRIDE - 1) <= qpos)
    scm = jnp.where(mc, sc, NEG_INF)
    p = jnp.where(mc, jnp.exp(scm - jnp.max(scm, axis=1, keepdims=True)), 0.0)
    l = jnp.sum(p, axis=1, keepdims=True)
    inv = jnp.where(l > 0.0, 1.0 / l, 0.0)
    oc_ref[0] = _dot(p.astype(BF16), vc) * inv
    pn = p * inv
    psum = pn[0:8] + pn[8:16] + pn[16:24] + pn[24:32]
    p_hi = psum.astype(BF16)
    p_lo = (psum - p_hi.astype(F32)).astype(BF16)
    imp_ref[0] = _dot(p_hi, smap_ref[...]) + _dot(p_lo, smap_ref[...])
    win = win_ref[0]
    n_w = win.shape[0]
    sw = _dot_nt(qb, win[:, :KV_COLS].astype(BF16))
    widx = lax.broadcasted_iota(jnp.int32, (SROWS, n_w), 1)
    dist = n_w - widx
    mw = (dist < WINDOW) & (qpos - dist >= 0)
    swm = jnp.where(mw, sw, NEG_INF)
    wnew = wnew_ref[0]
    s_new = jnp.sum(qb.astype(F32) * wnew[:, :KV_COLS].astype(BF16).astype(F32), axis=1, keepdims=True)
    m = jnp.maximum(jnp.max(swm, axis=1, keepdims=True), s_new)
    pw = jnp.exp(swm - m)
    p_new = jnp.exp(s_new - m)
    lw = jnp.sum(pw, axis=1, keepdims=True) + p_new
    ow = _dot(pw.astype(BF16), win[:, KV_COLS:].astype(BF16)) + p_new * wnew[:, KV_COLS:]
    ow_ref[0] = ow / lw


def _samp_topk_kernel(imp_ref, bias_ref, *, cur):
    imp = imp_ref[...]
    jidx = lax.broadcasted_iota(jnp.int32, imp.shape, 1)
    forced = (jidx == 0) | (jidx == cur) | (jidx == cur - 1)
    score = jnp.where(forced, FORCE_SCORE, jnp.where(jidx <= cur, imp, -1.0))
    work = score
    sel = jnp.zeros(imp.shape, F32)
    for _ in range(N_SEL):
        mx = jnp.max(work, axis=1, keepdims=True)
        first = jnp.min(jnp.where(work == mx, jidx, SBLK), axis=1, keepdims=True)
        hit = jidx == first
        sel = jnp.where(hit, 1.0, sel)
        work = jnp.where(hit, -3e38, work)
    bias_ref[...] = jnp.where((sel > 0.0) & (score >= 0.0), 0.0, NEG_INF)


def _samp_sel_kernel(pt_ref, *refs, npg, qpos):
    page_refs = refs[:npg]
    (qbd_ref, bias_ref, knew_ref, oc_ref, ow_ref, zg_ref, bg_ref, out_ref, m_ref, acc_ref, l_ref) = refs[npg:]
    j = pl.program_id(1)
    keys = npg * PAGE_SIZE
    qb = (qbd_ref[0] * SCALE).astype(BF16)

    @pl.when(j == 0)
    def _():
        m_ref[...] = jnp.full_like(m_ref, NEG_INF)
        acc_ref[...] = jnp.zeros_like(acc_ref)
        l_ref[...] = jnp.zeros_like(l_ref)

    bias8 = bias_ref[0].astype(BF16)
    bias32 = jnp.concatenate([bias8] * Q_PER_KV, axis=0)
    blk = lax.broadcasted_iota(jnp.int32, (SBLK, keys), 0)
    key = lax.broadcasted_iota(jnp.int32, (SBLK, keys), 1)
    onehot = jnp.where(blk == j * (keys // SEL_LEN) + key // SEL_LEN, 1.0, 0.0).astype(BF16)
    s = jnp.concatenate([_dot_nt(qb, page_refs[t][0, :, :KV_COLS].astype(BF16)) for t in range(npg)], axis=1)
    s = s + _dot(bias32, onehot)
    m_old = m_ref[...]
    m_new = jnp.maximum(m_old, jnp.max(s, axis=1, keepdims=True))
    p = jnp.exp(s - m_new)
    alpha = jnp.exp(m_old - m_new)
    pv = _dot(p[:, :PAGE_SIZE].astype(BF16), page_refs[0][0, :, KV_COLS:].astype(BF16))
    for t in range(1, npg):
        pv = pv + _dot(p[:, t * PAGE_SIZE:(t + 1) * PAGE_SIZE].astype(BF16),
                       page_refs[t][0, :, KV_COLS:].astype(BF16))
    m_ref[...] = m_new
    l_ref[...] = alpha * l_ref[...] + jnp.sum(p, axis=1, keepdims=True)
    acc_ref[...] = alpha * acc_ref[...] + pv

    @pl.when(j == pl.num_programs(1) - 1)
    def _():
        knew = knew_ref[0]
        s_new = jnp.sum(qb.astype(F32) * knew[:, :KV_COLS].astype(BF16).astype(F32), axis=1, keepdims=True)
        m_fin = jnp.maximum(m_ref[...], s_new)
        a = jnp.exp(m_ref[...] - m_fin)
        p_new = jnp.exp(s_new - m_fin)
        os_ = (a * acc_ref[...] + p_new * knew[:, KV_COLS:]) / (a * l_ref[...] + p_new)
        gate = 1.0 / (1.0 + jnp.exp(-(zg_ref[0] + bg_ref[0])))
        o = gate[:, 0:1] * oc_ref[0] + gate[:, 1:2] * os_ + gate[:, 2:3] * ow_ref[0]
        gsel = lax.broadcasted_iota(jnp.int32, (SROWS, HEAD_DIM), 0) & 7
        res = jnp.zeros((SROWS, HEAD_DIM), F32)
        for g in range(N_KV):
            res = res + jnp.where(gsel == g, o[:, g * HEAD_DIM:(g + 1) * HEAD_DIM], 0.0)
        out_ref[0] = res


def _nsa_sample(z, b_g, kvc, cache_sel, page_table, sel_new, win_state, win_new, past_len):
    n_seq, n_pages = page_table.shape
    npg = PAGES_PER_STEP
    qpos = past_len
    n_blk = -(-(past_len + 1) // SEL_LEN)
    assert n_blk <= SBLK and n_pages % npg == 0
    n_c = kvc.shape[2]
    n_w = win_state.shape[1]
    q4 = z[:, :D_MODEL].reshape(n_seq, N_KV, Q_PER_KV, HEAD_DIM)
    eye = jnp.eye(N_KV, dtype=F32)
    qbd = jnp.transpose(q4[:, :, :, None, :] * eye[None, :, None, :, None], (0, 2, 1, 3, 4))
    qbd = jnp.pad(qbd, ((0, 0), (0, 0), (0, 8 - N_KV), (0, 0), (0, 0))).reshape(n_seq, SROWS, KV_COLS)
    def gate_rows(a):
        n = a.shape[0]
        a = jnp.transpose(a.reshape(n, N_KV, Q_PER_KV, 3), (0, 2, 1, 3))
        return jnp.pad(a, ((0, 0), (0, 0), (0, 8 - N_KV), (0, LANES - 3))).reshape(n, SROWS, LANES)

    zg = gate_rows(z[:, D_MODEL:D_MODEL + 3 * N_HEADS])
    bg = gate_rows(b_g[None, :])
    smap = _selection_map(n_c, n_blk, SBLK)
    per_seq = lambda shape: pl.BlockSpec((1,) + shape, lambda s: (s,) + (0,) * len(shape))
    o_c, o_w, imp = pl.pallas_call(
        functools.partial(_samp_cmpwin_kernel, qpos=qpos),
        grid=(n_seq,),
        in_specs=[per_seq((SROWS, KV_COLS)), per_seq((2, n_c, KV_COLS)),
                  pl.BlockSpec((n_c, SBLK), lambda s: (0, 0)), per_seq((n_w, 2 * KV_COLS)),
                  per_seq((1, 2 * KV_COLS))],
        out_specs=(per_seq((SROWS, KV_COLS)), per_seq((SROWS, KV_COLS)), per_seq((8, SBLK))),
        out_shape=(jax.ShapeDtypeStruct((n_seq, SROWS, KV_COLS), F32),
                   jax.ShapeDtypeStruct((n_seq, SROWS, KV_COLS), F32),
                   jax.ShapeDtypeStruct((n_seq, 8, SBLK), F32)),
        compiler_params=_params(("parallel",)),
        name="nsa_sample_cmpwin",
    )(qbd, kvc, smap, win_state, win_new[:, None, :])
    bias = pl.pallas_call(
        functools.partial(_samp_topk_kernel, cur=qpos // SEL_LEN),
        out_shape=jax.ShapeDtypeStruct((n_seq * 8, SBLK), F32),
        name="nsa_sample_topk",
    )(imp.reshape(n_seq * 8, SBLK)).reshape(n_seq, 8, SBLK)

    def page_spec(t):
        return pl.BlockSpec((1, PAGE_SIZE, 2 * KV_COLS), lambda s, j, pt: (pt[s * n_pages + j * npg + t], 0, 0))

    seq_blk = lambda shape: pl.BlockSpec((1,) + shape, lambda s, j, pt: (s,) + (0,) * len(shape))
    out = pl.pallas_call(
        functools.partial(_samp_sel_kernel, npg=npg, qpos=qpos),
        grid_spec=pltpu.PrefetchScalarGridSpec(
            num_scalar_prefetch=1,
            grid=(n_seq, n_pages // npg),
            in_specs=[page_spec(t) for t in range(npg)] + [
                seq_blk((SROWS, KV_COLS)), seq_blk((8, SBLK)), seq_blk((1, 2 * KV_COLS)),
                seq_blk((SROWS, KV_COLS)), seq_blk((SROWS, KV_COLS)), seq_blk((SROWS, LANES)),
                pl.BlockSpec((1, SROWS, LANES), lambda s, j, pt: (0, 0, 0))],
            out_specs=seq_blk((SROWS, HEAD_DIM)),
            scratch_shapes=[pltpu.VMEM((SROWS, 1), F32), pltpu.VMEM((SROWS, KV_COLS), F32),
                            pltpu.VMEM((SROWS, 1), F32)]),
        out_shape=jax.ShapeDtypeStruct((n_seq, SROWS, HEAD_DIM), F32),
        compiler_params=_params(("parallel", "arbitrary")),
        name="nsa_sample_sel",
    )(page_table.reshape(-1), *([cache_sel] * npg), qbd, bias, sel_new[:, None, :], o_c, o_w, zg, bg)
    out = out.reshape(n_seq, Q_PER_KV, 8, HEAD_DIM)[:, :, :N_KV]
    return jnp.transpose(out, (0, 2, 1, 3)).reshape(n_seq, D_MODEL).astype(BF16)


def kernel(x_prompt, x_sample, cache_cmp_kv, cache_sel_kv, state_win_kv, page_table, a_norm_g, a_w_uv, a_v_norm_g, a_w_s, a_b_s, a_w_o, kv_norm_g, w_kv, cmp_pe_k, cmp_w1_k, cmp_w2_k, cmp_pe_v, cmp_w1_v, cmp_w2_v, b_norm_g, b_w_qg, b_b_g, b_w_o, mlp_norm_g, mlp_w_up, mlp_w_down, final_norm_g):
    b, t, _ = x_prompt.shape
    n_seq, dec_seq, _ = x_sample.shape
    n_pool, page, _, _, _ = cache_cmp_kv.shape
    n_pages = page_table.shape[1]
    past_len = n_pages * page
    win_buf = state_win_kv.shape[1]
    depth = mlp_norm_g.shape[0]
    n_a = a_norm_g.shape[0]
    assert dec_seq == 1 and page == PAGE_SIZE and t % PAGE_SIZE == 0
    cmp_w = (cmp_pe_k, cmp_w1_k, cmp_w2_k, cmp_pe_v, cmp_w1_v, cmp_w2_v)
    kv_shape = (2, N_KV, HEAD_DIM)
    gate_w = 3 * N_HEADS

    hp = x_prompt.reshape(b * t, D_MODEL)
    hs = x_sample.reshape(n_seq, D_MODEL)
    a_v_rows = []
    attn_p = attn_s = None
    for layer in range(depth):
        last = layer == depth - 1
        fin = final_norm_g if last else None
        if layer < n_a:
            i = layer
            hp = _gmlp_layer(hp, a_norm_g[i], a_w_uv[i], a_v_norm_g[i], a_w_s[i], a_b_s[i], a_w_o[i], chunked=True)
            hs, v_s = _gmlp_layer(hs, a_norm_g[i], a_w_uv[i], a_v_norm_g[i], a_w_s[i], a_b_s[i], a_w_o[i],
                                  chunked=False)
            a_v_rows.append(v_s.reshape(n_seq, 1, D_MODEL))
            hp = _mlp_layer(hp, mlp_norm_g[layer], mlp_w_up[layer], mlp_w_down[layer], final_g=fin)
            hs = _mlp_layer(hs, mlp_norm_g[layer], mlp_w_up[layer], mlp_w_down[layer], final_g=fin)
            continue
        if layer == n_a:
            cmp_p, sel_p, win_p = _norm_matmul(hp, kv_norm_g, w_kv, 3)
            cmp_s, sel_s, win_s = _norm_matmul(hs, kv_norm_g, w_kv, 3)
            kvc_p = _compress(cmp_p.reshape(b * t // PAGE_SIZE, PAGE_SIZE, 2 * KV_COLS),
                              jnp.arange(b * t // PAGE_SIZE, dtype=jnp.int32).reshape(b, t // PAGE_SIZE), *cmp_w)
            kvc_s = _compress(cache_cmp_kv.reshape(n_pool, PAGE_SIZE, 2 * KV_COLS), page_table, *cmp_w)
            cache_sel = cache_sel_kv.reshape(n_pool, PAGE_SIZE, 2 * KV_COLS)
            win_state = state_win_kv.reshape(n_seq, win_buf, 2 * KV_COLS)
        j = layer - n_a
        wg = b_w_qg[j][:, D_MODEL:].reshape(D_MODEL, N_KV, Q_PER_KV, 3)
        wg = jnp.transpose(wg, (0, 1, 3, 2)).reshape(D_MODEL, N_KV, 3 * Q_PER_KV)
        wg = jnp.pad(wg, ((0, 0), (0, 0), (0, LANES - 3 * Q_PER_KV))).reshape(D_MODEL, N_KV * LANES)
        bg = jnp.transpose(b_b_g[j].reshape(N_KV, Q_PER_KV, 3), (0, 2, 1)).reshape(N_KV, 3 * Q_PER_KV)
        bg = jnp.pad(bg, ((0, 0), (0, LANES - 3 * Q_PER_KV))).reshape(1, N_KV * LANES)
        (z_p,) = _norm_matmul(hp, b_norm_g[j], jnp.concatenate([b_w_qg[j][:, :D_MODEL], wg], axis=1), 1)
        attn_p = _nsa_prompt(z_p, bg, kvc_p, sel_p, win_p, b, t)
        w_s_pad = jnp.pad(b_w_qg[j], ((0, 0), (0, LANES - gate_w % LANES)))
        (z_s,) = _norm_matmul(hs, b_norm_g[j], w_s_pad, 1)
        attn_s = _nsa_sample(z_s, b_b_g[j], kvc_s, cache_sel, page_table, sel_s, win_state, win_s, past_len)
        hp = _mlp_layer(hp, mlp_norm_g[layer], mlp_w_up[layer], mlp_w_down[layer], attn=attn_p, w_o=b_w_o[j],
                        final_g=fin)
        hs = _mlp_layer(hs, mlp_norm_g[layer], mlp_w_up[layer], mlp_w_down[layer], attn=attn_s, w_o=b_w_o[j],
                        final_g=fin)

    y_prompt = hp.reshape(b, t, D_MODEL)
    y_sample = hs.reshape(n_seq, 1, D_MODEL)
    nw = min(WINDOW, t)
    win_kv_prompt = win_p.reshape(b, t, *kv_shape)[:, t - nw:]
    win_kv_sample = jnp.concatenate([state_win_kv, win_s.reshape(n_seq, 1, *kv_shape)], axis=1)[:, dec_seq:]
    return (y_prompt, y_sample, cmp_p.reshape(b, t, *kv_shape), sel_p.reshape(b, t, *kv_shape), win_kv_prompt,
            cmp_s.reshape(n_seq, 1, *kv_shape), sel_s.reshape(n_seq, 1, *kv_shape), win_kv_sample,
            jnp.stack(a_v_rows, axis=0))
```

```python
import functools

import numpy as np
import jax
import jax.numpy as jnp
from jax import lax
from jax.experimental import pallas as pl
from jax.experimental.pallas import tpu as pltpu

F32 = jnp.float32
BF16 = jnp.bfloat16

D_MODEL = 1024
D_FF = 4 * D_MODEL
CHUNK = 128
A_GROUP_DIM = 128
A_GROUPS = D_MODEL // A_GROUP_DIM
HEAD_DIM = 64
N_HEADS = D_MODEL // HEAD_DIM
N_KV = 4
Q_PER_KV = N_HEADS // N_KV
KV_COLS = N_KV * HEAD_DIM
CMP_STRIDE = 16
CMP_LEN = 2 * CMP_STRIDE
CMP_HIDDEN = 4 * HEAD_DIM
SEL_LEN = 64
N_SEL = 16
WINDOW = 512
Q_BLOCK = 128
PAGE_SIZE = 128
EPS = 1e-6
NEG_INF = -1e30
FORCE_SCORE = 1e6
SCALE = HEAD_DIM ** -0.5
LANES = 128
SEL_TILE = 512
PAGES_PER_STEP = 8
VMEM_LIMIT = 56 * 1024 * 1024


def _params(sem):
    return pltpu.CompilerParams(dimension_semantics=sem, vmem_limit_bytes=VMEM_LIMIT)


def _rms(x, g):
    return x * lax.rsqrt(jnp.mean(x * x, axis=-1, keepdims=True) + EPS) * g


def _dot(a, b):
    return jnp.dot(a, b, preferred_element_type=F32)


def _dot_nt(a, b):
    return lax.dot_general(a, b, (((1,), (1,)), ((), ())), preferred_element_type=F32)


def _gmlp_kernel(x_ref, g_ref, wuv_ref, gv_ref, ws_ref, bias_ref, wo_ref, out_ref, *rest, chunked, tm):
    x = x_ref[...]
    xn = _rms(x, g_ref[...]).astype(BF16)
    uv = _dot(xn, wuv_ref[...])
    u = uv[:, :D_MODEL]
    v = _rms(uv[:, D_MODEL:], gv_ref[...])
    if chunked:
        gated_ref = rest[0]
        row = lax.broadcasted_iota(jnp.int32, (CHUNK, CHUNK), 0)
        col = lax.broadcasted_iota(jnp.int32, (CHUNK, CHUNK), 1)
        for g in range(A_GROUPS):
            cs = slice(g * A_GROUP_DIM, (g + 1) * A_GROUP_DIM)
            w = jnp.where(row >= col, ws_ref[g], 0.0).astype(BF16)
            for c in range(tm // CHUNK):
                rs = slice(c * CHUNK, (c + 1) * CHUNK)
                s = _dot(w, v[rs, cs].astype(BF16)) + bias_ref[:, cs]
                gated_ref[rs, cs] = (u[rs, cs] * s).astype(BF16)
        gated = gated_ref[...]
    else:
        v_ref = rest[0]
        v_ref[...] = v
        gated = (u * (v * ws_ref[...] + bias_ref[...])).astype(BF16)
    out_ref[...] = x + _dot(gated, wo_ref[...])


def _gmlp_layer(h, g, w_uv, gv, w_s, b_s, w_o, *, chunked):
    m = h.shape[0]
    full = lambda shape: pl.BlockSpec(shape, lambda i: (0,) * len(shape))
    if chunked:
        tm = 256
        ws = w_s
        bias = jnp.repeat(b_s.T, A_GROUP_DIM, axis=1)
        ws_spec = full((A_GROUPS, CHUNK, CHUNK))
        bias_spec = full((CHUNK, D_MODEL))
        out_shape = jax.ShapeDtypeStruct((m, D_MODEL), F32)
        out_specs = pl.BlockSpec((tm, D_MODEL), lambda i: (i, 0))
        scratch = [pltpu.VMEM((tm, D_MODEL), BF16)]
    else:
        tm = m
        ws = jnp.repeat(w_s[:, 0, 0], A_GROUP_DIM)[None, :]
        bias = jnp.repeat(b_s[:, 0], A_GROUP_DIM)[None, :]
        ws_spec = full((1, D_MODEL))
        bias_spec = full((1, D_MODEL))
        out_shape = (jax.ShapeDtypeStruct((m, D_MODEL), F32), jax.ShapeDtypeStruct((m, D_MODEL), F32))
        out_specs = (pl.BlockSpec((tm, D_MODEL), lambda i: (i, 0)), pl.BlockSpec((tm, D_MODEL), lambda i: (i, 0)))
        scratch = []
    assert m % tm == 0
    return pl.pallas_call(
        functools.partial(_gmlp_kernel, chunked=chunked, tm=tm),
        grid=(m // tm,),
        in_specs=[pl.BlockSpec((tm, D_MODEL), lambda i: (i, 0)), full((1, D_MODEL)),
                  full((D_MODEL, 2 * D_MODEL)), full((1, D_MODEL)), ws_spec, bias_spec,
                  full((D_MODEL, D_MODEL))],
        out_specs=out_specs, out_shape=out_shape, scratch_shapes=scratch,
        compiler_params=_params(("parallel",)),
        name="gmlp_prompt" if chunked else "gmlp_sample",
    )(h, g[None, :], w_uv.astype(BF16), gv[None, :], ws, bias, w_o.astype(BF16))


def _mlp_kernel(*refs, has_attn, has_final):
    refs = list(refs)
    x_ref = refs.pop(0)
    attn_ref = refs.pop(0) if has_attn else None
    wo_ref = refs.pop(0) if has_attn else None
    g_ref, wup_ref, wdn_ref = refs.pop(0), refs.pop(0), refs.pop(0)
    fg_ref = refs.pop(0) if has_final else None
    out_ref, h_ref, xn_ref, acc_ref = refs
    j = pl.program_id(1)

    @pl.when(j == 0)
    def _():
        h = x_ref[...]
        if has_attn:
            h = h + _dot(attn_ref[...], wo_ref[...])
        h_ref[...] = h
        xn_ref[...] = _rms(h, g_ref[...]).astype(BF16)
        acc_ref[...] = jnp.zeros_like(acc_ref)

    a = _dot(xn_ref[...], wup_ref[...])
    a = jnp.square(jnp.maximum(a, 0.0)).astype(BF16)
    acc_ref[...] += _dot(a, wdn_ref[...])

    @pl.when(j == pl.num_programs(1) - 1)
    def _():
        h = h_ref[...] + acc_ref[...]
        if has_final:
            h = _rms(h, fg_ref[...])
        out_ref[...] = h


def _mlp_layer(h, g, w_up, w_down, *, attn=None, w_o=None, final_g=None):
    m = h.shape[0]
    tm = min(m, 1024)
    tf = 512
    assert m % tm == 0 and D_FF % tf == 0
    has_attn, has_final = attn is not None, final_g is not None
    row = pl.BlockSpec((tm, D_MODEL), lambda i, j: (i, 0))
    vec = pl.BlockSpec((1, D_MODEL), lambda i, j: (0, 0))
    args, specs = [h], [row]
    if has_attn:
        args += [attn, w_o.astype(BF16)]
        specs += [row, pl.BlockSpec((D_MODEL, D_MODEL), lambda i, j: (0, 0))]
    args += [g[None, :], w_up.astype(BF16), w_down.astype(BF16)]
    specs += [vec, pl.BlockSpec((D_MODEL, tf), lambda i, j: (0, j)), pl.BlockSpec((tf, D_MODEL), lambda i, j: (j, 0))]
    if has_final:
        args.append(final_g[None, :])
        specs.append(vec)
    return pl.pallas_call(
        functools.partial(_mlp_kernel, has_attn=has_attn, has_final=has_final),
        grid=(m // tm, D_FF // tf),
        in_specs=specs, out_specs=row,
        out_shape=jax.ShapeDtypeStruct((m, D_MODEL), F32),
        scratch_shapes=[pltpu.VMEM((tm, D_MODEL), F32), pltpu.VMEM((tm, D_MODEL), BF16),
                        pltpu.VMEM((tm, D_MODEL), F32)],
        compiler_params=_params(("parallel", "arbitrary")),
        name="mlp",
    )(*args)


def _norm_matmul_kernel(x_ref, g_ref, w_ref, *out_refs):
    xn = _rms(x_ref[...], g_ref[...]).astype(BF16)
    y = _dot(xn, w_ref[...])
    wd = y.shape[1] // len(out_refs)
    for k, o_ref in enumerate(out_refs):
        o_ref[...] = y[:, k * wd:(k + 1) * wd]


def _norm_matmul(h, g, w, n_out):
    m, n = h.shape[0], w.shape[1]
    tm = min(m, 512)
    wd = n // n_out
    assert m % tm == 0 and n % n_out == 0 and wd % LANES == 0
    outs = pl.pallas_call(
        _norm_matmul_kernel,
        grid=(m // tm,),
        in_specs=[pl.BlockSpec((tm, D_MODEL), lambda i: (i, 0)), pl.BlockSpec((1, D_MODEL), lambda i: (0, 0)),
                  pl.BlockSpec((D_MODEL, n), lambda i: (0, 0))],
        out_specs=tuple(pl.BlockSpec((tm, wd), lambda i: (i, 0)) for _ in range(n_out)),
        out_shape=tuple(jax.ShapeDtypeStruct((m, wd), F32) for _ in range(n_out)),
        compiler_params=_params(("parallel",)),
        name="norm_matmul",
    )(h, g[None, :], w.astype(BF16))
    return outs


def _compress_kernel(pt_ref, *refs, npg, channel_major):
    page_refs = refs[:npg]
    w1_ref, pe_ref, w2_ref, out_ref, z_ref, aprev_ref = refs[npg:npg + 6]
    j = pl.program_id(1)
    grp = PAGE_SIZE // CMP_STRIDE
    sec = grp * npg
    rows = N_KV * sec

    @pl.when(j == 0)
    def _():
        aprev_ref[...] = jnp.zeros_like(aprev_ref)

    ncol = 2 * KV_COLS // LANES
    if channel_major:
        tr_ref = refs[npg + 6]
        for t in range(npg):
            for c in range(ncol):
                tr_ref[t * ncol + c] = page_refs[t][0, c * LANES:(c + 1) * LANES, :].T

        def rows_of(t, c, p):
            return tr_ref[t * ncol + c, pl.ds(p, grp, stride=CMP_STRIDE), :]
    else:
        def rows_of(t, c, p):
            return page_refs[t][0, pl.ds(ncol * p + c, grp, stride=ncol * CMP_STRIDE), :]

    for t in range(npg):
        for q in range(CMP_STRIDE // 2):
            for c in range(ncol):
                xa = rows_of(t, c, 2 * q)
                xb = rows_of(t, c, 2 * q + 1)
                kv = c // (ncol // 2)
                for half in range(LANES // HEAD_DIM):
                    g = (c % (ncol // 2)) * (LANES // HEAD_DIM) + half
                    hs = slice(half * HEAD_DIM, (half + 1) * HEAD_DIM)
                    piece = jnp.concatenate([xa[:, hs], xb[:, hs]], axis=1)
                    z_ref[kv, g * sec + t * grp:g * sec + (t + 1) * grp, q * LANES:(q + 1) * LANES] = piece

    rid = lax.broadcasted_iota(jnp.int32, (rows, CMP_HIDDEN), 0) % sec
    for kv in range(2):
        z = z_ref[kv]
        za = (z + pe_ref[kv, 0:1, :]).astype(BF16)
        zb = (z + pe_ref[kv, 1:2, :]).astype(BF16)
        a = _dot(za, w1_ref[kv, :, :CMP_HIDDEN])
        b = _dot(zb, w1_ref[kv, :, CMP_HIDDEN:])
        a_prev = jnp.where(rid == 0, pltpu.roll(aprev_ref[kv], rows - (sec - 1), 0), pltpu.roll(a, 1, 0))
        aprev_ref[kv] = a
        hid = jax.nn.gelu(a_prev + b).astype(BF16)
        o = _dot(hid, w2_ref[kv])
        out_ref[0, kv] = jnp.concatenate([o[g * sec:(g + 1) * sec] for g in range(N_KV)], axis=1)


def _compress(pages, page_table, pe_k, w1_k, w2_k, pe_v, w1_v, w2_v, *, channel_major):
    n_seq, n_pages = page_table.shape
    npg = PAGES_PER_STEP
    assert n_pages % npg == 0
    grp = PAGE_SIZE // CMP_STRIDE
    half = CMP_STRIDE * HEAD_DIM

    def split(w1):
        return jnp.concatenate([w1[:half], w1[half:]], axis=1)

    w1 = jnp.stack([split(w1_k), split(w1_v)]).astype(BF16)
    pe = jnp.stack([pe_k.reshape(2, half), pe_v.reshape(2, half)])
    w2 = jnp.stack([w2_k, w2_v]).astype(BF16)

    ncol = 2 * KV_COLS // LANES
    scratch = [pltpu.VMEM((2, N_KV * grp * npg, half), F32), pltpu.VMEM((2, N_KV * grp * npg, CMP_HIDDEN), F32)]
    if channel_major:
        scratch.append(pltpu.VMEM((npg * ncol, PAGE_SIZE, LANES), F32))

    def page_spec(t):
        return pl.BlockSpec((1, ncol * PAGE_SIZE, LANES), lambda s, j, pt: (pt[s * n_pages + j * npg + t], 0, 0))

    const = lambda shape: pl.BlockSpec(shape, lambda s, j, pt: (0,) * len(shape))
    return pl.pallas_call(
        functools.partial(_compress_kernel, npg=npg, channel_major=channel_major),
        grid_spec=pltpu.PrefetchScalarGridSpec(
            num_scalar_prefetch=1,
            grid=(n_seq, n_pages // npg),
            in_specs=[page_spec(t) for t in range(npg)] + [const((2, half, 2 * CMP_HIDDEN)), const((2, 2, half)),
                                                           const((2, CMP_HIDDEN, HEAD_DIM))],
            out_specs=pl.BlockSpec((1, 2, grp * npg, KV_COLS), lambda s, j, pt: (s, 0, j, 0)),
            scratch_shapes=scratch),
        out_shape=jax.ShapeDtypeStruct((n_seq, 2, n_pages * grp, KV_COLS), F32),
        compiler_params=_params(("parallel", "arbitrary")),
        name="compress",
    )(page_table.reshape(-1), *([pages] * npg), w1, pe, w2)


def _selection_map(n_rows, n_blk, n_cols):
    i = np.arange(n_rows)[:, None] - 1
    j = np.arange(n_cols)[None, :]
    lo = np.maximum(i * CMP_STRIDE, j * SEL_LEN)
    hi = np.minimum(i * CMP_STRIDE + CMP_LEN, (j + 1) * SEL_LEN)
    m = np.maximum(hi - lo, 0) // CMP_STRIDE
    m = np.where((i >= 0) & (j < n_blk), m, 0)
    return jnp.asarray(m, dtype=BF16)


def _top_blocks_along_rows(score, n_rows):
    ridx = lax.broadcasted_iota(jnp.int32, score.shape, 0)
    work = score
    sel = jnp.zeros(score.shape, F32)
    for _ in range(N_SEL):
        mx = jnp.max(work, axis=0, keepdims=True)
        first = jnp.min(jnp.where(work == mx, ridx, n_rows), axis=0, keepdims=True)
        hit = ridx == first
        sel = jnp.where(hit, 1.0, sel)
        work = jnp.where(hit, -3e38, work)
    return sel


def _nsa_prompt_kernel(zq_ref, zg_ref, bg_ref, kct_ref, vca_ref, smap_ref, ksa_ref, vsa_ref, kwt_ref, vwa_ref,
                       out_ref, s_scr, p_scr, m_scr, acc_scr):
    n = pl.program_id(2)
    s0 = n * Q_BLOCK
    rows = Q_PER_KV * Q_BLOCK
    q = zq_ref[...] * SCALE
    qs = jnp.concatenate([q[:, r * HEAD_DIM:(r + 1) * HEAD_DIM] for r in range(Q_PER_KV)], axis=0).astype(BF16)
    i_row = lax.broadcasted_iota(jnp.int32, (rows, 1), 0) & (Q_BLOCK - 1)
    qpos = s0 + i_row

    n_c = kct_ref.shape[3]
    sc = _dot(qs, kct_ref[0, 0])
    cidx = lax.broadcasted_iota(jnp.int32, (rows, n_c), 1)
    mc = (cidx >= 1) & (cidx * CMP_STRIDE + (CMP_STRIDE - 1) <= qpos)
    scm = jnp.where(mc, sc, NEG_INF)
    p = jnp.where(mc, jnp.exp(scm - jnp.max(scm, axis=1, keepdims=True)), 0.0)
    acc_c = _dot(p.astype(BF16), vca_ref[0, 0])
    l_c = acc_c[:, HEAD_DIM:HEAD_DIM + 1]
    o_c = acc_c[:, :HEAD_DIM] * jnp.where(l_c > 0.0, 1.0 / l_c, 0.0)
    l32 = jnp.sum(p, axis=1, keepdims=True)
    pn = p * jnp.where(l32 > 0.0, 1.0 / l32, 0.0)
    psum = pn[0:Q_BLOCK]
    for r in range(1, Q_PER_KV):
        psum = psum + pn[r * Q_BLOCK:(r + 1) * Q_BLOCK]
    p_hi = psum.astype(BF16)
    p_lo = (psum - p_hi.astype(F32)).astype(BF16)
    imp = _dot(p_hi, smap_ref[...]) + _dot(p_lo, smap_ref[...])

    wk = WINDOW + Q_BLOCK
    w0 = pl.multiple_of(s0, Q_BLOCK)
    sw = _dot(qs, kwt_ref[0, 0, :, pl.ds(w0, wk)])
    rel = lax.broadcasted_iota(jnp.int32, (rows, wk), 1) - WINDOW
    mw = (rel <= i_row) & (rel > i_row - WINDOW) & (rel + s0 >= 0)
    swm = jnp.where(mw, sw, NEG_INF)
    pw = jnp.exp(swm - jnp.max(swm, axis=1, keepdims=True)).astype(BF16)
    acc_w = _dot(pw, vwa_ref[0, 0, pl.ds(w0, wk), :])
    o_w = acc_w[:, :HEAD_DIM] / acc_w[:, HEAD_DIM:HEAD_DIM + 1]

    imp_t = imp.T
    nb = imp_t.shape[0]
    jt = lax.broadcasted_iota(jnp.int32, (nb, Q_BLOCK), 0)
    cur = (s0 + lax.broadcasted_iota(jnp.int32, (nb, Q_BLOCK), 1)) // SEL_LEN
    forced = (jt == 0) | (jt == cur) | (jt == cur - 1)
    score = jnp.where(forced, FORCE_SCORE, jnp.where(jt <= cur, imp_t, -1.0))
    sel = _top_blocks_along_rows(score, nb)
    bias_t = jnp.where((sel > 0.0) & (score >= 0.0), 0.0, NEG_INF)
    bias = bias_t.T.astype(BF16)
    qaug = jnp.concatenate([jnp.concatenate([bias] * Q_PER_KV, axis=0), qs], axis=1)

    def scores(t):
        k0 = pl.multiple_of(t * SEL_TILE, SEL_TILE)
        return _dot(qaug, ksa_ref[0, 0, :, pl.ds(k0, SEL_TILE)])

    def weighted(pt, t):
        k0 = pl.multiple_of(t * SEL_TILE, SEL_TILE)
        return _dot(pt, vsa_ref[0, 0, pl.ds(k0, SEL_TILE), :])

    def stage(t, cur, causal, prefetch):
        prv = 1 - cur
        pv = weighted(p_scr[prv], jnp.maximum(t - 1, 0))
        s = s_scr[cur]
        if causal:
            kpos = t * SEL_TILE + lax.broadcasted_iota(jnp.int32, (rows, SEL_TILE), 1)
            s = jnp.where(kpos <= qpos, s, NEG_INF)
        m_prev = m_scr[...]
        m_new = jnp.maximum(m_prev, jnp.max(s, axis=1, keepdims=True))
        p_scr[cur] = jnp.exp(s - m_new).astype(BF16)
        m_scr[...] = m_new
        acc_scr[...] = (acc_scr[...] + pv) * jnp.exp(m_prev - m_new)
        if prefetch:
            s_scr[prv] = scores(t + 1)

    def trip(i, carry):
        stage(2 * i, 0, False, True)
        stage(2 * i + 1, 1, False, True)
        return carry

    n_pairs = s0 // (2 * SEL_TILE)
    s_scr[0] = scores(0)
    p_scr[1] = jnp.zeros((rows, SEL_TILE), BF16)
    m_scr[...] = jnp.full((rows, 1), NEG_INF, F32)
    acc_scr[...] = jnp.zeros((rows, LANES), F32)
    lax.fori_loop(0, n_pairs, trip, 0)
    stage(2 * n_pairs, 0, True, True)
    stage(2 * n_pairs + 1, 1, True, False)
    acc_s = acc_scr[...] + weighted(p_scr[1], 2 * n_pairs + 1)
    o_s = acc_s[:, :HEAD_DIM] / acc_s[:, HEAD_DIM:HEAD_DIM + 1]

    gate =1.0 / (1.0 + jnp.exp(-(zg_ref[...] + bg_ref[...])))
    outs = []
    for r in range(Q_PER_KV):
        rs = slice(r * Q_BLOCK, (r + 1) * Q_BLOCK)
        outs.append(gate[:, r:r + 1] * o_c[rs] + gate[:, Q_PER_KV + r:Q_PER_KV + r + 1] * o_s[rs]
                    + gate[:, 2 * Q_PER_KV + r:2 * Q_PER_KV + r + 1] * o_w[rs])
    out_ref[...] = jnp.concatenate(outs, axis=1).astype(BF16)


def _kv_layouts(rows, b, t, left_pad):
    k = rows[:, :KV_COLS].reshape(b, t, N_KV, HEAD_DIM).astype(BF16)
    v = rows[:, KV_COLS:].reshape(b, t, N_KV, HEAD_DIM).astype(BF16)
    kt = jnp.transpose(k, (0, 2, 3, 1))
    va = jnp.concatenate([jnp.transpose(v, (0, 2, 1, 3)), jnp.ones((b, N_KV, t, 1), BF16),
                          jnp.zeros((b, N_KV, t, LANES - HEAD_DIM - 1), BF16)], axis=3)
    if left_pad:
        kt = jnp.pad(kt, ((0, 0), (0, 0), (0, 0), (left_pad, 0)))
        va = jnp.pad(va, ((0, 0), (0, 0), (left_pad, 0), (0, 0)))
    return kt, va


def _nsa_prompt(z, bg_pad, kvc, sel_rows, win_rows, b, t):
    nq = t // Q_BLOCK
    n_c = kvc.shape[2]
    n_blk = t // SEL_LEN
    assert t % (2 * SEL_TILE) == 0 and N_SEL <= n_blk <= LANES
    rows = Q_PER_KV * Q_BLOCK
    kc = kvc[:, 0].reshape(b, n_c, N_KV, HEAD_DIM).astype(BF16)
    vc = kvc[:, 1].reshape(b, n_c, N_KV, HEAD_DIM).astype(BF16)
    kct = jnp.transpose(kc, (0, 2, 3, 1))
    vca = jnp.concatenate([jnp.transpose(vc, (0, 2, 1, 3)), jnp.ones((b, N_KV, n_c, 1), BF16),
                           jnp.zeros((b, N_KV, n_c, LANES - HEAD_DIM - 1), BF16)], axis=3)
    smap = _selection_map(n_c, n_blk, LANES)
    kst, vsa = _kv_layouts(sel_rows, b, t, 0)
    blk_of_key = np.arange(t)[None, :] // SEL_LEN == np.arange(LANES)[:, None]
    ksa = jnp.concatenate([jnp.broadcast_to(jnp.asarray(blk_of_key, BF16), (b, N_KV, LANES, t)), kst], axis=2)
    kwt, vwa = _kv_layouts(win_rows, b, t, WINDOW)
    per_bg = lambda shape: pl.BlockSpec((1, 1) + shape, lambda bb, g, n: (bb, g, 0, 0))
    return pl.pallas_call(
        _nsa_prompt_kernel,
        grid=(b, N_KV, nq),
        in_specs=[pl.BlockSpec((Q_BLOCK, Q_PER_KV * HEAD_DIM), lambda bb, g, n: (bb * nq + n, g)),
                  pl.BlockSpec((Q_BLOCK, LANES), lambda bb, g, n: (bb * nq + n, D_MODEL // LANES + g)),
                  pl.BlockSpec((1, LANES), lambda bb, g, n: (0, g)),
                  per_bg((HEAD_DIM, n_c)), per_bg((n_c, LANES)),
                  pl.BlockSpec((n_c, LANES), lambda bb, g, n: (0, 0)),
                  per_bg((HEAD_DIM + LANES, t)), per_bg((t, LANES)),
                  per_bg((HEAD_DIM, WINDOW + t)), per_bg((WINDOW + t, LANES))],
        out_specs=pl.BlockSpec((Q_BLOCK, Q_PER_KV * HEAD_DIM), lambda bb, g, n: (bb * nq + n, g)),
        out_shape=jax.ShapeDtypeStruct((b * t, D_MODEL), BF16),
        scratch_shapes=[pltpu.VMEM((2, rows, SEL_TILE), F32), pltpu.VMEM((2, rows, SEL_TILE), BF16),
                        pltpu.VMEM((rows, 1), F32), pltpu.VMEM((rows, LANES), F32)],
        compiler_params=_params(("parallel", "parallel", "arbitrary")),
        name="nsa_prompt",
    )(z, z, bg_pad, kct, vca, smap, ksa, vsa, kwt, vwa)


SROWS = 32
SBLK = 256


def _samp_cmpwin_kernel(qbd_ref, kvc_ref, smap_ref, win_ref, wnew_ref, oc_ref, ow_ref, imp_ref, *, qpos):
    qf = qbd_ref[0] * SCALE
    qb = qf.astype(BF16)
    kc = kvc_ref[0, 0].astype(BF16)
    vc = kvc_ref[0, 1].astype(BF16)
    n_c = kc.shape[0]
    sc = _dot_nt(qb, kc)
    cidx = lax.broadcasted_iota(jnp.int32, (SROWS, n_c), 1)
    mc = (cidx >= 1) & (cidx * CMP_STRIDE + (CMP_STRIDE - 1) <= qpos)
    scm = jnp.where(mc, sc, NEG_INF)
    p = jnp.where(mc, jnp.exp(scm - jnp.max(scm, axis=1, keepdims=True)), 0.0)
    l = jnp.sum(p, axis=1, keepdims=True)
    inv = jnp.where(l > 0.0, 1.0 / l, 0.0)
    oc_ref[0] = _dot(p.astype(BF16), vc) * inv
    pn = p * inv
    psum = pn[0:8] + pn[8:16] + pn[16:24] + pn[24:32]
    p_hi = psum.astype(BF16)
    p_lo = (psum - p_hi.astype(F32)).astype(BF16)
    imp_ref[0] = _dot(p_hi, smap_ref[...]) + _dot(p_lo, smap_ref[...])
    win = win_ref[0]
    n_w = win.shape[1]
    sw = _dot(qb, win[:KV_COLS].astype(BF16))
    widx = lax.broadcasted_iota(jnp.int32, (SROWS, n_w), 1)
    dist = n_w - widx
    mw = (dist < WINDOW) & (qpos - dist >= 0)
    swm = jnp.where(mw, sw, NEG_INF)
    wnew = wnew_ref[0]
    s_new = jnp.sum(qb.astype(F32) * wnew[:, :KV_COLS].astype(BF16).astype(F32), axis=1, keepdims=True)
    m = jnp.maximum(jnp.max(swm, axis=1, keepdims=True), s_new)
    pw = jnp.exp(swm - m)
    p_new = jnp.exp(s_new - m)
    lw = jnp.sum(pw, axis=1, keepdims=True) + p_new
    ow = _dot_nt(pw.astype(BF16), win[KV_COLS:].astype(BF16)) + p_new * wnew[:, KV_COLS:]
    ow_ref[0] = ow / lw


def _samp_topk_kernel(imp_ref, bias_ref, *, cur):
    imp = imp_ref[...]
    jidx = lax.broadcasted_iota(jnp.int32, imp.shape, 1)
    forced = (jidx == 0) | (jidx == cur) | (jidx == cur - 1)
    score = jnp.where(forced, FORCE_SCORE, jnp.where(jidx <= cur, imp, -1.0))
    work = score
    sel = jnp.zeros(imp.shape, F32)
    for _ in range(N_SEL):
        mx = jnp.max(work, axis=1, keepdims=True)
        first = jnp.min(jnp.where(work == mx, jidx, SBLK), axis=1, keepdims=True)
        hit = jidx == first
        sel = jnp.where(hit, 1.0, sel)
        work = jnp.where(hit, -3e38, work)
    bias_ref[...] = jnp.where((sel > 0.0) & (score >= 0.0), 0.0, NEG_INF)


def _samp_sel_kernel(pt_ref, *refs, npg, qpos):
    page_refs = refs[:npg]
    (qbd_ref, bias_ref, knew_ref, oc_ref, ow_ref, zg_ref, bg_ref, out_ref, m_ref, acc_ref, l_ref) = refs[npg:]
    j = pl.program_id(1)
    keys = npg * PAGE_SIZE
    qb = (qbd_ref[0] * SCALE).astype(BF16)

    @pl.when(j == 0)
    def _():
        m_ref[...] = jnp.full_like(m_ref, NEG_INF)
        acc_ref[...] = jnp.zeros_like(acc_ref)
        l_ref[...] = jnp.zeros_like(l_ref)

    bias8 = bias_ref[0].astype(BF16)
    bias32 = jnp.concatenate([bias8] * Q_PER_KV, axis=0)
    blk = lax.broadcasted_iota(jnp.int32, (SBLK, keys), 0)
    key = lax.broadcasted_iota(jnp.int32, (SBLK, keys), 1)
    onehot = jnp.where(blk == j * (keys // SEL_LEN) + key // SEL_LEN, 1.0, 0.0).astype(BF16)
    s = jnp.concatenate([_dot(qb, page_refs[t][0, :KV_COLS, :].astype(BF16)) for t in range(npg)], axis=1)
    s = s + _dot(bias32, onehot)
    m_old = m_ref[...]
    m_new = jnp.maximum(m_old, jnp.max(s, axis=1, keepdims=True))
    p = jnp.exp(s - m_new)
    alpha = jnp.exp(m_old - m_new)
    pv = _dot_nt(p[:, :PAGE_SIZE].astype(BF16), page_refs[0][0, KV_COLS:, :].astype(BF16))
    for t in range(1, npg):
        pv = pv + _dot_nt(p[:, t * PAGE_SIZE:(t + 1) * PAGE_SIZE].astype(BF16),
                          page_refs[t][0, KV_COLS:, :].astype(BF16))
    m_ref[...] = m_new
    l_ref[...] = alpha * l_ref[...] + jnp.sum(p, axis=1, keepdims=True)
    acc_ref[...] = alpha * acc_ref[...] + pv

    @pl.when(j == pl.num_programs(1) - 1)
    def _():
        knew = knew_ref[0]
        s_new = jnp.sum(qb.astype(F32) * knew[:, :KV_COLS].astype(BF16).astype(F32), axis=1, keepdims=True)
        m_fin = jnp.maximum(m_ref[...], s_new)
        a = jnp.exp(m_ref[...] - m_fin)
        p_new = jnp.exp(s_new - m_fin)
        os_ = (a * acc_ref[...] + p_new * knew[:, KV_COLS:]) / (a * l_ref[...] + p_new)
        gate = 1.0 / (1.0 + jnp.exp(-(zg_ref[0] + bg_ref[0])))
        o = gate[:, 0:1] * oc_ref[0] + gate[:, 1:2] * os_ + gate[:, 2:3] * ow_ref[0]
        gsel = lax.broadcasted_iota(jnp.int32, (SROWS, HEAD_DIM), 0) & 7
        res = jnp.zeros((SROWS, HEAD_DIM), F32)
        for g in range(N_KV):
            res = res + jnp.where(gsel == g, o[:, g * HEAD_DIM:(g + 1) * HEAD_DIM], 0.0)
        out_ref[0] = res


def _nsa_sample(z, b_g, kvc, cache_sel, page_table, sel_new, win_state, win_new, past_len):
    n_seq, n_pages = page_table.shape
    npg = PAGES_PER_STEP
    qpos = past_len
    n_blk = -(-(past_len + 1) // SEL_LEN)
    assert n_blk <= SBLK and n_pages % npg == 0
    n_c = kvc.shape[2]
    n_w = win_state.shape[2]
    q4 = z[:, :D_MODEL].reshape(n_seq, N_KV, Q_PER_KV, HEAD_DIM)
    eye = jnp.eye(N_KV, dtype=F32)
    qbd = jnp.transpose(q4[:, :, :, None, :] * eye[None, :, None, :, None], (0, 2, 1, 3, 4))
    qbd = jnp.pad(qbd, ((0, 0), (0, 0), (0, 8 - N_KV), (0, 0), (0, 0))).reshape(n_seq, SROWS, KV_COLS)
    def gate_rows(a):
        n = a.shape[0]
        a = jnp.transpose(a.reshape(n, N_KV, Q_PER_KV, 3), (0, 2, 1, 3))
        return jnp.pad(a, ((0, 0), (0, 0), (0, 8 - N_KV), (0, LANES - 3))).reshape(n, SROWS, LANES)

    zg = gate_rows(z[:, D_MODEL:D_MODEL + 3 * N_HEADS])
    bg = gate_rows(b_g[None, :])
    smap = _selection_map(n_c, n_blk, SBLK)
    per_seq = lambda shape: pl.BlockSpec((1,) + shape, lambda s: (s,) + (0,) * len(shape))
    o_c, o_w, imp = pl.pallas_call(
        functools.partial(_samp_cmpwin_kernel, qpos=qpos),
        grid=(n_seq,),
        in_specs=[per_seq((SROWS, KV_COLS)), per_seq((2, n_c, KV_COLS)),
                  pl.BlockSpec((n_c, SBLK), lambda s: (0, 0)), per_seq((2 * KV_COLS, n_w)),
                  per_seq((1, 2 * KV_COLS))],
        out_specs=(per_seq((SROWS, KV_COLS)), per_seq((SROWS, KV_COLS)), per_seq((8, SBLK))),
        out_shape=(jax.ShapeDtypeStruct((n_seq, SROWS, KV_COLS), F32),
                   jax.ShapeDtypeStruct((n_seq, SROWS, KV_COLS), F32),
                   jax.ShapeDtypeStruct((n_seq, 8, SBLK), F32)),
        compiler_params=_params(("parallel",)),
        name="nsa_sample_cmpwin",
    )(qbd, kvc, smap, win_state, win_new[:, None, :])
    bias = pl.pallas_call(
        functools.partial(_samp_topk_kernel, cur=qpos // SEL_LEN),
        out_shape=jax.ShapeDtypeStruct((n_seq * 8, SBLK), F32),
        name="nsa_sample_topk",
    )(imp.reshape(n_seq * 8, SBLK)).reshape(n_seq, 8, SBLK)

    def page_spec(t):
        return pl.BlockSpec((1, 2 * KV_COLS, PAGE_SIZE), lambda s, j, pt: (pt[s * n_pages + j * npg + t], 0, 0))

    seq_blk = lambda shape: pl.BlockSpec((1,) + shape, lambda s, j, pt: (s,) + (0,) * len(shape))
    out = pl.pallas_call(
        functools.partial(_samp_sel_kernel, npg=npg, qpos=qpos),
        grid_spec=pltpu.PrefetchScalarGridSpec(
            num_scalar_prefetch=1,
            grid=(n_seq, n_pages // npg),
            in_specs=[page_spec(t) for t in range(npg)] + [
                seq_blk((SROWS, KV_COLS)), seq_blk((8, SBLK)), seq_blk((1, 2 * KV_COLS)),
                seq_blk((SROWS, KV_COLS)), seq_blk((SROWS, KV_COLS)), seq_blk((SROWS, LANES)),
                pl.BlockSpec((1, SROWS, LANES), lambda s, j, pt: (0, 0, 0))],
            out_specs=seq_blk((SROWS, HEAD_DIM)),
            scratch_shapes=[pltpu.VMEM((SROWS, 1), F32), pltpu.VMEM((SROWS, KV_COLS), F32),
                            pltpu.VMEM((SROWS, 1), F32)]),
        out_shape=jax.ShapeDtypeStruct((n_seq, SROWS, HEAD_DIM), F32),
        compiler_params=_params(("parallel", "arbitrary")),
        name="nsa_sample_sel",
    )(page_table.reshape(-1), *([cache_sel] * npg), qbd, bias, sel_new[:, None, :], o_c, o_w, zg, bg)
    out = out.reshape(n_seq, Q_PER_KV, 8, HEAD_DIM)[:, :, :N_KV]
    return jnp.transpose(out, (0, 2, 1, 3)).reshape(n_seq, D_MODEL).astype(BF16)


def kernel(x_prompt, x_sample, cache_cmp_kv, cache_sel_kv, state_win_kv, page_table, a_norm_g, a_w_uv, a_v_norm_g, a_w_s, a_b_s, a_w_o, kv_norm_g, w_kv, cmp_pe_k, cmp_w1_k, cmp_w2_k, cmp_pe_v, cmp_w1_v, cmp_w2_v, b_norm_g, b_w_qg, b_b_g, b_w_o, mlp_norm_g, mlp_w_up, mlp_w_down, final_norm_g):
    b, t, _ = x_prompt.shape
    n_seq, dec_seq, _ = x_sample.shape
    n_pool, page, _, _, _ = cache_cmp_kv.shape
    n_pages = page_table.shape[1]
    past_len = n_pages * page
    win_buf = state_win_kv.shape[1]
    depth = mlp_norm_g.shape[0]
    n_a = a_norm_g.shape[0]
    assert dec_seq == 1 and page == PAGE_SIZE and t % PAGE_SIZE == 0
    cmp_w = (cmp_pe_k, cmp_w1_k, cmp_w2_k, cmp_pe_v, cmp_w1_v, cmp_w2_v)
    kv_shape = (2, N_KV, HEAD_DIM)
    gate_w = 3 * N_HEADS

    hp = x_prompt.reshape(b * t, D_MODEL)
    hs = x_sample.reshape(n_seq, D_MODEL)
    a_v_rows = []
    attn_p = attn_s = None
    for layer in range(depth):
        last = layer == depth - 1
        fin = final_norm_g if last else None
        if layer < n_a:
            i = layer
            hp = _gmlp_layer(hp, a_norm_g[i], a_w_uv[i], a_v_norm_g[i], a_w_s[i], a_b_s[i], a_w_o[i], chunked=True)
            hs, v_s = _gmlp_layer(hs, a_norm_g[i], a_w_uv[i], a_v_norm_g[i], a_w_s[i], a_b_s[i], a_w_o[i],
                                  chunked=False)
            a_v_rows.append(v_s.reshape(n_seq, 1, D_MODEL))
            hp = _mlp_layer(hp, mlp_norm_g[layer], mlp_w_up[layer], mlp_w_down[layer], final_g=fin)
            hs = _mlp_layer(hs, mlp_norm_g[layer], mlp_w_up[layer], mlp_w_down[layer], final_g=fin)
            continue
        if layer == n_a:
            cmp_p, sel_p, win_p = _norm_matmul(hp, kv_norm_g, w_kv, 3)
            cmp_s, sel_s, win_s = _norm_matmul(hs, kv_norm_g, w_kv, 3)
            ncol = 2 * KV_COLS // LANES
            kvc_p = _compress(cmp_p.reshape(b * t // PAGE_SIZE, ncol * PAGE_SIZE, LANES),
                              jnp.arange(b * t // PAGE_SIZE, dtype=jnp.int32).reshape(b, t // PAGE_SIZE), *cmp_w,
                              channel_major=False)

            def channel_major(a):
                return jnp.transpose(a, (0, 2, 3, 4, 1)).reshape(a.shape[0], 2 * KV_COLS, a.shape[1])

            kvc_s = _compress(channel_major(cache_cmp_kv), page_table, *cmp_w, channel_major=True)
            cache_sel = channel_major(cache_sel_kv)
            win_state = channel_major(state_win_kv)
        j = layer - n_a
        wg = b_w_qg[j][:, D_MODEL:].reshape(D_MODEL, N_KV, Q_PER_KV, 3)
        wg = jnp.transpose(wg, (0, 1, 3, 2)).reshape(D_MODEL, N_KV, 3 * Q_PER_KV)
        wg = jnp.pad(wg, ((0, 0), (0, 0), (0, LANES - 3 * Q_PER_KV))).reshape(D_MODEL, N_KV * LANES)
        bg = jnp.transpose(b_b_g[j].reshape(N_KV, Q_PER_KV, 3), (0, 2, 1)).reshape(N_KV, 3 * Q_PER_KV)
        bg = jnp.pad(bg, ((0, 0), (0, LANES - 3 * Q_PER_KV))).reshape(1, N_KV * LANES)
        (z_p,) = _norm_matmul(hp, b_norm_g[j], jnp.concatenate([b_w_qg[j][:, :D_MODEL], wg], axis=1), 1)
        attn_p = _nsa_prompt(z_p, bg, kvc_p, sel_p, win_p, b, t)
        w_s_pad = jnp.pad(b_w_qg[j], ((0, 0), (0, LANES - gate_w % LANES)))
        (z_s,) = _norm_matmul(hs, b_norm_g[j], w_s_pad, 1)
        attn_s = _nsa_sample(z_s, b_b_g[j], kvc_s, cache_sel, page_table, sel_s, win_state, win_s, past_len)
        hp = _mlp_layer(hp, mlp_norm_g[layer], mlp_w_up[layer], mlp_w_down[layer], attn=attn_p, w_o=b_w_o[j],
                        final_g=fin)
        hs = _mlp_layer(hs, mlp_norm_g[layer], mlp_w_up[layer], mlp_w_down[layer], attn=attn_s, w_o=b_w_o[j],
                        final_g=fin)

    y_prompt = hp.reshape(b, t, D_MODEL)
    y_sample = hs.reshape(n_seq, 1, D_MODEL)
    nw = min(WINDOW, t)
    win_kv_prompt = win_p.reshape(b, t, *kv_shape)[:, t - nw:]
    win_kv_sample = jnp.concatenate([state_win_kv, win_s.reshape(n_seq, 1, *kv_shape)], axis=1)[:, dec_seq:]
    return (y_prompt, y_sample, cmp_p.reshape(b, t, *kv_shape), sel_p.reshape(b, t, *kv_shape), win_kv_prompt,
            cmp_s.reshape(n_seq, 1, *kv_shape), sel_s.reshape(n_seq, 1, *kv_shape), win_kv_sample,
            jnp.stack(a_v_rows, axis=0))
```

```python
import functools

import numpy as np
import jax
import jax.numpy as jnp
from jax import lax
from jax.experimental import pallas as pl
from jax.experimental.pallas import tpu as pltpu

F32 = jnp.float32
BF16 = jnp.bfloat16

D_MODEL = 1024
D_FF = 4 * D_MODEL
CHUNK = 128
A_GROUP_DIM = 128
A_GROUPS = D_MODEL // A_GROUP_DIM
HEAD_DIM = 64
N_HEADS = D_MODEL // HEAD_DIM
N_KV = 4
Q_PER_KV = N_HEADS // N_KV
KV_COLS = N_KV * HEAD_DIM
CMP_STRIDE = 16
CMP_LEN = 2 * CMP_STRIDE
CMP_HIDDEN = 4 * HEAD_DIM
SEL_LEN = 64
N_SEL = 16
WINDOW = 512
Q_BLOCK = 128
PAGE_SIZE = 128
EPS = 1e-6
NEG_INF = -1e30
FORCE_SCORE = 1e6
SCALE = HEAD_DIM ** -0.5
LOG2E = 1.4426950408889634
LANES = 128
V_ROWS = 80
KA_COLS = 256
SEL_TILE = 512
PAGES_PER_STEP = 8
VMEM_LIMIT = 56 * 1024 * 1024


def _params(sem):
    return pltpu.CompilerParams(dimension_semantics=sem, vmem_limit_bytes=VMEM_LIMIT)


def _rms(x, g):
    return x * lax.rsqrt(jnp.mean(x * x, axis=-1, keepdims=True) + EPS) * g


def _dot(a, b):
    return jnp.dot(a, b, preferred_element_type=F32)


def _dot_nt(a, b):
    return lax.dot_general(a, b, (((1,), (1,)), ((), ())), preferred_element_type=F32)


def _gmlp_kernel(x_ref, g_ref, wuv_ref, gv_ref, ws_ref, bias_ref, wo_ref, out_ref, *rest, chunked, tm):
    x = x_ref[...]
    xn = _rms(x, g_ref[...]).astype(BF16)
    uv = _dot(xn, wuv_ref[...])
    u = uv[:, :D_MODEL]
    v = _rms(uv[:, D_MODEL:], gv_ref[...])
    if chunked:
        gated_ref = rest[0]
        row = lax.broadcasted_iota(jnp.int32, (CHUNK, CHUNK), 0)
        col = lax.broadcasted_iota(jnp.int32, (CHUNK, CHUNK), 1)
        for g in range(A_GROUPS):
            cs = slice(g * A_GROUP_DIM, (g + 1) * A_GROUP_DIM)
            w = jnp.where(row >= col, ws_ref[g], 0.0).astype(BF16)
            for c in range(tm // CHUNK):
                rs = slice(c * CHUNK, (c + 1) * CHUNK)
                s = _dot(w, v[rs, cs].astype(BF16)) + bias_ref[:, cs]
                gated_ref[rs, cs] = (u[rs, cs] * s).astype(BF16)
        gated = gated_ref[...]
    else:
        v_ref = rest[0]
        v_ref[...] = v
        gated = (u * (v * ws_ref[...] + bias_ref[...])).astype(BF16)
    out_ref[...] = x + _dot(gated, wo_ref[...])


def _gmlp_layer(h, g, w_uv, gv, w_s, b_s, w_o, *, chunked):
    m = h.shape[0]
    full = lambda shape: pl.BlockSpec(shape, lambda i: (0,) * len(shape))
    if chunked:
        tm = 256
        ws = w_s
        bias = jnp.repeat(b_s.T, A_GROUP_DIM, axis=1)
        ws_spec = full((A_GROUPS, CHUNK, CHUNK))
        bias_spec = full((CHUNK, D_MODEL))
        out_shape = jax.ShapeDtypeStruct((m, D_MODEL), F32)
        out_specs = pl.BlockSpec((tm, D_MODEL), lambda i: (i, 0))
        scratch = [pltpu.VMEM((tm, D_MODEL), BF16)]
    else:
        tm = m
        ws = jnp.repeat(w_s[:, 0, 0], A_GROUP_DIM)[None, :]
        bias = jnp.repeat(b_s[:, 0], A_GROUP_DIM)[None, :]
        ws_spec = full((1, D_MODEL))
        bias_spec = full((1, D_MODEL))
        out_shape = (jax.ShapeDtypeStruct((m, D_MODEL), F32), jax.ShapeDtypeStruct((m, D_MODEL), F32))
        out_specs = (pl.BlockSpec((tm, D_MODEL), lambda i: (i, 0)), pl.BlockSpec((tm, D_MODEL), lambda i: (i, 0)))
        scratch = []
    assert m % tm == 0
    return pl.pallas_call(
        functools.partial(_gmlp_kernel, chunked=chunked, tm=tm),
        grid=(m // tm,),
        in_specs=[pl.BlockSpec((tm, D_MODEL), lambda i: (i, 0)), full((1, D_MODEL)),
                  full((D_MODEL, 2 * D_MODEL)), full((1, D_MODEL)), ws_spec, bias_spec,
                  full((D_MODEL, D_MODEL))],
        out_specs=out_specs, out_shape=out_shape, scratch_shapes=scratch,
        compiler_params=_params(("parallel",)),
        name="gmlp_prompt" if chunked else "gmlp_sample",
    )(h, g[None, :], w_uv.astype(BF16), gv[None, :], ws, bias, w_o.astype(BF16))


def _mlp_kernel(*refs, has_attn, has_final):
    refs = list(refs)
    x_ref = refs.pop(0)
    attn_ref = refs.pop(0) if has_attn else None
    wo_ref = refs.pop(0) if has_attn else None
    g_ref, wup_ref, wdn_ref = refs.pop(0), refs.pop(0), refs.pop(0)
    fg_ref = refs.pop(0) if has_final else None
    out_ref, h_ref, xn_ref, acc_ref = refs
    j = pl.program_id(1)

    @pl.when(j == 0)
    def _():
        h = x_ref[...]
        if has_attn:
            h = h + _dot(attn_ref[...], wo_ref[...])
        h_ref[...] = h
        xn_ref[...] = _rms(h, g_ref[...]).astype(BF16)
        acc_ref[...] = jnp.zeros_like(acc_ref)

    a = _dot(xn_ref[...], wup_ref[...])
    a = jnp.square(jnp.maximum(a, 0.0)).astype(BF16)
    acc_ref[...] += _dot(a, wdn_ref[...])

    @pl.when(j == pl.num_programs(1) - 1)
    def _():
        h = h_ref[...] + acc_ref[...]
        if has_final:
            h = _rms(h, fg_ref[...])
        out_ref[...] = h


def _mlp_layer(h, g, w_up, w_down, *, attn=None, w_o=None, final_g=None):
    m = h.shape[0]
    tm = min(m, 1024)
    tf = 512
    assert m % tm == 0 and D_FF % tf == 0
    has_attn, has_final = attn is not None, final_g is not None
    row = pl.BlockSpec((tm, D_MODEL), lambda i, j: (i, 0))
    vec = pl.BlockSpec((1, D_MODEL), lambda i, j: (0, 0))
    args, specs = [h], [row]
    if has_attn:
        args += [attn, w_o.astype(BF16)]
        specs += [row, pl.BlockSpec((D_MODEL, D_MODEL), lambda i, j: (0, 0))]
    args += [g[None, :], w_up.astype(BF16), w_down.astype(BF16)]
    specs += [vec, pl.BlockSpec((D_MODEL, tf), lambda i, j: (0, j)), pl.BlockSpec((tf, D_MODEL), lambda i, j: (j, 0))]
    if has_final:
        args.append(final_g[None, :])
        specs.append(vec)
    return pl.pallas_call(
        functools.partial(_mlp_kernel, has_attn=has_attn, has_final=has_final),
        grid=(m // tm, D_FF // tf),
        in_specs=specs, out_specs=row,
        out_shape=jax.ShapeDtypeStruct((m, D_MODEL), F32),
        scratch_shapes=[pltpu.VMEM((tm, D_MODEL), F32), pltpu.VMEM((tm, D_MODEL), BF16),
                        pltpu.VMEM((tm, D_MODEL), F32)],
        compiler_params=_params(("parallel", "arbitrary")),
        name="mlp",
    )(*args)


def _norm_matmul_kernel(x_ref, g_ref, w_ref, *out_refs):
    xn = _rms(x_ref[...], g_ref[...]).astype(BF16)
    y = _dot(xn, w_ref[...])
    wd = y.shape[1] // len(out_refs)
    for k, o_ref in enumerate(out_refs):
        o_ref[...] = y[:, k * wd:(k + 1) * wd]


def _norm_matmul(h, g, w, n_out):
    m, n = h.shape[0], w.shape[1]
    tm = min(m, 512)
    wd = n // n_out
    assert m % tm == 0 and n % n_out == 0 and wd % LANES == 0
    outs = pl.pallas_call(
        _norm_matmul_kernel,
        grid=(m // tm,),
        in_specs=[pl.BlockSpec((tm, D_MODEL), lambda i: (i, 0)), pl.BlockSpec((1, D_MODEL), lambda i: (0, 0)),
                  pl.BlockSpec((D_MODEL, n), lambda i: (0, 0))],
        out_specs=tuple(pl.BlockSpec((tm, wd), lambda i: (i, 0)) for _ in range(n_out)),
        out_shape=tuple(jax.ShapeDtypeStruct((m, wd), F32) for _ in range(n_out)),
        compiler_params=_params(("parallel",)),
        name="norm_matmul",
    )(h, g[None, :], w.astype(BF16))
    return outs


def _compress_kernel(pt_ref, *refs, npg, channel_major):
    page_refs = refs[:npg]
    w1_ref, pe_ref, w2_ref, out_ref, z_ref, aprev_ref = refs[npg:npg + 6]
    j = pl.program_id(1)
    grp = PAGE_SIZE // CMP_STRIDE
    sec = grp * npg
    rows = N_KV * sec

    @pl.when(j == 0)
    def _():
        aprev_ref[...] = jnp.zeros_like(aprev_ref)

    ncol = 2 * KV_COLS // LANES
    if channel_major:
        tr_ref = refs[npg + 6]
        for t in range(npg):
            for c in range(ncol):
                tr_ref[t * ncol + c] = page_refs[t][0, c * LANES:(c + 1) * LANES, :].T

        def rows_of(t, c, p):
            return tr_ref[t * ncol + c, pl.ds(p, grp, stride=CMP_STRIDE), :]
    else:
        def rows_of(t, c, p):
            return page_refs[t][0, pl.ds(ncol * p + c, grp, stride=ncol * CMP_STRIDE), :]

    for t in range(npg):
        for q in range(CMP_STRIDE // 2):
            for c in range(ncol):
                xa = rows_of(t, c, 2 * q)
                xb = rows_of(t, c, 2 * q + 1)
                kv = c // (ncol // 2)
                for half in range(LANES // HEAD_DIM):
                    g = (c % (ncol // 2)) * (LANES // HEAD_DIM) + half
                    hs = slice(half * HEAD_DIM, (half + 1) * HEAD_DIM)
                    piece = jnp.concatenate([xa[:, hs], xb[:, hs]], axis=1)
                    z_ref[kv, g * sec + t * grp:g * sec + (t + 1) * grp, q * LANES:(q + 1) * LANES] = piece

    rid = lax.broadcasted_iota(jnp.int32, (rows, CMP_HIDDEN), 0) % sec
    for kv in range(2):
        z = z_ref[kv]
        za = (z + pe_ref[kv, 0:1, :]).astype(BF16)
        zb = (z + pe_ref[kv, 1:2, :]).astype(BF16)
        a = _dot(za, w1_ref[kv, :, :CMP_HIDDEN])
        b = _dot(zb, w1_ref[kv, :, CMP_HIDDEN:])
        a_prev = jnp.where(rid == 0, pltpu.roll(aprev_ref[kv], rows - (sec - 1), 0), pltpu.roll(a, 1, 0))
        aprev_ref[kv] = a
        hid = jax.nn.gelu(a_prev + b).astype(BF16)
        o = _dot(hid, w2_ref[kv])
        out_ref[0, kv] = jnp.concatenate([o[g * sec:(g + 1) * sec] for g in range(N_KV)], axis=1)


def _compress(pages, page_table, pe_k, w1_k, w2_k, pe_v, w1_v, w2_v, *, channel_major):
    n_seq, n_pages = page_table.shape
    npg = PAGES_PER_STEP
    assert n_pages % npg == 0
    grp = PAGE_SIZE // CMP_STRIDE
    half = CMP_STRIDE * HEAD_DIM

    def split(w1):
        return jnp.concatenate([w1[:half], w1[half:]], axis=1)

    w1 = jnp.stack([split(w1_k), split(w1_v)]).astype(BF16)
    pe = jnp.stack([pe_k.reshape(2, half), pe_v.reshape(2, half)])
    w2 = jnp.stack([w2_k, w2_v]).astype(BF16)

    ncol = 2 * KV_COLS // LANES
    scratch = [pltpu.VMEM((2, N_KV * grp * npg, half), F32), pltpu.VMEM((2, N_KV * grp * npg, CMP_HIDDEN), F32)]
    if channel_major:
        scratch.append(pltpu.VMEM((npg * ncol, PAGE_SIZE, LANES), F32))

    def page_spec(t):
        return pl.BlockSpec((1, ncol * PAGE_SIZE, LANES), lambda s, j, pt: (pt[s * n_pages + j * npg + t], 0, 0))

    const = lambda shape: pl.BlockSpec(shape, lambda s, j, pt: (0,) * len(shape))
    return pl.pallas_call(
        functools.partial(_compress_kernel, npg=npg, channel_major=channel_major),
        grid_spec=pltpu.PrefetchScalarGridSpec(
            num_scalar_prefetch=1,
            grid=(n_seq, n_pages // npg),
            in_specs=[page_spec(t) for t in range(npg)] + [const((2, half, 2 * CMP_HIDDEN)), const((2, 2, half)),
                                                           const((2, CMP_HIDDEN, HEAD_DIM))],
            out_specs=pl.BlockSpec((1, 2, grp * npg, KV_COLS), lambda s, j, pt: (s, 0, j, 0)),
            scratch_shapes=scratch),
        out_shape=jax.ShapeDtypeStruct((n_seq, 2, n_pages * grp, KV_COLS), F32),
        compiler_params=_params(("parallel", "arbitrary")),
        name="compress",
    )(page_table.reshape(-1), *([pages] * npg), w1, pe, w2)


def _selection_map(n_rows, n_blk, n_cols):
    i = np.arange(n_rows)[:, None] - 1
    j = np.arange(n_cols)[None, :]
    lo = np.maximum(i * CMP_STRIDE, j * SEL_LEN)
    hi = np.minimum(i * CMP_STRIDE + CMP_LEN, (j + 1) * SEL_LEN)
    m = np.maximum(hi - lo, 0) // CMP_STRIDE
    m = np.where((i >= 0) & (j < n_blk), m, 0)
    return jnp.asarray(m, dtype=BF16)


def _top_blocks_along_rows(score, n_rows):
    ridx = lax.broadcasted_iota(jnp.int32, score.shape, 0)
    work = score
    for _ in range(N_SEL):
        mx = jnp.max(work, axis=0, keepdims=True)
        first = jnp.min(jnp.where(work == mx, ridx, n_rows), axis=0, keepdims=True)
        work = jnp.where(ridx == first, -3e38, work)
    return work < -1e38


def _nsa_prompt_kernel(zq_ref, zg_ref, bg_ref, kc_ref, vct_ref, smapt_ref, ka_ref, vat_ref, kw_ref, vwt_ref,
                       out_ref, s_scr, p_scr, m_scr, acc_scr):
    n = pl.program_id(2)
    s0 = n * Q_BLOCK
    cols = Q_PER_KV * Q_BLOCK
    qt = (zq_ref[...] * (SCALE * LOG2E)).T
    qst = jnp.concatenate([qt[r * HEAD_DIM:(r + 1) * HEAD_DIM] for r in range(Q_PER_KV)], axis=1).astype(BF16)
    i_col = lax.broadcasted_iota(jnp.int32, (1, cols), 1) & (Q_BLOCK - 1)
    qpos = s0 + i_col

    n_c = kc_ref.shape[2]
    sc = _dot(kc_ref[0, 0], qst)
    cidx = lax.broadcasted_iota(jnp.int32, (n_c, cols), 0)
    lim = jnp.maximum((qpos - (CMP_STRIDE - 1)) // CMP_STRIDE, 0)
    scm = jnp.where((cidx >= 1) & (cidx <= lim), sc, NEG_INF)
    mx_c = jnp.where(lim >= 1, jnp.max(scm, axis=0, keepdims=True), -NEG_INF)
    p = jnp.exp2(scm - mx_c)
    acc_c = _dot(vct_ref[0, 0], p.astype(BF16))
    l_c = acc_c[HEAD_DIM:HEAD_DIM + 1]
    o_c = acc_c[:HEAD_DIM] * jnp.where(l_c > 0.0, 1.0 / l_c, 0.0)
    l32 = jnp.sum(p, axis=0, keepdims=True)
    pn = p * jnp.where(l32 > 0.0, 1.0 / l32, 0.0)
    psum = pn[:, 0:Q_BLOCK]
    for r in range(1, Q_PER_KV):
        psum = psum + pn[:, r * Q_BLOCK:(r + 1) * Q_BLOCK]
    p_hi = psum.astype(BF16)
    p_lo = (psum - p_hi.astype(F32)).astype(BF16)
    imp_t = _dot(smapt_ref[...], p_hi) + _dot(smapt_ref[...], p_lo)

    wk = WINDOW + Q_BLOCK
    w0 = pl.multiple_of(s0, Q_BLOCK)
    flag_row = lax.broadcasted_iota(jnp.int32, (LANES - HEAD_DIM, cols), 0) == 0
    qw = jnp.concatenate([qst, jnp.where(flag_row, NEG_INF, 0.0).astype(BF16)], axis=0)
    sw = _dot(kw_ref[0, 0, pl.ds(w0, wk), :], qw)
    rr = lax.broadcasted_iota(jnp.int32, (Q_BLOCK, cols), 0)
    swm = jnp.concatenate([jnp.where(rr > i_col, sw[:Q_BLOCK], NEG_INF), sw[Q_BLOCK:WINDOW],
                           jnp.where(rr <= i_col, sw[WINDOW:], NEG_INF)], axis=0)
    pw = jnp.exp2(swm - jnp.max(swm, axis=0, keepdims=True)).astype(BF16)
    acc_w = _dot(vwt_ref[0, 0, :, pl.ds(w0, wk)], pw)
    o_w = acc_w[:HEAD_DIM] / acc_w[HEAD_DIM:HEAD_DIM + 1]

    nb = imp_t.shape[0]
    jt = lax.broadcasted_iota(jnp.int32, (nb, Q_BLOCK), 0)
    cur = (s0 + lax.broadcasted_iota(jnp.int32, (nb, Q_BLOCK), 1)) // SEL_LEN
    forced = (jt == 0) | (jt == cur) | (jt == cur - 1)
    score = jnp.where(forced, FORCE_SCORE, jnp.where(jt <= cur, imp_t, -1.0))
    sel = _top_blocks_along_rows(score, nb)
    bias_t = jnp.where(sel & (score >= 0.0), 0.0, NEG_INF).astype(BF16)
    qaug_t = jnp.concatenate([jnp.concatenate([bias_t] * Q_PER_KV, axis=1), qst,
                              jnp.zeros((KA_COLS - nb - HEAD_DIM, cols), BF16)], axis=0)

    def scores(t):
        k0 = pl.multiple_of(t * SEL_TILE, SEL_TILE)
        return _dot(ka_ref[0, 0, pl.ds(k0, SEL_TILE), :], qaug_t)

    def weighted(pt, t):
        k0 = pl.multiple_of(t * SEL_TILE, SEL_TILE)
        return _dot(vat_ref[0, 0, :, pl.ds(k0, SEL_TILE)], pt)

    def stage(t, cur, causal, prefetch):
        prv = 1 - cur
        pv = weighted(p_scr[prv], jnp.maximum(t - 1, 0))
        s = s_scr[cur]
        if causal:
            kpos = t * SEL_TILE + lax.broadcasted_iota(jnp.int32, (SEL_TILE, cols), 0)
            s = jnp.where(kpos <= qpos, s, NEG_INF)
        m_prev = m_scr[...]
        m_new = jnp.maximum(m_prev, jnp.max(s, axis=0, keepdims=True))
        p_scr[cur] = jnp.exp2(s - m_new).astype(BF16)
        m_scr[...] = m_new
        acc_scr[...] = (acc_scr[...] + pv) * jnp.exp2(m_prev - m_new)
        if prefetch:
            s_scr[prv] = scores(t + 1)

    def trip(i, carry):
        stage(2 * i, 0, False, True)
        stage(2 * i + 1, 1, False, True)
        return carry

    n_pairs = s0 // (2 * SEL_TILE)
    s_scr[0] = scores(0)
    p_scr[1] = jnp.zeros((SEL_TILE, cols), BF16)
    m_scr[...] = jnp.full((1, cols), NEG_INF, F32)
    acc_scr[...] = jnp.zeros((V_ROWS, cols), F32)
    lax.fori_loop(0, n_pairs, trip, 0)
    stage(2 * n_pairs, 0, True, True)
    stage(2 * n_pairs + 1, 1, True, False)
    acc_s = acc_scr[...] + weighted(p_scr[1], 2 * n_pairs + 1)
    o_s = acc_s[:HEAD_DIM] / acc_s[HEAD_DIM:HEAD_DIM + 1]

    gate_t = (1.0 / (1.0 + jnp.exp(-(zg_ref[...] + bg_ref[...])))).T
    outs = []
    for r in range(Q_PER_KV):
        cs = slice(r * Q_BLOCK, (r + 1) * Q_BLOCK)
        outs.append(gate_t[r:r + 1] * o_c[:, cs] + gate_t[Q_PER_KV + r:Q_PER_KV + r + 1] * o_s[:, cs]
                    + gate_t[2 * Q_PER_KV + r:2 * Q_PER_KV + r + 1] * o_w[:, cs])
    out_ref[...] = jnp.concatenate(outs, axis=0).T.astype(BF16)


def _kv_layouts(rows, b, t, left_pad):
    k = rows[:, :KV_COLS].reshape(b, t, N_KV, HEAD_DIM).astype(BF16)
    v = rows[:, KV_COLS:].reshape(b, t, N_KV, HEAD_DIM).astype(BF16)
    kk = jnp.transpose(k, (0, 2, 1, 3))
    vt = jnp.concatenate([jnp.transpose(v, (0, 2, 3, 1)), jnp.ones((b, N_KV, 1, t), BF16),
                          jnp.zeros((b, N_KV, V_ROWS - HEAD_DIM - 1, t), BF16)], axis=2)
    if left_pad:
        kk = jnp.pad(kk, ((0, 0), (0, 0), (left_pad, 0), (0, LANES - HEAD_DIM)))
        flag = (np.arange(left_pad + t)[:, None] < left_pad) & (np.arange(LANES)[None, :] == HEAD_DIM)
        kk = kk + jnp.asarray(flag, BF16)
        vt = jnp.pad(vt, ((0, 0), (0, 0), (0, 0), (left_pad, 0)))
    return kk, vt


def _nsa_prompt(z, bg_pad, kvc, sel_rows, win_rows, b, t):
    nq = t // Q_BLOCK
    n_c = kvc.shape[2]
    n_blk = t // SEL_LEN
    assert t % (2 * SEL_TILE) == 0 and N_SEL <= n_blk <= LANES
    cols = Q_PER_KV * Q_BLOCK
    kc = jnp.transpose(kvc[:, 0].reshape(b, n_c, N_KV, HEAD_DIM).astype(BF16), (0, 2, 1, 3))
    vc = kvc[:, 1].reshape(b, n_c, N_KV, HEAD_DIM).astype(BF16)
    vct = jnp.concatenate([jnp.transpose(vc, (0, 2, 3, 1)), jnp.ones((b, N_KV, 1, n_c), BF16),
                           jnp.zeros((b, N_KV, V_ROWS - HEAD_DIM - 1, n_c), BF16)], axis=2)
    smapt = _selection_map(n_c, n_blk, LANES).T
    ks, vat = _kv_layouts(sel_rows, b, t, 0)
    key_in_blk = np.arange(t)[:, None] // SEL_LEN == np.arange(LANES)[None, :]
    ka = jnp.concatenate([jnp.broadcast_to(jnp.asarray(key_in_blk, BF16), (b, N_KV, t, LANES)), ks,
                          jnp.zeros((b, N_KV, t, KA_COLS - LANES - HEAD_DIM), BF16)], axis=3)
    kw, vwt = _kv_layouts(win_rows, b, t, WINDOW)
    per_bg = lambda shape: pl.BlockSpec((1, 1) + shape, lambda bb, g, n: (bb, g, 0, 0))
    return pl.pallas_call(
        _nsa_prompt_kernel,
        grid=(b, N_KV, nq),
        in_specs=[pl.BlockSpec((Q_BLOCK, Q_PER_KV * HEAD_DIM), lambda bb, g, n: (bb * nq + n, g)),
                  pl.BlockSpec((Q_BLOCK, LANES), lambda bb, g, n: (bb * nq + n, D_MODEL // LANES + g)),
                  pl.BlockSpec((1, LANES), lambda bb, g, n: (0, g)),
                  per_bg((n_c, HEAD_DIM)), per_bg((V_ROWS, n_c)),
                  pl.BlockSpec((LANES, n_c), lambda bb, g, n: (0, 0)),
                  per_bg((t, KA_COLS)), per_bg((V_ROWS, t)),
                  per_bg((WINDOW + t, LANES)), per_bg((V_ROWS, WINDOW + t))],
        out_specs=pl.BlockSpec((Q_BLOCK, Q_PER_KV * HEAD_DIM), lambda bb, g, n: (bb * nq + n, g)),
        out_shape=jax.ShapeDtypeStruct((b * t, D_MODEL), BF16),
        scratch_shapes=[pltpu.VMEM((2, SEL_TILE, cols), F32), pltpu.VMEM((2, SEL_TILE, cols), BF16),
                        pltpu.VMEM((1, cols), F32), pltpu.VMEM((V_ROWS, cols), F32)],
        compiler_params=_params(("parallel", "parallel", "arbitrary")),
        name="nsa_prompt",
    )(z, z, bg_pad, kc, vct, smapt, ka, vat, kw, vwt)


SROWS = 32
SBLK = 256


def _samp_cmpwin_kernel(qbd_ref, kvc_ref, smap_ref, win_ref, wnew_ref, oc_ref, ow_ref, imp_ref, *, qpos):
    qf = qbd_ref[0] * SCALE
    qb = qf.astype(BF16)
    kc = kvc_ref[0, 0].astype(BF16)
    vc = kvc_ref[0, 1].astype(BF16)
    n_c = kc.shape[0]
    sc = _dot_nt(qb, kc)
    cidx = lax.broadcasted_iota(jnp.int32, (SROWS, n_c), 1)
    mc = (cidx >= 1) & (cidx * CMP_STRIDE + (CMP_STRIDE - 1) <= qpos)
    scm = jnp.where(mc, sc, NEG_INF)
    p = jnp.where(mc, jnp.exp(scm - jnp.max(scm, axis=1, keepdims=True)), 0.0)
    l = jnp.sum(p, axis=1, keepdims=True)
    inv = jnp.where(l > 0.0, 1.0 / l, 0.0)
    oc_ref[0] = _dot(p.astype(BF16), vc) * inv
    pn = p * inv
    psum = pn[0:8] + pn[8:16] + pn[16:24] + pn[24:32]
    p_hi = psum.astype(BF16)
    p_lo = (psum - p_hi.astype(F32)).astype(BF16)
    imp_ref[0] = _dot(p_hi, smap_ref[...]) + _dot(p_lo, smap_ref[...])
    win = win_ref[0]
    n_w = win.shape[1]
    sw = _dot(qb, win[:KV_COLS].astype(BF16))
    widx = lax.broadcasted_iota(jnp.int32, (SROWS, n_w), 1)
    dist = n_w - widx
    mw = (dist < WINDOW) & (qpos - dist >= 0)
    swm = jnp.where(mw, sw, NEG_INF)
    wnew = wnew_ref[0]
    s_new = jnp.sum(qb.astype(F32) * wnew[:, :KV_COLS].astype(BF16).astype(F32), axis=1, keepdims=True)
    m = jnp.maximum(jnp.max(swm, axis=1, keepdims=True), s_new)
    pw = jnp.exp(swm - m)
    p_new = jnp.exp(s_new - m)
    lw = jnp.sum(pw, axis=1, keepdims=True) + p_new
    ow = _dot_nt(pw.astype(BF16), win[KV_COLS:].astype(BF16)) + p_new * wnew[:, KV_COLS:]
    ow_ref[0] = ow / lw


def _samp_topk_kernel(imp_ref, bias_ref, *, cur):
    imp = imp_ref[...]
    jidx = lax.broadcasted_iota(jnp.int32, imp.shape, 1)
    forced = (jidx == 0) | (jidx == cur) | (jidx == cur - 1)
    score = jnp.where(forced, FORCE_SCORE, jnp.where(jidx <= cur, imp, -1.0))
    work = score
    sel = jnp.zeros(imp.shape, F32)
    for _ in range(N_SEL):
        mx = jnp.max(work, axis=1, keepdims=True)
        first = jnp.min(jnp.where(work == mx, jidx, SBLK), axis=1, keepdims=True)
        hit = jidx == first
        sel = jnp.where(hit, 1.0, sel)
        work = jnp.where(hit, -3e38, work)
    bias_ref[...] = jnp.where((sel > 0.0) & (score >= 0.0), 0.0, NEG_INF)


def _samp_sel_kernel(pt_ref, *refs, npg, qpos):
    page_refs = refs[:npg]
    (qbd_ref, bias_ref, knew_ref, oc_ref, ow_ref, zg_ref, bg_ref, out_ref, m_ref, acc_ref, l_ref) = refs[npg:]
    j = pl.program_id(1)
    keys = npg * PAGE_SIZE
    qb = (qbd_ref[0] * SCALE).astype(BF16)

    @pl.when(j == 0)
    def _():
        m_ref[...] = jnp.full_like(m_ref, NEG_INF)
        acc_ref[...] = jnp.zeros_like(acc_ref)
        l_ref[...] = jnp.zeros_like(l_ref)

    bias8 = bias_ref[0].astype(BF16)
    bias32 = jnp.concatenate([bias8] * Q_PER_KV, axis=0)
    blk = lax.broadcasted_iota(jnp.int32, (SBLK, keys), 0)
    key = lax.broadcasted_iota(jnp.int32, (SBLK, keys), 1)
    onehot = jnp.where(blk == j * (keys // SEL_LEN) + key // SEL_LEN, 1.0, 0.0).astype(BF16)
    s = jnp.concatenate([_dot(qb, page_refs[t][0, :KV_COLS, :].astype(BF16)) for t in range(npg)], axis=1)
    s = s + _dot(bias32, onehot)
    m_old = m_ref[...]
    m_new = jnp.maximum(m_old, jnp.max(s, axis=1, keepdims=True))
    p = jnp.exp(s - m_new)
    alpha = jnp.exp(m_old - m_new)
    pv = _dot_nt(p[:, :PAGE_SIZE].astype(BF16), page_refs[0][0, KV_COLS:, :].astype(BF16))
    for t in range(1, npg):
        pv = pv + _dot_nt(p[:, t * PAGE_SIZE:(t + 1) * PAGE_SIZE].astype(BF16),
                          page_refs[t][0, KV_COLS:, :].astype(BF16))
    m_ref[...] = m_new
    l_ref[...] = alpha * l_ref[...] + jnp.sum(p, axis=1, keepdims=True)
    acc_ref[...] = alpha * acc_ref[...] + pv

    @pl.when(j == pl.num_programs(1) - 1)
    def _():
        knew = knew_ref[0]
        s_new = jnp.sum(qb.astype(F32) * knew[:, :KV_COLS].astype(BF16).astype(F32), axis=1, keepdims=True)
        m_fin = jnp.maximum(m_ref[...], s_new)
        a = jnp.exp(m_ref[...] - m_fin)
        p_new = jnp.exp(s_new - m_fin)
        os_ = (a * acc_ref[...] + p_new * knew[:, KV_COLS:]) / (a * l_ref[...] + p_new)
        gate = 1.0 / (1.0 + jnp.exp(-(zg_ref[0] + bg_ref[0])))
        o = gate[:, 0:1] * oc_ref[0] + gate[:, 1:2] * os_ + gate[:, 2:3] * ow_ref[0]
        gsel = lax.broadcasted_iota(jnp.int32, (SROWS, HEAD_DIM), 0) & 7
        res = jnp.zeros((SROWS, HEAD_DIM), F32)
        for g in range(N_KV):
            res = res + jnp.where(gsel == g, o[:, g * HEAD_DIM:(g + 1) * HEAD_DIM], 0.0)
        out_ref[0] = res


def _nsa_sample(z, b_g, kvc, cache_sel, page_table, sel_new, win_state, win_new, past_len):
    n_seq, n_pages = page_table.shape
    npg = PAGES_PER_STEP
    qpos = past_len
    n_blk = -(-(past_len + 1) // SEL_LEN)
    assert n_blk <= SBLK and n_pages % npg == 0
    n_c = kvc.shape[2]
    n_w = win_state.shape[2]
    q4 = z[:, :D_MODEL].reshape(n_seq, N_KV, Q_PER_KV, HEAD_DIM)
    eye = jnp.eye(N_KV, dtype=F32)
    qbd = jnp.transpose(q4[:, :, :, None, :] * eye[None, :, None, :, None], (0, 2, 1, 3, 4))
    qbd = jnp.pad(qbd, ((0, 0), (0, 0), (0, 8 - N_KV), (0, 0), (0, 0))).reshape(n_seq, SROWS, KV_COLS)
    def gate_rows(a):
        n = a.shape[0]
        a = jnp.transpose(a.reshape(n, N_KV, Q_PER_KV, 3), (0, 2, 1, 3))
        return jnp.pad(a, ((0, 0), (0, 0), (0, 8 - N_KV), (0, LANES - 3))).reshape(n, SROWS, LANES)

    zg = gate_rows(z[:, D_MODEL:D_MODEL + 3 * N_HEADS])
    bg = gate_rows(b_g[None, :])
    smap = _selection_map(n_c, n_blk, SBLK)
    per_seq = lambda shape: pl.BlockSpec((1,) + shape, lambda s: (s,) + (0,) * len(shape))
    o_c, o_w, imp = pl.pallas_call(
        functools.partial(_samp_cmpwin_kernel, qpos=qpos),
        grid=(n_seq,),
        in_specs=[per_seq((SROWS, KV_COLS)), per_seq((2, n_c, KV_COLS)),
                  pl.BlockSpec((n_c, SBLK), lambda s: (0, 0)), per_seq((2 * KV_COLS, n_w)),
                  per_seq((1, 2 * KV_COLS))],
        out_specs=(per_seq((SROWS, KV_COLS)), per_seq((SROWS, KV_COLS)), per_seq((8, SBLK))),
        out_shape=(jax.ShapeDtypeStruct((n_seq, SROWS, KV_COLS), F32),
                   jax.ShapeDtypeStruct((n_seq, SROWS, KV_COLS), F32),
                   jax.ShapeDtypeStruct((n_seq, 8, SBLK), F32)),
        compiler_params=_params(("parallel",)),
        name="nsa_sample_cmpwin",
    )(qbd, kvc, smap, win_state, win_new[:, None, :])
    bias = pl.pallas_call(
        functools.partial(_samp_topk_kernel, cur=qpos // SEL_LEN),
        out_shape=jax.ShapeDtypeStruct((n_seq * 8, SBLK), F32),
        name="nsa_sample_topk",
    )(imp.reshape(n_seq * 8, SBLK)).reshape(n_seq, 8, SBLK)

    def page_spec(t):
        return pl.BlockSpec((1, 2 * KV_COLS, PAGE_SIZE), lambda s, j, pt: (pt[s * n_pages + j * npg + t], 0, 0))

    seq_blk = lambda shape: pl.BlockSpec((1,) + shape, lambda s, j, pt: (s,) + (0,) * len(shape))
    out = pl.pallas_call(
        functools.partial(_samp_sel_kernel, npg=npg, qpos=qpos),
        grid_spec=pltpu.PrefetchScalarGridSpec(
            num_scalar_prefetch=1,
            grid=(n_seq, n_pages // npg),
            in_specs=[page_spec(t) for t in range(npg)] + [
                seq_blk((SROWS, KV_COLS)), seq_blk((8, SBLK)), seq_blk((1, 2 * KV_COLS)),
                seq_blk((SROWS, KV_COLS)), seq_blk((SROWS, KV_COLS)), seq_blk((SROWS, LANES)),
                pl.BlockSpec((1, SROWS, LANES), lambda s, j, pt: (0, 0, 0))],
            out_specs=seq_blk((SROWS, HEAD_DIM)),
            scratch_shapes=[pltpu.VMEM((SROWS, 1), F32), pltpu.VMEM((SROWS, KV_COLS), F32),
                            pltpu.VMEM((SROWS, 1), F32)]),
        out_shape=jax.ShapeDtypeStruct((n_seq, SROWS, HEAD_DIM), F32),
        compiler_params=_params(("parallel", "arbitrary")),
        name="nsa_sample_sel",
    )(page_table.reshape(-1), *([cache_sel] * npg), qbd, bias, sel_new[:, None, :], o_c, o_w, zg, bg)
    out = out.reshape(n_seq, Q_PER_KV, 8, HEAD_DIM)[:, :, :N_KV]
    return jnp.transpose(out, (0, 2, 1, 3)).reshape(n_seq, D_MODEL).astype(BF16)


def kernel(x_prompt, x_sample, cache_cmp_kv, cache_sel_kv, state_win_kv, page_table, a_norm_g, a_w_uv, a_v_norm_g, a_w_s, a_b_s, a_w_o, kv_norm_g, w_kv, cmp_pe_k, cmp_w1_k, cmp_w2_k, cmp_pe_v, cmp_w1_v, cmp_w2_v, b_norm_g, b_w_qg, b_b_g, b_w_o, mlp_norm_g, mlp_w_up, mlp_w_down, final_norm_g):
    b, t, _ = x_prompt.shape
    n_seq, dec_seq, _ = x_sample.shape
    n_pool, page, _, _, _ = cache_cmp_kv.shape
    n_pages = page_table.shape[1]
    past_len = n_pages * page
    win_buf = state_win_kv.shape[1]
    depth = mlp_norm_g.shape[0]
    n_a = a_norm_g.shape[0]
    assert dec_seq == 1 and page == PAGE_SIZE and t % PAGE_SIZE == 0
    cmp_w = (cmp_pe_k, cmp_w1_k, cmp_w2_k, cmp_pe_v, cmp_w1_v, cmp_w2_v)
    kv_shape = (2, N_KV, HEAD_DIM)
    gate_w = 3 * N_HEADS

    hp = x_prompt.reshape(b * t, D_MODEL)
    hs = x_sample.reshape(n_seq, D_MODEL)
    a_v_rows = []
    attn_p = attn_s = None
    for layer in range(depth):
        last = layer == depth - 1
        fin = final_norm_g if last else None
        if layer < n_a:
            i = layer
            hp = _gmlp_layer(hp, a_norm_g[i], a_w_uv[i], a_v_norm_g[i], a_w_s[i], a_b_s[i], a_w_o[i], chunked=True)
            hs, v_s = _gmlp_layer(hs, a_norm_g[i], a_w_uv[i], a_v_norm_g[i], a_w_s[i], a_b_s[i], a_w_o[i],
                                  chunked=False)
            a_v_rows.append(v_s.reshape(n_seq, 1, D_MODEL))
            hp = _mlp_layer(hp, mlp_norm_g[layer], mlp_w_up[layer], mlp_w_down[layer], final_g=fin)
            hs = _mlp_layer(hs, mlp_norm_g[layer], mlp_w_up[layer], mlp_w_down[layer], final_g=fin)
            continue
        if layer == n_a:
            cmp_p, sel_p, win_p = _norm_matmul(hp, kv_norm_g, w_kv, 3)
            cmp_s, sel_s, win_s = _norm_matmul(hs, kv_norm_g, w_kv, 3)
            ncol = 2 * KV_COLS // LANES
            kvc_p = _compress(cmp_p.reshape(b * t // PAGE_SIZE, ncol * PAGE_SIZE, LANES),
                              jnp.arange(b * t // PAGE_SIZE, dtype=jnp.int32).reshape(b, t // PAGE_SIZE), *cmp_w,
                              channel_major=False)

            def channel_major(a):
                return jnp.transpose(a, (0, 2, 3, 4, 1)).reshape(a.shape[0], 2 * KV_COLS, a.shape[1])

            kvc_s = _compress(channel_major(cache_cmp_kv), page_table, *cmp_w, channel_major=True)
            cache_sel = channel_major(cache_sel_kv)
            win_state = channel_major(state_win_kv)
        j = layer - n_a
        wg = b_w_qg[j][:, D_MODEL:].reshape(D_MODEL, N_KV, Q_PER_KV, 3)
        wg = jnp.transpose(wg, (0, 1, 3, 2)).reshape(D_MODEL, N_KV, 3 * Q_PER_KV)
        wg = jnp.pad(wg, ((0, 0), (0, 0), (0, LANES - 3 * Q_PER_KV))).reshape(D_MODEL, N_KV * LANES)
        bg = jnp.transpose(b_b_g[j].reshape(N_KV, Q_PER_KV, 3), (0, 2, 1)).reshape(N_KV, 3 * Q_PER_KV)
        bg = jnp.pad(bg, ((0, 0), (0, LANES - 3 * Q_PER_KV))).reshape(1, N_KV * LANES)
        (z_p,) = _norm_matmul(hp, b_norm_g[j], jnp.concatenate([b_w_qg[j][:, :D_MODEL], wg], axis=1), 1)
        attn_p = _nsa_prompt(z_p, bg, kvc_p, sel_p, win_p, b, t)
        w_s_pad = jnp.pad(b_w_qg[j], ((0, 0), (0, LANES - gate_w % LANES)))
        (z_s,) = _norm_matmul(hs, b_norm_g[j], w_s_pad, 1)
        attn_s = _nsa_sample(z_s, b_b_g[j], kvc_s, cache_sel, page_table, sel_s, win_state, win_s, past_len)
        hp = _mlp_layer(hp, mlp_norm_g[layer], mlp_w_up[layer], mlp_w_down[layer], attn=attn_p, w_o=b_w_o[j],
                        final_g=fin)
        hs = _mlp_layer(hs, mlp_norm_g[layer], mlp_w_up[layer], mlp_w_down[layer], attn=attn_s, w_o=b_w_o[j],
                        final_g=fin)

    y_prompt = hp.reshape(b, t, D_MODEL)
    y_sample = hs.reshape(n_seq, 1, D_MODEL)
    nw = min(WINDOW, t)
    win_kv_prompt = win_p.reshape(b, t, *kv_shape)[:, t - nw:]
    win_kv_sample = jnp.concatenate([state_win_kv, win_s.reshape(n_seq, 1, *kv_shape)], axis=1)[:, dec_seq:]
    return (y_prompt, y_sample, cmp_p.reshape(b, t, *kv_shape), sel_p.reshape(b, t, *kv_shape), win_kv_prompt,
            cmp_s.reshape(n_seq, 1, *kv_shape), sel_s.reshape(n_seq, 1, *kv_shape), win_kv_sample,
            jnp.stack(a_v_rows, axis=0))
```

```python
import functools

import numpy as np
import jax
import jax.numpy as jnp
from jax import lax
from jax.experimental import pallas as pl
from jax.experimental.pallas import tpu as pltpu

F32 = jnp.float32
BF16 = jnp.bfloat16

D_MODEL = 1024
D_FF = 4 * D_MODEL
CHUNK = 128
A_GROUP_DIM = 128
A_GROUPS = D_MODEL // A_GROUP_DIM
HEAD_DIM = 64
N_HEADS = D_MODEL // HEAD_DIM
N_KV = 4
Q_PER_KV = N_HEADS // N_KV
KV_COLS = N_KV * HEAD_DIM
CMP_STRIDE = 16
CMP_LEN = 2 * CMP_STRIDE
CMP_HIDDEN = 4 * HEAD_DIM
SEL_LEN = 64
N_SEL = 16
WINDOW = 512
Q_BLOCK = 256
PAGE_SIZE = 128
EPS = 1e-6
NEG_INF = -1e30
FORCE_SCORE = 1e6
SCALE = HEAD_DIM ** -0.5
LOG2E = 1.4426950408889634
LANES = 128
V_ROWS = 80
KA_COLS = 256
SEL_TILE = 512
PAGES_PER_STEP = 8
VMEM_LIMIT = 56 * 1024 * 1024


def _params(sem):
    return pltpu.CompilerParams(dimension_semantics=sem, vmem_limit_bytes=VMEM_LIMIT)


def _rms(x, g):
    return x * lax.rsqrt(jnp.mean(x * x, axis=-1, keepdims=True) + EPS) * g


def _dot(a, b):
    return jnp.dot(a, b, preferred_element_type=F32)


def _dot_nt(a, b):
    return lax.dot_general(a, b, (((1,), (1,)), ((), ())), preferred_element_type=F32)


def _gmlp_kernel(x_ref, g_ref, wuv_ref, gv_ref, ws_ref, bias_ref, wo_ref, out_ref, *rest, chunked, tm):
    x = x_ref[...]
    xn = _rms(x, g_ref[...]).astype(BF16)
    uv = _dot(xn, wuv_ref[...])
    u = uv[:, :D_MODEL]
    v = _rms(uv[:, D_MODEL:], gv_ref[...])
    if chunked:
        gated_ref = rest[0]
        row = lax.broadcasted_iota(jnp.int32, (CHUNK, CHUNK), 0)
        col = lax.broadcasted_iota(jnp.int32, (CHUNK, CHUNK), 1)
        for g in range(A_GROUPS):
            cs = slice(g * A_GROUP_DIM, (g + 1) * A_GROUP_DIM)
            w = jnp.where(row >= col, ws_ref[g], 0.0).astype(BF16)
            for c in range(tm // CHUNK):
                rs = slice(c * CHUNK, (c + 1) * CHUNK)
                s = _dot(w, v[rs, cs].astype(BF16)) + bias_ref[:, cs]
                gated_ref[rs, cs] = (u[rs, cs] * s).astype(BF16)
        gated = gated_ref[...]
    else:
        v_ref = rest[0]
        v_ref[...] = v
        gated = (u * (v * ws_ref[...] + bias_ref[...])).astype(BF16)
    out_ref[...] = x + _dot(gated, wo_ref[...])


def _gmlp_layer(h, g, w_uv, gv, w_s, b_s, w_o, *, chunked):
    m = h.shape[0]
    full = lambda shape: pl.BlockSpec(shape, lambda i: (0,) * len(shape))
    if chunked:
        tm = 256
        ws = w_s
        bias = jnp.repeat(b_s.T, A_GROUP_DIM, axis=1)
        ws_spec = full((A_GROUPS, CHUNK, CHUNK))
        bias_spec = full((CHUNK, D_MODEL))
        out_shape = jax.ShapeDtypeStruct((m, D_MODEL), F32)
        out_specs = pl.BlockSpec((tm, D_MODEL), lambda i: (i, 0))
        scratch = [pltpu.VMEM((tm, D_MODEL), BF16)]
    else:
        tm = m
        ws = jnp.repeat(w_s[:, 0, 0], A_GROUP_DIM)[None, :]
        bias = jnp.repeat(b_s[:, 0], A_GROUP_DIM)[None, :]
        ws_spec = full((1, D_MODEL))
        bias_spec = full((1, D_MODEL))
        out_shape = (jax.ShapeDtypeStruct((m, D_MODEL), F32), jax.ShapeDtypeStruct((m, D_MODEL), F32))
        out_specs = (pl.BlockSpec((tm, D_MODEL), lambda i: (i, 0)), pl.BlockSpec((tm, D_MODEL), lambda i: (i, 0)))
        scratch = []
    assert m % tm == 0
    return pl.pallas_call(
        functools.partial(_gmlp_kernel, chunked=chunked, tm=tm),
        grid=(m // tm,),
        in_specs=[pl.BlockSpec((tm, D_MODEL), lambda i: (i, 0)), full((1, D_MODEL)),
                  full((D_MODEL, 2 * D_MODEL)), full((1, D_MODEL)), ws_spec, bias_spec,
                  full((D_MODEL, D_MODEL))],
        out_specs=out_specs, out_shape=out_shape, scratch_shapes=scratch,
        compiler_params=_params(("parallel",)),
        name="gmlp_prompt" if chunked else "gmlp_sample",
    )(h, g[None, :], w_uv.astype(BF16), gv[None, :], ws, bias, w_o.astype(BF16))


def _mlp_kernel(*refs, has_attn, has_final):
    refs = list(refs)
    x_ref = refs.pop(0)
    attn_ref = refs.pop(0) if has_attn else None
    wo_ref = refs.pop(0) if has_attn else None
    g_ref, wup_ref, wdn_ref = refs.pop(0), refs.pop(0), refs.pop(0)
    fg_ref = refs.pop(0) if has_final else None
    out_ref, h_ref, xn_ref, acc_ref = refs
    j = pl.program_id(1)

    @pl.when(j == 0)
    def _():
        h = x_ref[...]
        if has_attn:
            h = h + _dot(attn_ref[...], wo_ref[...])
        h_ref[...] = h
        xn_ref[...] = _rms(h, g_ref[...]).astype(BF16)
        acc_ref[...] = jnp.zeros_like(acc_ref)

    a = _dot(xn_ref[...], wup_ref[...])
    a = jnp.square(jnp.maximum(a, 0.0)).astype(BF16)
    acc_ref[...] += _dot(a, wdn_ref[...])

    @pl.when(j == pl.num_programs(1) - 1)
    def _():
        h = h_ref[...] + acc_ref[...]
        if has_final:
            h = _rms(h, fg_ref[...])
        out_ref[...] = h


def _mlp_layer(h, g, w_up, w_down, *, attn=None, w_o=None, final_g=None):
    m = h.shape[0]
    tm = min(m, 1024)
    tf = 512
    assert m % tm == 0 and D_FF % tf == 0
    has_attn, has_final = attn is not None, final_g is not None
    row = pl.BlockSpec((tm, D_MODEL), lambda i, j: (i, 0))
    vec = pl.BlockSpec((1, D_MODEL), lambda i, j: (0, 0))
    args, specs = [h], [row]
    if has_attn:
        args += [attn, w_o.astype(BF16)]
        specs += [row, pl.BlockSpec((D_MODEL, D_MODEL), lambda i, j: (0, 0))]
    args += [g[None, :], w_up.astype(BF16), w_down.astype(BF16)]
    specs += [vec, pl.BlockSpec((D_MODEL, tf), lambda i, j: (0, j)), pl.BlockSpec((tf, D_MODEL), lambda i, j: (j, 0))]
    if has_final:
        args.append(final_g[None, :])
        specs.append(vec)
    return pl.pallas_call(
        functools.partial(_mlp_kernel, has_attn=has_attn, has_final=has_final),
        grid=(m // tm, D_FF // tf),
        in_specs=specs, out_specs=row,
        out_shape=jax.ShapeDtypeStruct((m, D_MODEL), F32),
        scratch_shapes=[pltpu.VMEM((tm, D_MODEL), F32), pltpu.VMEM((tm, D_MODEL), BF16),
                        pltpu.VMEM((tm, D_MODEL), F32)],
        compiler_params=_params(("parallel", "arbitrary")),
        name="mlp",
    )(*args)


def _norm_matmul_kernel(x_ref, g_ref, w_ref, *out_refs):
    xn = _rms(x_ref[...], g_ref[...]).astype(BF16)
    y = _dot(xn, w_ref[...])
    wd = y.shape[1] // len(out_refs)
    for k, o_ref in enumerate(out_refs):
        o_ref[...] = y[:, k * wd:(k + 1) * wd]


def _norm_matmul(h, g, w, n_out):
    m, n = h.shape[0], w.shape[1]
    tm = min(m, 512)
    wd = n // n_out
    assert m % tm == 0 and n % n_out == 0 and wd % LANES == 0
    outs = pl.pallas_call(
        _norm_matmul_kernel,
        grid=(m // tm,),
        in_specs=[pl.BlockSpec((tm, D_MODEL), lambda i: (i, 0)), pl.BlockSpec((1, D_MODEL), lambda i: (0, 0)),
                  pl.BlockSpec((D_MODEL, n), lambda i: (0, 0))],
        out_specs=tuple(pl.BlockSpec((tm, wd), lambda i: (i, 0)) for _ in range(n_out)),
        out_shape=tuple(jax.ShapeDtypeStruct((m, wd), F32) for _ in range(n_out)),
        compiler_params=_params(("parallel",)),
        name="norm_matmul",
    )(h, g[None, :], w.astype(BF16))
    return outs


def _compress_kernel(pt_ref, *refs, npg, channel_major):
    page_refs = refs[:npg]
    w1_ref, pe_ref, w2_ref, out_ref, z_ref, aprev_ref = refs[npg:npg + 6]
    j = pl.program_id(1)
    grp = PAGE_SIZE // CMP_STRIDE
    sec = grp * npg
    rows = N_KV * sec

    @pl.when(j == 0)
    def _():
        aprev_ref[...] = jnp.zeros_like(aprev_ref)

    ncol = 2 * KV_COLS // LANES
    if channel_major:
        tr_ref = refs[npg + 6]
        for t in range(npg):
            for c in range(ncol):
                tr_ref[t * ncol + c] = page_refs[t][0, c * LANES:(c + 1) * LANES, :].T

        def rows_of(t, c, p):
            return tr_ref[t * ncol + c, pl.ds(p, grp, stride=CMP_STRIDE), :]
    else:
        def rows_of(t, c, p):
            return page_refs[t][0, pl.ds(ncol * p + c, grp, stride=ncol * CMP_STRIDE), :]

    for t in range(npg):
        for q in range(CMP_STRIDE // 2):
            for c in range(ncol):
                xa = rows_of(t, c, 2 * q)
                xb = rows_of(t, c, 2 * q + 1)
                kv = c // (ncol // 2)
                for half in range(LANES // HEAD_DIM):
                    g = (c % (ncol // 2)) * (LANES // HEAD_DIM) + half
                    hs = slice(half * HEAD_DIM, (half + 1) * HEAD_DIM)
                    piece = jnp.concatenate([xa[:, hs], xb[:, hs]], axis=1)
                    z_ref[kv, g * sec + t * grp:g * sec + (t + 1) * grp, q * LANES:(q + 1) * LANES] = piece

    rid = lax.broadcasted_iota(jnp.int32, (rows, CMP_HIDDEN), 0) % sec
    for kv in range(2):
        z = z_ref[kv]
        za = (z + pe_ref[kv, 0:1, :]).astype(BF16)
        zb = (z + pe_ref[kv, 1:2, :]).astype(BF16)
        a = _dot(za, w1_ref[kv, :, :CMP_HIDDEN])
        b = _dot(zb, w1_ref[kv, :, CMP_HIDDEN:])
        a_prev = jnp.where(rid == 0, pltpu.roll(aprev_ref[kv], rows - (sec - 1), 0), pltpu.roll(a, 1, 0))
        aprev_ref[kv] = a
        hid = jax.nn.gelu(a_prev + b).astype(BF16)
        o = _dot(hid, w2_ref[kv])
        out_ref[0, kv] = jnp.concatenate([o[g * sec:(g + 1) * sec] for g in range(N_KV)], axis=1)


def _compress(pages, page_table, pe_k, w1_k, w2_k, pe_v, w1_v, w2_v, *, channel_major):
    n_seq, n_pages = page_table.shape
    npg = PAGES_PER_STEP
    assert n_pages % npg == 0
    grp = PAGE_SIZE // CMP_STRIDE
    half = CMP_STRIDE * HEAD_DIM

    def split(w1):
        return jnp.concatenate([w1[:half], w1[half:]], axis=1)

    w1 = jnp.stack([split(w1_k), split(w1_v)]).astype(BF16)
    pe = jnp.stack([pe_k.reshape(2, half), pe_v.reshape(2, half)])
    w2 = jnp.stack([w2_k, w2_v]).astype(BF16)

    ncol = 2 * KV_COLS // LANES
    scratch = [pltpu.VMEM((2, N_KV * grp * npg, half), F32), pltpu.VMEM((2, N_KV * grp * npg, CMP_HIDDEN), F32)]
    if channel_major:
        scratch.append(pltpu.VMEM((npg * ncol, PAGE_SIZE, LANES), F32))

    def page_spec(t):
        return pl.BlockSpec((1, ncol * PAGE_SIZE, LANES), lambda s, j, pt: (pt[s * n_pages + j * npg + t], 0, 0))

    const = lambda shape: pl.BlockSpec(shape, lambda s, j, pt: (0,) * len(shape))
    return pl.pallas_call(
        functools.partial(_compress_kernel, npg=npg, channel_major=channel_major),
        grid_spec=pltpu.PrefetchScalarGridSpec(
            num_scalar_prefetch=1,
            grid=(n_seq, n_pages // npg),
            in_specs=[page_spec(t) for t in range(npg)] + [const((2, half, 2 * CMP_HIDDEN)), const((2, 2, half)),
                                                           const((2, CMP_HIDDEN, HEAD_DIM))],
            out_specs=pl.BlockSpec((1, 2, grp * npg, KV_COLS), lambda s, j, pt: (s, 0, j, 0)),
            scratch_shapes=scratch),
        out_shape=jax.ShapeDtypeStruct((n_seq, 2, n_pages * grp, KV_COLS), F32),
        compiler_params=_params(("parallel", "arbitrary")),
        name="compress",
    )(page_table.reshape(-1), *([pages] * npg), w1, pe, w2)


def _selection_map(n_rows, n_blk, n_cols):
    i = np.arange(n_rows)[:, None] - 1
    j = np.arange(n_cols)[None, :]
    lo = np.maximum(i * CMP_STRIDE, j * SEL_LEN)
    hi = np.minimum(i * CMP_STRIDE + CMP_LEN, (j + 1) * SEL_LEN)
    m = np.maximum(hi - lo, 0) // CMP_STRIDE
    m = np.where((i >= 0) & (j < n_blk), m, 0)
    return jnp.asarray(m, dtype=BF16)


def _top_blocks_along_rows(score, n_rows):
    ridx = lax.broadcasted_iota(jnp.int32, score.shape, 0)
    work = score
    for _ in range(N_SEL):
        mx = jnp.max(work, axis=0, keepdims=True)
        first = jnp.min(jnp.where(work == mx, ridx, n_rows), axis=0, keepdims=True)
        work = jnp.where(ridx == first, -3e38, work)
    return work < -1e38


def _nsa_prompt_kernel(zq_ref, zg_ref, bg_ref, kc_ref, vct_ref, smapt_ref, ka_ref, vat_ref, kw_ref, vwt_ref,
                       out_ref, s_scr, p_scr, m_scr, acc_scr):
    n = pl.program_id(2)
    s0 = n * Q_BLOCK
    cols = Q_PER_KV * Q_BLOCK
    qt = (zq_ref[...] * (SCALE * LOG2E)).T
    qst = jnp.concatenate([qt[r * HEAD_DIM:(r + 1) * HEAD_DIM] for r in range(Q_PER_KV)], axis=1).astype(BF16)
    i_col = lax.broadcasted_iota(jnp.int32, (1, cols), 1) & (Q_BLOCK - 1)
    qpos = s0 + i_col

    n_c = kc_ref.shape[2]
    sc = _dot(kc_ref[0, 0], qst)
    cidx = lax.broadcasted_iota(jnp.int32, (n_c, cols), 0)
    lim = jnp.maximum((qpos - (CMP_STRIDE - 1)) // CMP_STRIDE, 0)
    scm = jnp.where((cidx >= 1) & (cidx <= lim), sc, NEG_INF)
    mx_c = jnp.where(lim >= 1, jnp.max(scm, axis=0, keepdims=True), -NEG_INF)
    p = jnp.exp2(scm - mx_c)
    acc_c = _dot(vct_ref[0, 0], p.astype(BF16))
    l_c = acc_c[HEAD_DIM:HEAD_DIM + 1]
    o_c = acc_c[:HEAD_DIM] * jnp.where(l_c > 0.0, 1.0 / l_c, 0.0)
    l32 = jnp.sum(p, axis=0, keepdims=True)
    pn = p * jnp.where(l32 > 0.0, 1.0 / l32, 0.0)
    psum = pn[:, 0:Q_BLOCK]
    for r in range(1, Q_PER_KV):
        psum = psum + pn[:, r * Q_BLOCK:(r + 1) * Q_BLOCK]
    p_hi = psum.astype(BF16)
    p_lo = (psum - p_hi.astype(F32)).astype(BF16)
    imp_t = _dot(smapt_ref[...], p_hi) + _dot(smapt_ref[...], p_lo)

    wk = WINDOW + Q_BLOCK
    w0 = pl.multiple_of(s0, Q_BLOCK)
    flag_row = lax.broadcasted_iota(jnp.int32, (LANES - HEAD_DIM, cols), 0) == 0
    qw = jnp.concatenate([qst, jnp.where(flag_row, NEG_INF, 0.0).astype(BF16)], axis=0)
    sw = _dot(kw_ref[0, 0, pl.ds(w0, wk), :], qw)
    rr = lax.broadcasted_iota(jnp.int32, (Q_BLOCK, cols), 0)
    swm = jnp.concatenate([jnp.where(rr > i_col, sw[:Q_BLOCK], NEG_INF), sw[Q_BLOCK:WINDOW],
                           jnp.where(rr <= i_col, sw[WINDOW:], NEG_INF)], axis=0)
    pw = jnp.exp2(swm - jnp.max(swm, axis=0, keepdims=True)).astype(BF16)
    acc_w = _dot(vwt_ref[0, 0, :, pl.ds(w0, wk)], pw)
    o_w = acc_w[:HEAD_DIM] / acc_w[HEAD_DIM:HEAD_DIM + 1]

    nb = imp_t.shape[0]
    jt = lax.broadcasted_iota(jnp.int32, (nb, Q_BLOCK), 0)
    cur = (s0 + lax.broadcasted_iota(jnp.int32, (nb, Q_BLOCK), 1)) // SEL_LEN
    forced = (jt == 0) | (jt == cur) | (jt == cur - 1)
    score = jnp.where(forced, FORCE_SCORE, jnp.where(jt <= cur, imp_t, -1.0))
    sel = _top_blocks_along_rows(score, nb)
    bias_t = jnp.where(sel & (score >= 0.0), 0.0, NEG_INF).astype(BF16)
    qaug_t = jnp.concatenate([jnp.concatenate([bias_t] * Q_PER_KV, axis=1), qst,
                              jnp.zeros((KA_COLS - nb - HEAD_DIM, cols), BF16)], axis=0)

    def scores(t):
        k0 = pl.multiple_of(t * SEL_TILE, SEL_TILE)
        return _dot(ka_ref[0, 0, pl.ds(k0, SEL_TILE), :], qaug_t)

    def weighted(pt, t):
        k0 = pl.multiple_of(t * SEL_TILE, SEL_TILE)
        return _dot(vat_ref[0, 0, :, pl.ds(k0, SEL_TILE)], pt)

    def stage(t, cur, causal, prefetch):
        prv = 1 - cur
        pv = weighted(p_scr[prv], jnp.maximum(t - 1, 0))
        s = s_scr[cur]
        if causal:
            kpos = t * SEL_TILE + lax.broadcasted_iota(jnp.int32, (SEL_TILE, cols), 0)
            s = jnp.where(kpos <= qpos, s, NEG_INF)
        m_prev = m_scr[...]
        m_new = jnp.maximum(m_prev, jnp.max(s, axis=0, keepdims=True))
        p_scr[cur] = jnp.exp2(s - m_new).astype(BF16)
        m_scr[...] = m_new
        acc_scr[...] = (acc_scr[...] + pv) * jnp.exp2(m_prev - m_new)
        if prefetch:
            s_scr[prv] = scores(t + 1)

    def trip(i, carry):
        stage(2 * i, 0, False, True)
        stage(2 * i + 1, 1, False, True)
        return carry

    n_pairs = s0 // (2 * SEL_TILE)
    s_scr[0] = scores(0)
    p_scr[1] = jnp.zeros((SEL_TILE, cols), BF16)
    m_scr[...] = jnp.full((1, cols), NEG_INF, F32)
    acc_scr[...] = jnp.zeros((V_ROWS, cols), F32)
    lax.fori_loop(0, n_pairs, trip, 0)
    stage(2 * n_pairs, 0, True, True)
    stage(2 * n_pairs + 1, 1, True, False)
    acc_s = acc_scr[...] + weighted(p_scr[1], 2 * n_pairs + 1)
    o_s = acc_s[:HEAD_DIM] / acc_s[HEAD_DIM:HEAD_DIM + 1]

    gate_t = (1.0 / (1.0 + jnp.exp(-(zg_ref[...] + bg_ref[...])))).T
    outs = []
    for r in range(Q_PER_KV):
        cs = slice(r * Q_BLOCK, (r + 1) * Q_BLOCK)
        outs.append(gate_t[r:r + 1] * o_c[:, cs] + gate_t[Q_PER_KV + r:Q_PER_KV + r + 1] * o_s[:, cs]
                    + gate_t[2 * Q_PER_KV + r:2 * Q_PER_KV + r + 1] * o_w[:, cs])
    out_ref[...] = jnp.concatenate(outs, axis=0).T.astype(BF16)


def _kv_layouts(rows, b, t, left_pad):
    k = rows[:, :KV_COLS].reshape(b, t, N_KV, HEAD_DIM).astype(BF16)
    v = rows[:, KV_COLS:].reshape(b, t, N_KV, HEAD_DIM).astype(BF16)
    kk = jnp.transpose(k, (0, 2, 1, 3))
    vt = jnp.concatenate([jnp.transpose(v, (0, 2, 3, 1)), jnp.ones((b, N_KV, 1, t), BF16),
                          jnp.zeros((b, N_KV, V_ROWS - HEAD_DIM - 1, t), BF16)], axis=2)
    if left_pad:
        kk = jnp.pad(kk, ((0, 0), (0, 0), (left_pad, 0), (0, LANES - HEAD_DIM)))
        flag = (np.arange(left_pad + t)[:, None] < left_pad) & (np.arange(LANES)[None, :] == HEAD_DIM)
        kk = kk + jnp.asarray(flag, BF16)
        vt = jnp.pad(vt, ((0, 0), (0, 0), (0, 0), (left_pad, 0)))
    return kk, vt


def _nsa_prompt(z, bg_pad, kvc, sel_rows, win_rows, b, t):
    nq = t // Q_BLOCK
    n_c = kvc.shape[2]
    n_blk = t // SEL_LEN
    assert t % (2 * SEL_TILE) == 0 and N_SEL <= n_blk <= LANES
    cols = Q_PER_KV * Q_BLOCK
    kc = jnp.transpose(kvc[:, 0].reshape(b, n_c, N_KV, HEAD_DIM).astype(BF16), (0, 2, 1, 3))
    vc = kvc[:, 1].reshape(b, n_c, N_KV, HEAD_DIM).astype(BF16)
    vct = jnp.concatenate([jnp.transpose(vc, (0, 2, 3, 1)), jnp.ones((b, N_KV, 1, n_c), BF16),
                           jnp.zeros((b, N_KV, V_ROWS - HEAD_DIM - 1, n_c), BF16)], axis=2)
    smapt = _selection_map(n_c, n_blk, LANES).T
    ks, vat = _kv_layouts(sel_rows, b, t, 0)
    key_in_blk = np.arange(t)[:, None] // SEL_LEN == np.arange(LANES)[None, :]
    ka = jnp.concatenate([jnp.broadcast_to(jnp.asarray(key_in_blk, BF16), (b, N_KV, t, LANES)), ks,
                          jnp.zeros((b, N_KV, t, KA_COLS - LANES - HEAD_DIM), BF16)], axis=3)
    kw, vwt = _kv_layouts(win_rows, b, t, WINDOW)
    per_bg = lambda shape: pl.BlockSpec((1, 1) + shape, lambda bb, g, n: (bb, g, 0, 0))
    return pl.pallas_call(
        _nsa_prompt_kernel,
        grid=(b, N_KV, nq),
        in_specs=[pl.BlockSpec((Q_BLOCK, Q_PER_KV * HEAD_DIM), lambda bb, g, n: (bb * nq + n, g)),
                  pl.BlockSpec((Q_BLOCK, LANES), lambda bb, g, n: (bb * nq + n, D_MODEL // LANES + g)),
                  pl.BlockSpec((1, LANES), lambda bb, g, n: (0, g)),
                  per_bg((n_c, HEAD_DIM)), per_bg((V_ROWS, n_c)),
                  pl.BlockSpec((LANES, n_c), lambda bb, g, n: (0, 0)),
                  per_bg((t, KA_COLS)), per_bg((V_ROWS, t)),
                  per_bg((WINDOW + t, LANES)), per_bg((V_ROWS, WINDOW + t))],
        out_specs=pl.BlockSpec((Q_BLOCK, Q_PER_KV * HEAD_DIM), lambda bb, g, n: (bb * nq + n, g)),
        out_shape=jax.ShapeDtypeStruct((b * t, D_MODEL), BF16),
        scratch_shapes=[pltpu.VMEM((2, SEL_TILE, cols), F32), pltpu.VMEM((2, SEL_TILE, cols), BF16),
                        pltpu.VMEM((1, cols), F32), pltpu.VMEM((V_ROWS, cols), F32)],
        compiler_params=_params(("parallel", "parallel", "arbitrary")),
        name="nsa_prompt",
    )(z, z, bg_pad, kc, vct, smapt, ka, vat, kw, vwt)


SROWS = 32
SBLK = 256


def _samp_cmpwin_kernel(qbd_ref, kvc_ref, smap_ref, win_ref, wnew_ref, oc_ref, ow_ref, imp_ref, *, qpos):
    qf = qbd_ref[0] * SCALE
    qb = qf.astype(BF16)
    kc = kvc_ref[0, 0].astype(BF16)
    vc = kvc_ref[0, 1].astype(BF16)
    n_c = kc.shape[0]
    sc = _dot_nt(qb, kc)
    cidx = lax.broadcasted_iota(jnp.int32, (SROWS, n_c), 1)
    mc = (cidx >= 1) & (cidx * CMP_STRIDE + (CMP_STRIDE - 1) <= qpos)
    scm = jnp.where(mc, sc, NEG_INF)
    p = jnp.where(mc, jnp.exp(scm - jnp.max(scm, axis=1, keepdims=True)), 0.0)
    l = jnp.sum(p, axis=1, keepdims=True)
    inv = jnp.where(l > 0.0, 1.0 / l, 0.0)
    oc_ref[0] = _dot(p.astype(BF16), vc) * inv
    pn = p * inv
    psum = pn[0:8] + pn[8:16] + pn[16:24] + pn[24:32]
    p_hi = psum.astype(BF16)
    p_lo = (psum - p_hi.astype(F32)).astype(BF16)
    imp_ref[0] = _dot(p_hi, smap_ref[...]) + _dot(p_lo, smap_ref[...])
    win = win_ref[0]
    n_w = win.shape[1]
    sw = _dot(qb, win[:KV_COLS].astype(BF16))
    widx = lax.broadcasted_iota(jnp.int32, (SROWS, n_w), 1)
    dist = n_w - widx
    mw = (dist < WINDOW) & (qpos - dist >= 0)
    swm = jnp.where(mw, sw, NEG_INF)
    wnew = wnew_ref[0]
    s_new = jnp.sum(qb.astype(F32) * wnew[:, :KV_COLS].astype(BF16).astype(F32), axis=1, keepdims=True)
    m = jnp.maximum(jnp.max(swm, axis=1, keepdims=True), s_new)
    pw = jnp.exp(swm - m)
    p_new = jnp.exp(s_new - m)
    lw = jnp.sum(pw, axis=1, keepdims=True) + p_new
    ow = _dot_nt(pw.astype(BF16), win[KV_COLS:].astype(BF16)) + p_new * wnew[:, KV_COLS:]
    ow_ref[0] = ow / lw


def _samp_topk_kernel(imp_ref, bias_ref, *, cur):
    imp = imp_ref[...]
    jidx = lax.broadcasted_iota(jnp.int32, imp.shape, 1)
    forced = (jidx == 0) | (jidx == cur) | (jidx == cur - 1)
    score = jnp.where(forced, FORCE_SCORE, jnp.where(jidx <= cur, imp, -1.0))
    work = score
    sel = jnp.zeros(imp.shape, F32)
    for _ in range(N_SEL):
        mx = jnp.max(work, axis=1, keepdims=True)
        first = jnp.min(jnp.where(work == mx, jidx, SBLK), axis=1, keepdims=True)
        hit = jidx == first
        sel = jnp.where(hit, 1.0, sel)
        work = jnp.where(hit, -3e38, work)
    bias_ref[...] = jnp.where((sel > 0.0) & (score >= 0.0), 0.0, NEG_INF)


def _samp_sel_kernel(pt_ref, *refs, npg):
    page_refs = refs[:npg]
    (qbd_ref, bias_ref, onehot_ref, knew_ref, oc_ref, ow_ref, zg_ref, bg_ref, out_ref) = refs[npg:]
    n_past_blk = npg * PAGE_SIZE // SEL_LEN
    qb = (qbd_ref[0] * SCALE).astype(BF16)
    bias8 = bias_ref[0][:, :n_past_blk].astype(BF16)
    bias32 = jnp.concatenate([bias8] * Q_PER_KV, axis=0)
    s = jnp.concatenate([_dot(qb, page_refs[t][0, :KV_COLS, :].astype(BF16)) for t in range(npg)], axis=1)
    s = s + _dot(bias32, onehot_ref[...])
    knew = knew_ref[0]
    s_new = jnp.sum(qb.astype(F32) * knew[:, :KV_COLS].astype(BF16).astype(F32), axis=1, keepdims=True)
    m = jnp.maximum(jnp.max(s, axis=1, keepdims=True), s_new)
    p = jnp.exp(s - m)
    p_new = jnp.exp(s_new - m)
    pv = p_new * knew[:, KV_COLS:]
    for t in range(npg):
        pv = pv + _dot_nt(p[:, t * PAGE_SIZE:(t + 1) * PAGE_SIZE].astype(BF16),
                          page_refs[t][0, KV_COLS:, :].astype(BF16))
    os_ = pv / (jnp.sum(p, axis=1, keepdims=True) + p_new)
    gate = 1.0 / (1.0 + jnp.exp(-(zg_ref[0] + bg_ref[0])))
    o = gate[:, 0:1] * oc_ref[0] + gate[:, 1:2] * os_ + gate[:, 2:3] * ow_ref[0]
    gsel = lax.broadcasted_iota(jnp.int32, (SROWS, HEAD_DIM), 0) & 7
    res = jnp.zeros((SROWS, HEAD_DIM), F32)
    for g in range(N_KV):
        res = res + jnp.where(gsel == g, o[:, g * HEAD_DIM:(g + 1) * HEAD_DIM], 0.0)
    out_ref[0] = res


def _nsa_sample(z, b_g, kvc, cache_sel, page_table, sel_new, win_state, win_new, past_len):
    n_seq, n_pages = page_table.shape
    qpos = past_len
    n_blk = -(-(past_len + 1) // SEL_LEN)
    assert n_blk <= SBLK
    n_c = kvc.shape[2]
    n_w = win_state.shape[2]
    q4 = z[:, :D_MODEL].reshape(n_seq, N_KV, Q_PER_KV, HEAD_DIM)
    eye = jnp.eye(N_KV, dtype=F32)
    qbd = jnp.transpose(q4[:, :, :, None, :] * eye[None, :, None, :, None], (0, 2, 1, 3, 4))
    qbd = jnp.pad(qbd, ((0, 0), (0, 0), (0, 8 - N_KV), (0, 0), (0, 0))).reshape(n_seq, SROWS, KV_COLS)
    def gate_rows(a):
        n = a.shape[0]
        a = jnp.transpose(a.reshape(n, N_KV, Q_PER_KV, 3), (0, 2, 1, 3))
        return jnp.pad(a, ((0, 0), (0, 0), (0, 8 - N_KV), (0, LANES - 3))).reshape(n, SROWS, LANES)

    zg = gate_rows(z[:, D_MODEL:D_MODEL + 3 * N_HEADS])
    bg = gate_rows(b_g[None, :])
    smap = _selection_map(n_c, n_blk, SBLK)
    per_seq = lambda shape: pl.BlockSpec((1,) + shape, lambda s: (s,) + (0,) * len(shape))
    o_c, o_w, imp = pl.pallas_call(
        functools.partial(_samp_cmpwin_kernel, qpos=qpos),
        grid=(n_seq,),
        in_specs=[per_seq((SROWS, KV_COLS)), per_seq((2, n_c, KV_COLS)),
                  pl.BlockSpec((n_c, SBLK), lambda s: (0, 0)), per_seq((2 * KV_COLS, n_w)),
                  per_seq((1, 2 * KV_COLS))],
        out_specs=(per_seq((SROWS, KV_COLS)), per_seq((SROWS, KV_COLS)), per_seq((8, SBLK))),
        out_shape=(jax.ShapeDtypeStruct((n_seq, SROWS, KV_COLS), F32),
                   jax.ShapeDtypeStruct((n_seq, SROWS, KV_COLS), F32),
                   jax.ShapeDtypeStruct((n_seq, 8, SBLK), F32)),
        compiler_params=_params(("parallel",)),
        name="nsa_sample_cmpwin",
    )(qbd, kvc, smap, win_state, win_new[:, None, :])
    bias = pl.pallas_call(
        functools.partial(_samp_topk_kernel, cur=qpos // SEL_LEN),
        out_shape=jax.ShapeDtypeStruct((n_seq * 8, SBLK), F32),
        name="nsa_sample_topk",
    )(imp.reshape(n_seq * 8, SBLK)).reshape(n_seq, 8, SBLK)

    n_past_blk = past_len // SEL_LEN
    keys = n_pages * PAGE_SIZE
    assert n_past_blk == n_blk - 1
    onehot = jnp.asarray(np.arange(keys)[None, :] // SEL_LEN == np.arange(n_past_blk)[:, None], BF16)

    def page_spec(t):
        return pl.BlockSpec((1, 2 * KV_COLS, PAGE_SIZE), lambda s, pt: (pt[s * n_pages + t], 0, 0))

    seq_blk = lambda shape: pl.BlockSpec((1,) + shape, lambda s, pt: (s,) + (0,) * len(shape))
    out = pl.pallas_call(
        functools.partial(_samp_sel_kernel, npg=n_pages),
        grid_spec=pltpu.PrefetchScalarGridSpec(
            num_scalar_prefetch=1,
            grid=(n_seq,),
            in_specs=[page_spec(t) for t in range(n_pages)] + [
                seq_blk((SROWS, KV_COLS)), seq_blk((8, SBLK)),
                pl.BlockSpec((n_past_blk, keys), lambda s, pt: (0, 0)), seq_blk((1, 2 * KV_COLS)),
                seq_blk((SROWS, KV_COLS)), seq_blk((SROWS, KV_COLS)), seq_blk((SROWS, LANES)),
                pl.BlockSpec((1, SROWS, LANES), lambda s, pt: (0, 0, 0))],
            out_specs=seq_blk((SROWS, HEAD_DIM))),
        out_shape=jax.ShapeDtypeStruct((n_seq, SROWS, HEAD_DIM), F32),
        compiler_params=_params(("parallel",)),
        name="nsa_sample_sel",
    )(page_table.reshape(-1), *([cache_sel] * n_pages), qbd, bias, onehot, sel_new[:, None, :], o_c, o_w, zg, bg)
    out = out.reshape(n_seq, Q_PER_KV, 8, HEAD_DIM)[:, :, :N_KV]
    return jnp.transpose(out, (0, 2, 1, 3)).reshape(n_seq, D_MODEL).astype(BF16)


def kernel(x_prompt, x_sample, cache_cmp_kv, cache_sel_kv, state_win_kv, page_table, a_norm_g, a_w_uv, a_v_norm_g, a_w_s, a_b_s, a_w_o, kv_norm_g, w_kv, cmp_pe_k, cmp_w1_k, cmp_w2_k, cmp_pe_v, cmp_w1_v, cmp_w2_v, b_norm_g, b_w_qg, b_b_g, b_w_o, mlp_norm_g, mlp_w_up, mlp_w_down, final_norm_g):
    b, t, _ = x_prompt.shape
    n_seq, dec_seq, _ = x_sample.shape
    n_pool, page, _, _, _ = cache_cmp_kv.shape
    n_pages = page_table.shape[1]
    past_len = n_pages * page
    win_buf = state_win_kv.shape[1]
    depth = mlp_norm_g.shape[0]
    n_a = a_norm_g.shape[0]
    assert dec_seq == 1 and page == PAGE_SIZE and t % PAGE_SIZE == 0
    cmp_w = (cmp_pe_k, cmp_w1_k, cmp_w2_k, cmp_pe_v, cmp_w1_v, cmp_w2_v)
    kv_shape = (2, N_KV, HEAD_DIM)
    gate_w = 3 * N_HEADS

    hp = x_prompt.reshape(b * t, D_MODEL)
    hs = x_sample.reshape(n_seq, D_MODEL)
    a_v_rows = []
    attn_p = attn_s = None
    for layer in range(depth):
        last = layer == depth - 1
        fin = final_norm_g if last else None
        if layer < n_a:
            i = layer
            hp = _gmlp_layer(hp, a_norm_g[i], a_w_uv[i], a_v_norm_g[i], a_w_s[i], a_b_s[i], a_w_o[i], chunked=True)
            hs, v_s = _gmlp_layer(hs, a_norm_g[i], a_w_uv[i], a_v_norm_g[i], a_w_s[i], a_b_s[i], a_w_o[i],
                                  chunked=False)
            a_v_rows.append(v_s.reshape(n_seq, 1, D_MODEL))
            hp = _mlp_layer(hp, mlp_norm_g[layer], mlp_w_up[layer], mlp_w_down[layer], final_g=fin)
            hs = _mlp_layer(hs, mlp_norm_g[layer], mlp_w_up[layer], mlp_w_down[layer], final_g=fin)
            continue
        if layer == n_a:
            cmp_p, sel_p, win_p = _norm_matmul(hp, kv_norm_g, w_kv, 3)
            cmp_s, sel_s, win_s = _norm_matmul(hs, kv_norm_g, w_kv, 3)
            ncol = 2 * KV_COLS // LANES
            kvc_p = _compress(cmp_p.reshape(b * t // PAGE_SIZE, ncol * PAGE_SIZE, LANES),
                              jnp.arange(b * t // PAGE_SIZE, dtype=jnp.int32).reshape(b, t // PAGE_SIZE), *cmp_w,
                              channel_major=False)

            def channel_major(a):
                return jnp.transpose(a, (0, 2, 3, 4, 1)).reshape(a.shape[0], 2 * KV_COLS, a.shape[1])

            kvc_s = _compress(channel_major(cache_cmp_kv), page_table, *cmp_w, channel_major=True)
            cache_sel = channel_major(cache_sel_kv)
            win_state = channel_major(state_win_kv)
        j = layer - n_a
        wg = b_w_qg[j][:, D_MODEL:].reshape(D_MODEL, N_KV, Q_PER_KV, 3)
        wg = jnp.transpose(wg, (0, 1, 3, 2)).reshape(D_MODEL, N_KV, 3 * Q_PER_KV)
        wg = jnp.pad(wg, ((0, 0), (0, 0), (0, LANES - 3 * Q_PER_KV))).reshape(D_MODEL, N_KV * LANES)
        bg = jnp.transpose(b_b_g[j].reshape(N_KV, Q_PER_KV, 3), (0, 2, 1)).reshape(N_KV, 3 * Q_PER_KV)
        bg = jnp.pad(bg, ((0, 0), (0, LANES - 3 * Q_PER_KV))).reshape(1, N_KV * LANES)
        (z_p,) = _norm_matmul(hp, b_norm_g[j], jnp.concatenate([b_w_qg[j][:, :D_MODEL], wg], axis=1), 1)
        attn_p = _nsa_prompt(z_p, bg, kvc_p, sel_p, win_p, b, t)
        w_s_pad = jnp.pad(b_w_qg[j], ((0, 0), (0, LANES - gate_w % LANES)))
        (z_s,) = _norm_matmul(hs, b_norm_g[j], w_s_pad, 1)
        attn_s = _nsa_sample(z_s, b_b_g[j], kvc_s, cache_sel, page_table, sel_s, win_state, win_s, past_len)
        hp = _mlp_layer(hp, mlp_norm_g[layer], mlp_w_up[layer], mlp_w_down[layer], attn=attn_p, w_o=b_w_o[j],
                        final_g=fin)
        hs = _mlp_layer(hs, mlp_norm_g[layer], mlp_w_up[layer], mlp_w_down[layer], attn=attn_s, w_o=b_w_o[j],
                        final_g=fin)

    y_prompt = hp.reshape(b, t, D_MODEL)
    y_sample = hs.reshape(n_seq, 1, D_MODEL)
    nw = min(WINDOW, t)
    win_kv_prompt = win_p.reshape(b, t, *kv_shape)[:, t - nw:]
    win_kv_sample = jnp.concatenate([state_win_kv, win_s.reshape(n_seq, 1, *kv_shape)], axis=1)[:, dec_seq:]
    return (y_prompt, y_sample, cmp_p.reshape(b, t, *kv_shape), sel_p.reshape(b, t, *kv_shape), win_kv_prompt,
            cmp_s.reshape(n_seq, 1, *kv_shape), sel_s.reshape(n_seq, 1, *kv_shape), win_kv_sample,
            jnp.stack(a_v_rows, axis=0))
```

```python
import functools

import numpy as np
import jax
import jax.numpy as jnp
from jax import lax
from jax.experimental import pallas as pl
from jax.experimental.pallas import tpu as pltpu

F32 = jnp.float32
BF16 = jnp.bfloat16

D_MODEL = 1024
D_FF = 4 * D_MODEL
CHUNK = 128
A_GROUP_DIM = 128
A_GROUPS = D_MODEL // A_GROUP_DIM
HEAD_DIM = 64
N_HEADS = D_MODEL // HEAD_DIM
N_KV = 4
Q_PER_KV = N_HEADS // N_KV
KV_COLS = N_KV * HEAD_DIM
CMP_STRIDE = 16
CMP_LEN = 2 * CMP_STRIDE
CMP_HIDDEN = 4 * HEAD_DIM
SEL_LEN = 64
N_SEL = 16
WINDOW = 512
Q_BLOCK = 256
PAGE_SIZE = 128
EPS = 1e-6
NEG_INF = -1e30
FORCE_SCORE = 1e6
SCALE = HEAD_DIM ** -0.5
LOG2E = 1.4426950408889634
LANES = 128
V_ROWS = 80
KA_COLS = 256
SEL_TILE = 512
PAGES_PER_STEP = 8
VMEM_LIMIT = 56 * 1024 * 1024


def _params(sem):
    return pltpu.CompilerParams(dimension_semantics=sem, vmem_limit_bytes=VMEM_LIMIT)


def _rms(x, g):
    return x * lax.rsqrt(jnp.mean(x * x, axis=-1, keepdims=True) + EPS) * g


def _dot(a, b):
    return jnp.dot(a, b, preferred_element_type=F32)


def _dot_nt(a, b):
    return lax.dot_general(a, b, (((1,), (1,)), ((), ())), preferred_element_type=F32)


def _gmlp_kernel(x_ref, g_ref, wuv_ref, gv_ref, ws_ref, bias_ref, wo_ref, out_ref, *rest, chunked, tm):
    x = x_ref[...]
    xn = _rms(x, g_ref[...]).astype(BF16)
    uv = _dot(xn, wuv_ref[...])
    u = uv[:, :D_MODEL]
    v = _rms(uv[:, D_MODEL:], gv_ref[...])
    if chunked:
        gated_ref = rest[0]
        row = lax.broadcasted_iota(jnp.int32, (CHUNK, CHUNK), 0)
        col = lax.broadcasted_iota(jnp.int32, (CHUNK, CHUNK), 1)
        for g in range(A_GROUPS):
            cs = slice(g * A_GROUP_DIM, (g + 1) * A_GROUP_DIM)
            w = jnp.where(row >= col, ws_ref[g], 0.0).astype(BF16)
            for c in range(tm // CHUNK):
                rs = slice(c * CHUNK, (c + 1) * CHUNK)
                s = _dot(w, v[rs, cs].astype(BF16)) + bias_ref[:, cs]
                gated_ref[rs, cs] = (u[rs, cs] * s).astype(BF16)
        gated = gated_ref[...]
    else:
        v_ref = rest[0]
        v_ref[...] = v
        gated = (u * (v * ws_ref[...] + bias_ref[...])).astype(BF16)
    out_ref[...] = x + _dot(gated, wo_ref[...])


def _gmlp_layer(h, g, w_uv, gv, w_s, b_s, w_o, *, chunked):
    m = h.shape[0]
    full = lambda shape: pl.BlockSpec(shape, lambda i: (0,) * len(shape))
    if chunked:
        tm = 512
        ws = w_s
        bias = jnp.repeat(b_s.T, A_GROUP_DIM, axis=1)
        ws_spec = full((A_GROUPS, CHUNK, CHUNK))
        bias_spec = full((CHUNK, D_MODEL))
        out_shape = jax.ShapeDtypeStruct((m, D_MODEL), F32)
        out_specs = pl.BlockSpec((tm, D_MODEL), lambda i: (i, 0))
        scratch = [pltpu.VMEM((tm, D_MODEL), BF16)]
    else:
        tm = m
        ws = jnp.repeat(w_s[:, 0, 0], A_GROUP_DIM)[None, :]
        bias = jnp.repeat(b_s[:, 0], A_GROUP_DIM)[None, :]
        ws_spec = full((1, D_MODEL))
        bias_spec = full((1, D_MODEL))
        out_shape = (jax.ShapeDtypeStruct((m, D_MODEL), F32), jax.ShapeDtypeStruct((m, D_MODEL), F32))
        out_specs = (pl.BlockSpec((tm, D_MODEL), lambda i: (i, 0)), pl.BlockSpec((tm, D_MODEL), lambda i: (i, 0)))
        scratch = []
    assert m % tm == 0
    return pl.pallas_call(
        functools.partial(_gmlp_kernel, chunked=chunked, tm=tm),
        grid=(m // tm,),
        in_specs=[pl.BlockSpec((tm, D_MODEL), lambda i: (i, 0)), full((1, D_MODEL)),
                  full((D_MODEL, 2 * D_MODEL)), full((1, D_MODEL)), ws_spec, bias_spec,
                  full((D_MODEL, D_MODEL))],
        out_specs=out_specs, out_shape=out_shape, scratch_shapes=scratch,
        compiler_params=_params(("parallel",)),
        name="gmlp_prompt" if chunked else "gmlp_sample",
    )(h, g[None, :], w_uv.astype(BF16), gv[None, :], ws, bias, w_o.astype(BF16))


def _mlp_kernel(*refs, has_attn, has_final):
    refs = list(refs)
    x_ref = refs.pop(0)
    attn_ref = refs.pop(0) if has_attn else None
    wo_ref = refs.pop(0) if has_attn else None
    g_ref, wup_ref, wdn_ref = refs.pop(0), refs.pop(0), refs.pop(0)
    fg_ref = refs.pop(0) if has_final else None
    out_ref, h_ref, xn_ref, acc_ref = refs
    j = pl.program_id(1)

    @pl.when(j == 0)
    def _():
        h = x_ref[...]
        if has_attn:
            h = h + _dot(attn_ref[...], wo_ref[...])
        h_ref[...] = h
        xn_ref[...] = _rms(h, g_ref[...]).astype(BF16)
        acc_ref[...] = jnp.zeros_like(acc_ref)

    a = _dot(xn_ref[...], wup_ref[...])
    a = jnp.square(jnp.maximum(a, 0.0)).astype(BF16)
    acc_ref[...] += _dot(a, wdn_ref[...])

    @pl.when(j == pl.num_programs(1) - 1)
    def _():
        h = h_ref[...] + acc_ref[...]
        if has_final:
            h = _rms(h, fg_ref[...])
        out_ref[...] = h


def _mlp_layer(h, g, w_up, w_down, *, attn=None, w_o=None, final_g=None):
    m = h.shape[0]
    tm = min(m, 1024)
    tf = 1024
    assert m % tm == 0 and D_FF % tf == 0
    has_attn, has_final = attn is not None, final_g is not None
    row = pl.BlockSpec((tm, D_MODEL), lambda i, j: (i, 0))
    vec = pl.BlockSpec((1, D_MODEL), lambda i, j: (0, 0))
    args, specs = [h], [row]
    if has_attn:
        args += [attn, w_o.astype(BF16)]
        specs += [row, pl.BlockSpec((D_MODEL, D_MODEL), lambda i, j: (0, 0))]
    args += [g[None, :], w_up.astype(BF16), w_down.astype(BF16)]
    specs += [vec, pl.BlockSpec((D_MODEL, tf), lambda i, j: (0, j)), pl.BlockSpec((tf, D_MODEL), lambda i, j: (j, 0))]
    if has_final:
        args.append(final_g[None, :])
        specs.append(vec)
    return pl.pallas_call(
        functools.partial(_mlp_kernel, has_attn=has_attn, has_final=has_final),
        grid=(m // tm, D_FF // tf),
        in_specs=specs, out_specs=row,
        out_shape=jax.ShapeDtypeStruct((m, D_MODEL), F32),
        scratch_shapes=[pltpu.VMEM((tm, D_MODEL), F32), pltpu.VMEM((tm, D_MODEL), BF16),
                        pltpu.VMEM((tm, D_MODEL), F32)],
        compiler_params=_params(("parallel", "arbitrary")),
        name="mlp",
    )(*args)


def _norm_matmul_kernel(x_ref, g_ref, w_ref, *out_refs):
    xn = _rms(x_ref[...], g_ref[...]).astype(BF16)
    y = _dot(xn, w_ref[...])
    wd = y.shape[1] // len(out_refs)
    for k, o_ref in enumerate(out_refs):
        o_ref[...] = y[:, k * wd:(k + 1) * wd]


def _norm_matmul(h, g, w, n_out):
    m, n = h.shape[0], w.shape[1]
    tm = min(m, 512)
    wd = n // n_out
    assert m % tm == 0 and n % n_out == 0 and wd % LANES == 0
    outs = pl.pallas_call(
        _norm_matmul_kernel,
        grid=(m // tm,),
        in_specs=[pl.BlockSpec((tm, D_MODEL), lambda i: (i, 0)), pl.BlockSpec((1, D_MODEL), lambda i: (0, 0)),
                  pl.BlockSpec((D_MODEL, n), lambda i: (0, 0))],
        out_specs=tuple(pl.BlockSpec((tm, wd), lambda i: (i, 0)) for _ in range(n_out)),
        out_shape=tuple(jax.ShapeDtypeStruct((m, wd), F32) for _ in range(n_out)),
        compiler_params=_params(("parallel",)),
        name="norm_matmul",
    )(h, g[None, :], w.astype(BF16))
    return outs


def _compress_kernel(pt_ref, *refs, npg, channel_major):
    page_refs = refs[:npg]
    w1_ref, pe_ref, w2_ref, out_ref, z_ref, aprev_ref = refs[npg:npg + 6]
    j = pl.program_id(1)
    grp = PAGE_SIZE // CMP_STRIDE
    sec = grp * npg
    rows = N_KV * sec

    @pl.when(j == 0)
    def _():
        aprev_ref[...] = jnp.zeros_like(aprev_ref)

    ncol = 2 * KV_COLS // LANES
    if channel_major:
        tr_ref = refs[npg + 6]
        for t in range(npg):
            for c in range(ncol):
                tr_ref[t * ncol + c] = page_refs[t][0, c * LANES:(c + 1) * LANES, :].T

        def rows_of(t, c, p):
            return tr_ref[t * ncol + c, pl.ds(p, grp, stride=CMP_STRIDE), :]
    else:
        def rows_of(t, c, p):
            return page_refs[t][0, pl.ds(ncol * p + c, grp, stride=ncol * CMP_STRIDE), :]

    for t in range(npg):
        for q in range(CMP_STRIDE // 2):
            for c in range(ncol):
                xa = rows_of(t, c, 2 * q)
                xb = rows_of(t, c, 2 * q + 1)
                kv = c // (ncol // 2)
                for half in range(LANES // HEAD_DIM):
                    g = (c % (ncol // 2)) * (LANES // HEAD_DIM) + half
                    hs = slice(half * HEAD_DIM, (half + 1) * HEAD_DIM)
                    piece = jnp.concatenate([xa[:, hs], xb[:, hs]], axis=1)
                    z_ref[kv, g * sec + t * grp:g * sec + (t + 1) * grp, q * LANES:(q + 1) * LANES] = piece

    rid = lax.broadcasted_iota(jnp.int32, (rows, CMP_HIDDEN), 0) % sec
    for kv in range(2):
        z = z_ref[kv]
        za = (z + pe_ref[kv, 0:1, :]).astype(BF16)
        zb = (z + pe_ref[kv, 1:2, :]).astype(BF16)
        a = _dot(za, w1_ref[kv, :, :CMP_HIDDEN])
        b = _dot(zb, w1_ref[kv, :, CMP_HIDDEN:])
        a_prev = jnp.where(rid == 0, pltpu.roll(aprev_ref[kv], rows - (sec - 1), 0), pltpu.roll(a, 1, 0))
        aprev_ref[kv] = a
        hid = jax.nn.gelu(a_prev + b).astype(BF16)
        o = _dot(hid, w2_ref[kv])
        out_ref[0, kv] = jnp.concatenate([o[g * sec:(g + 1) * sec] for g in range(N_KV)], axis=1)


def _compress(pages, page_table, pe_k, w1_k, w2_k, pe_v, w1_v, w2_v, *, channel_major):
    n_seq, n_pages = page_table.shape
    npg = PAGES_PER_STEP
    assert n_pages % npg == 0
    grp = PAGE_SIZE // CMP_STRIDE
    half = CMP_STRIDE * HEAD_DIM

    def split(w1):
        return jnp.concatenate([w1[:half], w1[half:]], axis=1)

    w1 = jnp.stack([split(w1_k), split(w1_v)]).astype(BF16)
    pe = jnp.stack([pe_k.reshape(2, half), pe_v.reshape(2, half)])
    w2 = jnp.stack([w2_k, w2_v]).astype(BF16)

    ncol = 2 * KV_COLS // LANES
    scratch = [pltpu.VMEM((2, N_KV * grp * npg, half), F32), pltpu.VMEM((2, N_KV * grp * npg, CMP_HIDDEN), F32)]
    if channel_major:
        scratch.append(pltpu.VMEM((npg * ncol, PAGE_SIZE, LANES), F32))

    def page_spec(t):
        return pl.BlockSpec((1, ncol * PAGE_SIZE, LANES), lambda s, j, pt: (pt[s * n_pages + j * npg + t], 0, 0))

    const = lambda shape: pl.BlockSpec(shape, lambda s, j, pt: (0,) * len(shape))
    return pl.pallas_call(
        functools.partial(_compress_kernel, npg=npg, channel_major=channel_major),
        grid_spec=pltpu.PrefetchScalarGridSpec(
            num_scalar_prefetch=1,
            grid=(n_seq, n_pages // npg),
            in_specs=[page_spec(t) for t in range(npg)] + [const((2, half, 2 * CMP_HIDDEN)), const((2, 2, half)),
                                                           const((2, CMP_HIDDEN, HEAD_DIM))],
            out_specs=pl.BlockSpec((1, 2, grp * npg, KV_COLS), lambda s, j, pt: (s, 0, j, 0)),
            scratch_shapes=scratch),
        out_shape=jax.ShapeDtypeStruct((n_seq, 2, n_pages * grp, KV_COLS), F32),
        compiler_params=_params(("parallel", "arbitrary")),
        name="compress",
    )(page_table.reshape(-1), *([pages] * npg), w1, pe, w2)


def _selection_map(n_rows, n_blk, n_cols):
    i = np.arange(n_rows)[:, None] - 1
    j = np.arange(n_cols)[None, :]
    lo = np.maximum(i * CMP_STRIDE, j * SEL_LEN)
    hi = np.minimum(i * CMP_STRIDE + CMP_LEN, (j + 1) * SEL_LEN)
    m = np.maximum(hi - lo, 0) // CMP_STRIDE
    m = np.where((i >= 0) & (j < n_blk), m, 0)
    return jnp.asarray(m, dtype=BF16)


def _top_blocks_along_rows(score, n_rows):
    ridx = lax.broadcasted_iota(jnp.int32, score.shape, 0)
    work = score
    for _ in range(N_SEL):
        mx = jnp.max(work, axis=0, keepdims=True)
        first = jnp.min(jnp.where(work == mx, ridx, n_rows), axis=0, keepdims=True)
        work = jnp.where(ridx == first, -3e38, work)
    return work < -1e38


def _query_columns(zq):
    qt = (zq * (SCALE * LOG2E)).T
    return jnp.concatenate([qt[r * HEAD_DIM:(r + 1) * HEAD_DIM] for r in range(Q_PER_KV)], axis=1).astype(BF16)


def _nsa_prompt_kernel(zq_ref, zg_ref, bg_ref, kc_ref, vct_ref, smapt_ref, ka_ref, vat_ref, kw_ref, vwt_ref,
                       out_ref, s_scr, p_scr, m_scr, acc_scr, mx_scr):
    n = pl.program_id(2)
    s0 = n * Q_BLOCK
    cols = Q_PER_KV * Q_BLOCK
    qst = _query_columns(zq_ref[...])
    i_col = lax.broadcasted_iota(jnp.int32, (1, cols), 1) & (Q_BLOCK - 1)
    qpos = s0 + i_col

    n_c = kc_ref.shape[2]
    sc = _dot(kc_ref[0, 0], qst)
    cidx = lax.broadcasted_iota(jnp.int32, (n_c, cols), 0)
    lim = jnp.maximum((qpos - (CMP_STRIDE - 1)) // CMP_STRIDE, 0)
    scm = jnp.where((cidx >= 1) & (cidx <= lim), sc, NEG_INF)
    mx_c = jnp.where(lim >= 1, jnp.max(scm, axis=0, keepdims=True), -NEG_INF)
    p = jnp.exp2(scm - mx_c)
    acc_c = _dot(vct_ref[0, 0], p.astype(BF16))
    l_c = acc_c[HEAD_DIM:HEAD_DIM + 1]
    o_c = acc_c[:HEAD_DIM] * jnp.where(l_c > 0.0, 1.0 / l_c, 0.0)
    l32 = jnp.sum(p, axis=0, keepdims=True)
    pn = p * jnp.where(l32 > 0.0, 1.0 / l32, 0.0)
    psum = pn[:, 0:Q_BLOCK]
    for r in range(1, Q_PER_KV):
        psum = psum + pn[:, r * Q_BLOCK:(r + 1) * Q_BLOCK]
    p_hi = psum.astype(BF16)
    p_lo = (psum - p_hi.astype(F32)).astype(BF16)
    imp_t = _dot(smapt_ref[...], p_hi) + _dot(smapt_ref[...], p_lo)

    wk = WINDOW + Q_BLOCK
    w0 = pl.multiple_of(s0, Q_BLOCK)
    flag_row = lax.broadcasted_iota(jnp.int32, (LANES - HEAD_DIM, cols), 0) == 0
    qw = jnp.concatenate([qst, jnp.where(flag_row, NEG_INF, 0.0).astype(BF16)], axis=0)
    sw = _dot(kw_ref[0, 0, pl.ds(w0, wk), :], qw)
    rr = lax.broadcasted_iota(jnp.int32, (Q_BLOCK, cols), 0)
    swm = jnp.concatenate([jnp.where(rr > i_col, sw[:Q_BLOCK], NEG_INF), sw[Q_BLOCK:WINDOW],
                           jnp.where(rr <= i_col, sw[WINDOW:], NEG_INF)], axis=0)
    pw = jnp.exp2(swm - jnp.max(swm, axis=0, keepdims=True)).astype(BF16)
    acc_w = _dot(vwt_ref[0, 0, :, pl.ds(w0, wk)], pw)
    o_w = acc_w[:HEAD_DIM] / acc_w[HEAD_DIM:HEAD_DIM + 1]

    nb = imp_t.shape[0]
    jt = lax.broadcasted_iota(jnp.int32, (nb, Q_BLOCK), 0)
    cur = (s0 + lax.broadcasted_iota(jnp.int32, (nb, Q_BLOCK), 1)) // SEL_LEN
    forced = (jt == 0) | (jt == cur) | (jt == cur - 1)
    score = jnp.where(forced, FORCE_SCORE, jnp.where(jt <= cur, imp_t, -1.0))
    sel = _top_blocks_along_rows(score, nb)
    bias_t = jnp.where(sel & (score >= 0.0), 0.0, NEG_INF).astype(BF16)
    qaug_t = jnp.concatenate([jnp.concatenate([bias_t] * Q_PER_KV, axis=1), qst,
                              jnp.zeros((KA_COLS - nb - HEAD_DIM, cols), BF16)], axis=0)

    def scores(t):
        k0 = pl.multiple_of(t * SEL_TILE, SEL_TILE)
        return _dot(ka_ref[0, 0, pl.ds(k0, SEL_TILE), :], qaug_t)

    def weighted(pt, t):
        k0 = pl.multiple_of(t * SEL_TILE, SEL_TILE)
        return _dot(vat_ref[0, 0, :, pl.ds(k0, SEL_TILE)], pt)

    def put_scores(slot, t):
        s = scores(t)
        s_scr[slot] = s
        mx_scr[slot] = jnp.max(s, axis=0, keepdims=True)

    def stage(t, cur, causal, prefetch):
        prv = 1 - cur
        pv = weighted(p_scr[prv], jnp.maximum(t - 1, 0))
        s = s_scr[cur]
        if causal:
            kpos = t * SEL_TILE + lax.broadcasted_iota(jnp.int32, (SEL_TILE, cols), 0)
            s = jnp.where(kpos <= qpos, s, NEG_INF)
            mx = jnp.max(s, axis=0, keepdims=True)
        else:
            mx = mx_scr[cur]
        m_prev = m_scr[...]
        m_new = jnp.maximum(m_prev, mx)
        p_scr[cur] = jnp.exp2(s - m_new).astype(BF16)
        m_scr[...] = m_new
        acc_scr[...] = (acc_scr[...] + pv) * jnp.exp2(m_prev - m_new)
        if prefetch:
            put_scores(prv, t + 1)

    def trip(i, carry):
        stage(2 * i, 0, False, True)
        stage(2 * i + 1, 1, False, True)
        return carry

    n_pairs = s0 // (2 * SEL_TILE)
    put_scores(0, 0)
    p_scr[1] = jnp.zeros((SEL_TILE, cols), BF16)
    m_scr[...] = jnp.full((1, cols), NEG_INF, F32)
    acc_scr[...] = jnp.zeros((V_ROWS, cols), F32)
    lax.fori_loop(0, n_pairs, trip, 0)
    stage(2 * n_pairs, 0, True, True)
    stage(2 * n_pairs + 1, 1, True, False)
    acc_s = acc_scr[...] + weighted(p_scr[1], 2 * n_pairs + 1)
    o_s = acc_s[:HEAD_DIM] / acc_s[HEAD_DIM:HEAD_DIM + 1]

    gate_t = (1.0 / (1.0 + jnp.exp(-(zg_ref[...] + bg_ref[...])))).T
    outs = []
    for r in range(Q_PER_KV):
        cs = slice(r * Q_BLOCK, (r + 1) * Q_BLOCK)
        outs.append(gate_t[r:r + 1] * o_c[:, cs] + gate_t[Q_PER_KV + r:Q_PER_KV + r + 1] * o_s[:, cs]
                    + gate_t[2 * Q_PER_KV + r:2 * Q_PER_KV + r + 1] * o_w[:, cs])
    out_ref[...] = jnp.concatenate(outs, axis=0).T.astype(BF16)


def _kv_layouts(rows, b, t, left_pad):
    k = rows[:, :KV_COLS].reshape(b, t, N_KV, HEAD_DIM).astype(BF16)
    v = rows[:, KV_COLS:].reshape(b, t, N_KV, HEAD_DIM).astype(BF16)
    kk = jnp.transpose(k, (0, 2, 1, 3))
    vt = jnp.concatenate([jnp.transpose(v, (0, 2, 3, 1)), jnp.ones((b, N_KV, 1, t), BF16),
                          jnp.zeros((b, N_KV, V_ROWS - HEAD_DIM - 1, t), BF16)], axis=2)
    if left_pad:
        kk = jnp.pad(kk, ((0, 0), (0, 0), (left_pad, 0), (0, LANES - HEAD_DIM)))
        flag = (np.arange(left_pad + t)[:, None] < left_pad) & (np.arange(LANES)[None, :] == HEAD_DIM)
        kk = kk + jnp.asarray(flag, BF16)
        vt = jnp.pad(vt, ((0, 0), (0, 0), (0, 0), (left_pad, 0)))
    return kk, vt


def _nsa_prompt(z, bg_pad, kvc, sel_rows, win_rows, b, t):
    nq = t // Q_BLOCK
    n_c = kvc.shape[2]
    n_blk = t // SEL_LEN
    assert t % (2 * SEL_TILE) == 0 and N_SEL <= n_blk <= LANES
    cols = Q_PER_KV * Q_BLOCK
    kc = jnp.transpose(kvc[:, 0].reshape(b, n_c, N_KV, HEAD_DIM).astype(BF16), (0, 2, 1, 3))
    vc = kvc[:, 1].reshape(b, n_c, N_KV, HEAD_DIM).astype(BF16)
    vct = jnp.concatenate([jnp.transpose(vc, (0, 2, 3, 1)), jnp.ones((b, N_KV, 1, n_c), BF16),
                           jnp.zeros((b, N_KV, V_ROWS - HEAD_DIM - 1, n_c), BF16)], axis=2)
    smapt = _selection_map(n_c, n_blk, LANES).T
    ks, vat = _kv_layouts(sel_rows, b, t, 0)
    key_in_blk = np.arange(t)[:, None] // SEL_LEN == np.arange(LANES)[None, :]
    ka = jnp.concatenate([jnp.broadcast_to(jnp.asarray(key_in_blk, BF16), (b, N_KV, t, LANES)), ks,
                          jnp.zeros((b, N_KV, t, KA_COLS - LANES - HEAD_DIM), BF16)], axis=3)
    kw, vwt = _kv_layouts(win_rows, b, t, WINDOW)
    per_bg = lambda shape: pl.BlockSpec((1, 1) + shape, lambda bb, g, n: (bb, g, 0, 0))
    return pl.pallas_call(
        _nsa_prompt_kernel,
        grid=(b, N_KV, nq),
        in_specs=[pl.BlockSpec((Q_BLOCK, Q_PER_KV * HEAD_DIM), lambda bb, g, n: (bb * nq + n, g)),
                  pl.BlockSpec((Q_BLOCK, LANES), lambda bb, g, n: (bb * nq + n, D_MODEL // LANES + g)),
                  pl.BlockSpec((1, LANES), lambda bb, g, n: (0, g)),
                  per_bg((n_c, HEAD_DIM)), per_bg((V_ROWS, n_c)),
                  pl.BlockSpec((LANES, n_c), lambda bb, g, n: (0, 0)),
                  per_bg((t, KA_COLS)), per_bg((V_ROWS, t)),
                  per_bg((WINDOW + t, LANES)), per_bg((V_ROWS, WINDOW + t))],
        out_specs=pl.BlockSpec((Q_BLOCK, Q_PER_KV * HEAD_DIM), lambda bb, g, n: (bb * nq + n, g)),
        out_shape=jax.ShapeDtypeStruct((b * t, D_MODEL), BF16),
        scratch_shapes=[pltpu.VMEM((2, SEL_TILE, cols), F32), pltpu.VMEM((2, SEL_TILE, cols), BF16),
                        pltpu.VMEM((1, cols), F32), pltpu.VMEM((V_ROWS, cols), F32),
                        pltpu.VMEM((2, 1, cols), F32)],
        compiler_params=_params(("parallel", "parallel", "arbitrary")),
        name="nsa_prompt",
    )(z, z, bg_pad, kc, vct, smapt, ka, vat, kw, vwt)


SROWS = 32
SBLK = 256


def _samp_cmpwin_kernel(qbd_ref, kvc_ref, smap_ref, win_ref, wnew_ref, oc_ref, ow_ref, imp_ref, *, qpos):
    qf = qbd_ref[0] * SCALE
    qb = qf.astype(BF16)
    kc = kvc_ref[0, 0].astype(BF16)
    vc = kvc_ref[0, 1].astype(BF16)
    n_c = kc.shape[0]
    sc = _dot_nt(qb, kc)
    cidx = lax.broadcasted_iota(jnp.int32, (SROWS, n_c), 1)
    mc = (cidx >= 1) & (cidx * CMP_STRIDE + (CMP_STRIDE - 1) <= qpos)
    scm = jnp.where(mc, sc, NEG_INF)
    p = jnp.where(mc, jnp.exp(scm - jnp.max(scm, axis=1, keepdims=True)), 0.0)
    l = jnp.sum(p, axis=1, keepdims=True)
    inv = jnp.where(l > 0.0, 1.0 / l, 0.0)
    oc_ref[0] = _dot(p.astype(BF16), vc) * inv
    pn = p * inv
    psum = pn[0:8] + pn[8:16] + pn[16:24] + pn[24:32]
    p_hi = psum.astype(BF16)
    p_lo = (psum - p_hi.astype(F32)).astype(BF16)
    imp_ref[0] = _dot(p_hi, smap_ref[...]) + _dot(p_lo, smap_ref[...])
    win = win_ref[0]
    n_w = win.shape[1]
    sw = _dot(qb, win[:KV_COLS].astype(BF16))
    widx = lax.broadcasted_iota(jnp.int32, (SROWS, n_w), 1)
    dist = n_w - widx
    mw = (dist < WINDOW) & (qpos - dist >= 0)
    swm = jnp.where(mw, sw, NEG_INF)
    wnew = wnew_ref[0]
    s_new = jnp.sum(qb.astype(F32) * wnew[:, :KV_COLS].astype(BF16).astype(F32), axis=1, keepdims=True)
    m = jnp.maximum(jnp.max(swm, axis=1, keepdims=True), s_new)
    pw = jnp.exp(swm - m)
    p_new = jnp.exp(s_new - m)
    lw = jnp.sum(pw, axis=1, keepdims=True) + p_new
    ow = _dot_nt(pw.astype(BF16), win[KV_COLS:].astype(BF16)) + p_new * wnew[:, KV_COLS:]
    ow_ref[0] = ow / lw


def _samp_topk_kernel(imp_ref, bias_ref, *, cur):
    imp = imp_ref[...]
    jidx = lax.broadcasted_iota(jnp.int32, imp.shape, 1)
    forced = (jidx == 0) | (jidx == cur) | (jidx == cur - 1)
    score = jnp.where(forced, FORCE_SCORE, jnp.where(jidx <= cur, imp, -1.0))
    work = score
    sel = jnp.zeros(imp.shape, F32)
    for _ in range(N_SEL):
        mx = jnp.max(work, axis=1, keepdims=True)
        first = jnp.min(jnp.where(work == mx, jidx, SBLK), axis=1, keepdims=True)
        hit = jidx == first
        sel = jnp.where(hit, 1.0, sel)
        work = jnp.where(hit, -3e38, work)
    bias_ref[...] = jnp.where((sel > 0.0) & (score >= 0.0), 0.0, NEG_INF)


def _samp_sel_kernel(pt_ref, *refs, npg):
    page_refs = refs[:npg]
    (qbd_ref, bias_ref, onehot_ref, knew_ref, oc_ref, ow_ref, zg_ref, bg_ref, out_ref) = refs[npg:]
    n_past_blk = npg * PAGE_SIZE // SEL_LEN
    qb = (qbd_ref[0] * SCALE).astype(BF16)
    bias8 = bias_ref[0][:, :n_past_blk].astype(BF16)
    bias32 = jnp.concatenate([bias8] * Q_PER_KV, axis=0)
    s = jnp.concatenate([_dot(qb, page_refs[t][0, :KV_COLS, :].astype(BF16)) for t in range(npg)], axis=1)
    s = s + _dot(bias32, onehot_ref[...])
    knew = knew_ref[0]
    s_new = jnp.sum(qb.astype(F32) * knew[:, :KV_COLS].astype(BF16).astype(F32), axis=1, keepdims=True)
    m = jnp.maximum(jnp.max(s, axis=1, keepdims=True), s_new)
    p = jnp.exp(s - m)
    p_new = jnp.exp(s_new - m)
    pv = p_new * knew[:, KV_COLS:]
    for t in range(npg):
        pv = pv + _dot_nt(p[:, t * PAGE_SIZE:(t + 1) * PAGE_SIZE].astype(BF16),
                          page_refs[t][0, KV_COLS:, :].astype(BF16))
    os_ = pv / (jnp.sum(p, axis=1, keepdims=True) + p_new)
    gate = 1.0 / (1.0 + jnp.exp(-(zg_ref[0] + bg_ref[0])))
    o = gate[:, 0:1] * oc_ref[0] + gate[:, 1:2] * os_ + gate[:, 2:3] * ow_ref[0]
    gsel = lax.broadcasted_iota(jnp.int32, (SROWS, HEAD_DIM), 0) & 7
    res = jnp.zeros((SROWS, HEAD_DIM), F32)
    for g in range(N_KV):
        res = res + jnp.where(gsel == g, o[:, g * HEAD_DIM:(g + 1) * HEAD_DIM], 0.0)
    out_ref[0] = res


def _nsa_sample(z, b_g, kvc, cache_sel, page_table, sel_new, win_state, win_new, past_len):
    n_seq, n_pages = page_table.shape
    qpos = past_len
    n_blk = -(-(past_len + 1) // SEL_LEN)
    assert n_blk <= SBLK
    n_c = kvc.shape[2]
    n_w = win_state.shape[2]
    q4 = z[:, :D_MODEL].reshape(n_seq, N_KV, Q_PER_KV, HEAD_DIM)
    eye = jnp.eye(N_KV, dtype=F32)
    qbd = jnp.transpose(q4[:, :, :, None, :] * eye[None, :, None, :, None], (0, 2, 1, 3, 4))
    qbd = jnp.pad(qbd, ((0, 0), (0, 0), (0, 8 - N_KV), (0, 0), (0, 0))).reshape(n_seq, SROWS, KV_COLS)
    def gate_rows(a):
        n = a.shape[0]
        a = jnp.transpose(a.reshape(n, N_KV, Q_PER_KV, 3), (0, 2, 1, 3))
        return jnp.pad(a, ((0, 0), (0, 0), (0, 8 - N_KV), (0, LANES - 3))).reshape(n, SROWS, LANES)

    zg = gate_rows(z[:, D_MODEL:D_MODEL + 3 * N_HEADS])
    bg = gate_rows(b_g[None, :])
    smap = _selection_map(n_c, n_blk, SBLK)
    per_seq = lambda shape: pl.BlockSpec((1,) + shape, lambda s: (s,) + (0,) * len(shape))
    o_c, o_w, imp = pl.pallas_call(
        functools.partial(_samp_cmpwin_kernel, qpos=qpos),
        grid=(n_seq,),
        in_specs=[per_seq((SROWS, KV_COLS)), per_seq((2, n_c, KV_COLS)),
                  pl.BlockSpec((n_c, SBLK), lambda s: (0, 0)), per_seq((2 * KV_COLS, n_w)),
                  per_seq((1, 2 * KV_COLS))],
        out_specs=(per_seq((SROWS, KV_COLS)), per_seq((SROWS, KV_COLS)), per_seq((8, SBLK))),
        out_shape=(jax.ShapeDtypeStruct((n_seq, SROWS, KV_COLS), F32),
                   jax.ShapeDtypeStruct((n_seq, SROWS, KV_COLS), F32),
                   jax.ShapeDtypeStruct((n_seq, 8, SBLK), F32)),
        compiler_params=_params(("parallel",)),
        name="nsa_sample_cmpwin",
    )(qbd, kvc, smap, win_state, win_new[:, None, :])
    bias = pl.pallas_call(
        functools.partial(_samp_topk_kernel, cur=qpos // SEL_LEN),
        out_shape=jax.ShapeDtypeStruct((n_seq * 8, SBLK), F32),
        name="nsa_sample_topk",
    )(imp.reshape(n_seq * 8, SBLK)).reshape(n_seq, 8, SBLK)

    n_past_blk = past_len // SEL_LEN
    keys = n_pages * PAGE_SIZE
    assert n_past_blk == n_blk - 1
    onehot = jnp.asarray(np.arange(keys)[None, :] // SEL_LEN == np.arange(n_past_blk)[:, None], BF16)

    def page_spec(t):
        return pl.BlockSpec((1, 2 * KV_COLS, PAGE_SIZE), lambda s, pt: (pt[s * n_pages + t], 0, 0))

    seq_blk = lambda shape: pl.BlockSpec((1,) + shape, lambda s, pt: (s,) + (0,) * len(shape))
    out = pl.pallas_call(
        functools.partial(_samp_sel_kernel, npg=n_pages),
        grid_spec=pltpu.PrefetchScalarGridSpec(
            num_scalar_prefetch=1,
            grid=(n_seq,),
            in_specs=[page_spec(t) for t in range(n_pages)] + [
                seq_blk((SROWS, KV_COLS)), seq_blk((8, SBLK)),
                pl.BlockSpec((n_past_blk, keys), lambda s, pt: (0, 0)), seq_blk((1, 2 * KV_COLS)),
                seq_blk((SROWS, KV_COLS)), seq_blk((SROWS, KV_COLS)), seq_blk((SROWS, LANES)),
                pl.BlockSpec((1, SROWS, LANES), lambda s, pt: (0, 0, 0))],
            out_specs=seq_blk((SROWS, HEAD_DIM))),
        out_shape=jax.ShapeDtypeStruct((n_seq, SROWS, HEAD_DIM), F32),
        compiler_params=_params(("parallel",)),
        name="nsa_sample_sel",
    )(page_table.reshape(-1), *([cache_sel] * n_pages), qbd, bias, onehot, sel_new[:, None, :], o_c, o_w, zg, bg)
    out = out.reshape(n_seq, Q_PER_KV, 8, HEAD_DIM)[:, :, :N_KV]
    return jnp.transpose(out, (0, 2, 1, 3)).reshape(n_seq, D_MODEL).astype(BF16)


def kernel(x_prompt, x_sample, cache_cmp_kv, cache_sel_kv, state_win_kv, page_table, a_norm_g, a_w_uv, a_v_norm_g, a_w_s, a_b_s, a_w_o, kv_norm_g, w_kv, cmp_pe_k, cmp_w1_k, cmp_w2_k, cmp_pe_v, cmp_w1_v, cmp_w2_v, b_norm_g, b_w_qg, b_b_g, b_w_o, mlp_norm_g, mlp_w_up, mlp_w_down, final_norm_g):
    b, t, _ = x_prompt.shape
    n_seq, dec_seq, _ = x_sample.shape
    n_pool, page, _, _, _ = cache_cmp_kv.shape
    n_pages = page_table.shape[1]
    past_len = n_pages * page
    win_buf = state_win_kv.shape[1]
    depth = mlp_norm_g.shape[0]
    n_a = a_norm_g.shape[0]
    assert dec_seq == 1 and page == PAGE_SIZE and t % PAGE_SIZE == 0
    cmp_w = (cmp_pe_k, cmp_w1_k, cmp_w2_k, cmp_pe_v, cmp_w1_v, cmp_w2_v)
    kv_shape = (2, N_KV, HEAD_DIM)
    gate_w = 3 * N_HEADS

    hp = x_prompt.reshape(b * t, D_MODEL)
    hs = x_sample.reshape(n_seq, D_MODEL)
    a_v_rows = []
    attn_p = attn_s = None
    for layer in range(depth):
        last = layer == depth - 1
        fin = final_norm_g if last else None
        if layer < n_a:
            i = layer
            hp = _gmlp_layer(hp, a_norm_g[i], a_w_uv[i], a_v_norm_g[i], a_w_s[i], a_b_s[i], a_w_o[i], chunked=True)
            hs, v_s = _gmlp_layer(hs, a_norm_g[i], a_w_uv[i], a_v_norm_g[i], a_w_s[i], a_b_s[i], a_w_o[i],
                                  chunked=False)
            a_v_rows.append(v_s.reshape(n_seq, 1, D_MODEL))
            hp = _mlp_layer(hp, mlp_norm_g[layer], mlp_w_up[layer], mlp_w_down[layer], final_g=fin)
            hs = _mlp_layer(hs, mlp_norm_g[layer], mlp_w_up[layer], mlp_w_down[layer], final_g=fin)
            continue
        if layer == n_a:
            cmp_p, sel_p, win_p = _norm_matmul(hp, kv_norm_g, w_kv, 3)
            cmp_s, sel_s, win_s = _norm_matmul(hs, kv_norm_g, w_kv, 3)
            ncol = 2 * KV_COLS // LANES
            kvc_p = _compress(cmp_p.reshape(b * t // PAGE_SIZE, ncol * PAGE_SIZE, LANES),
                              jnp.arange(b * t // PAGE_SIZE, dtype=jnp.int32).reshape(b, t // PAGE_SIZE), *cmp_w,
                              channel_major=False)

            def channel_major(a):
                return jnp.transpose(a, (0, 2, 3, 4, 1)).reshape(a.shape[0], 2 * KV_COLS, a.shape[1])

            kvc_s = _compress(channel_major(cache_cmp_kv), page_table, *cmp_w, channel_major=True)
            cache_sel = channel_major(cache_sel_kv)
            win_state = channel_major(state_win_kv)
        j = layer - n_a
        wg = b_w_qg[j][:, D_MODEL:].reshape(D_MODEL, N_KV, Q_PER_KV, 3)
        wg = jnp.transpose(wg, (0, 1, 3, 2)).reshape(D_MODEL, N_KV, 3 * Q_PER_KV)
        wg = jnp.pad(wg, ((0, 0), (0, 0), (0, LANES - 3 * Q_PER_KV))).reshape(D_MODEL, N_KV * LANES)
        bg = jnp.transpose(b_b_g[j].reshape(N_KV, Q_PER_KV, 3), (0, 2, 1)).reshape(N_KV, 3 * Q_PER_KV)
        bg = jnp.pad(bg, ((0, 0), (0, LANES - 3 * Q_PER_KV))).reshape(1, N_KV * LANES)
        (z_p,) = _norm_matmul(hp, b_norm_g[j], jnp.concatenate([b_w_qg[j][:, :D_MODEL], wg], axis=1), 1)
        attn_p = _nsa_prompt(z_p, bg, kvc_p, sel_p, win_p, b, t)
        w_s_pad = jnp.pad(b_w_qg[j], ((0, 0), (0, LANES - gate_w % LANES)))
        (z_s,) = _norm_matmul(hs, b_norm_g[j], w_s_pad, 1)
        attn_s = _nsa_sample(z_s, b_b_g[j], kvc_s, cache_sel, page_table, sel_s, win_state, win_s, past_len)
        hp = _mlp_layer(hp, mlp_norm_g[layer], mlp_w_up[layer], mlp_w_down[layer], attn=attn_p, w_o=b_w_o[j],
                        final_g=fin)
        hs = _mlp_layer(hs, mlp_norm_g[layer], mlp_w_up[layer], mlp_w_down[layer], attn=attn_s, w_o=b_w_o[j],
                        final_g=fin)

    y_prompt = hp.reshape(b, t, D_MODEL)
    y_sample = hs.reshape(n_seq, 1, D_MODEL)
    nw = min(WINDOW, t)
    win_kv_prompt = win_p.reshape(b, t, *kv_shape)[:, t - nw:]
    win_kv_sample = jnp.concatenate([state_win_kv, win_s.reshape(n_seq, 1, *kv_shape)], axis=1)[:, dec_seq:]
    return (y_prompt, y_sample, cmp_p.reshape(b, t, *kv_shape), sel_p.reshape(b, t, *kv_shape), win_kv_prompt,
            cmp_s.reshape(n_seq, 1, *kv_shape), sel_s.reshape(n_seq, 1, *kv_shape), win_kv_sample,
            jnp.stack(a_v_rows, axis=0))
```

```python
import functools

import numpy as np
import jax
import jax.numpy as jnp
from jax import lax
from jax.experimental import pallas as pl
from jax.experimental.pallas import tpu as pltpu

F32 = jnp.float32
BF16 = jnp.bfloat16

D_MODEL = 1024
D_FF = 4 * D_MODEL
CHUNK = 128
A_GROUP_DIM = 128
A_GROUPS = D_MODEL // A_GROUP_DIM
HEAD_DIM = 64
N_HEADS = D_MODEL // HEAD_DIM
N_KV = 4
Q_PER_KV = N_HEADS // N_KV
KV_COLS = N_KV * HEAD_DIM
CMP_STRIDE = 16
CMP_LEN = 2 * CMP_STRIDE
CMP_HIDDEN = 4 * HEAD_DIM
SEL_LEN = 64
N_SEL = 16
WINDOW = 512
Q_BLOCK = 256
PAGE_SIZE = 128
EPS = 1e-6
NEG_INF = -1e30
FORCE_SCORE = 1e6
SCALE = HEAD_DIM ** -0.5
LOG2E = 1.4426950408889634
LANES = 128
V_ROWS = 80
KA_COLS = 256
SEL_TILE = 512
PAGES_PER_STEP = 16
VMEM_LIMIT = 56 * 1024 * 1024


def _params(sem):
    return pltpu.CompilerParams(dimension_semantics=sem, vmem_limit_bytes=VMEM_LIMIT)


def _rms(x, g):
    return x * lax.rsqrt(jnp.mean(x * x, axis=-1, keepdims=True) + EPS) * g


def _dot(a, b):
    return jnp.dot(a, b, preferred_element_type=F32)


def _dot_nt(a, b):
    return lax.dot_general(a, b, (((1,), (1,)), ((), ())), preferred_element_type=F32)


def _gmlp_kernel(x_ref, g_ref, wuv_ref, gv_ref, ws_ref, bias_ref, wo_ref, out_ref, *rest, chunked, tm):
    x = x_ref[...]
    xn = _rms(x, g_ref[...]).astype(BF16)
    uv = _dot(xn, wuv_ref[...])
    u = uv[:, :D_MODEL]
    v = _rms(uv[:, D_MODEL:], gv_ref[...])
    if chunked:
        gated_ref = rest[0]
        row = lax.broadcasted_iota(jnp.int32, (CHUNK, CHUNK), 0)
        col = lax.broadcasted_iota(jnp.int32, (CHUNK, CHUNK), 1)
        for g in range(A_GROUPS):
            cs = slice(g * A_GROUP_DIM, (g + 1) * A_GROUP_DIM)
            w = jnp.where(row >= col, ws_ref[g], 0.0).astype(BF16)
            for c in range(tm // CHUNK):
                rs = slice(c * CHUNK, (c + 1) * CHUNK)
                s = _dot(w, v[rs, cs].astype(BF16)) + bias_ref[:, cs]
                gated_ref[rs, cs] = (u[rs, cs] * s).astype(BF16)
        gated = gated_ref[...]
    else:
        v_ref = rest[0]
        v_ref[...] = v
        gated = (u * (v * ws_ref[...] + bias_ref[...])).astype(BF16)
    out_ref[...] = x + _dot(gated, wo_ref[...])


def _gmlp_layer(h, g, w_uv, gv, w_s, b_s, w_o, *, chunked):
    m = h.shape[0]
    full = lambda shape: pl.BlockSpec(shape, lambda i: (0,) * len(shape))
    if chunked:
        tm = 512
        ws = w_s
        bias = jnp.repeat(b_s.T, A_GROUP_DIM, axis=1)
        ws_spec = full((A_GROUPS, CHUNK, CHUNK))
        bias_spec = full((CHUNK, D_MODEL))
        out_shape = jax.ShapeDtypeStruct((m, D_MODEL), F32)
        out_specs = pl.BlockSpec((tm, D_MODEL), lambda i: (i, 0))
        scratch = [pltpu.VMEM((tm, D_MODEL), BF16)]
    else:
        tm = m
        ws = jnp.repeat(w_s[:, 0, 0], A_GROUP_DIM)[None, :]
        bias = jnp.repeat(b_s[:, 0], A_GROUP_DIM)[None, :]
        ws_spec = full((1, D_MODEL))
        bias_spec = full((1, D_MODEL))
        out_shape = (jax.ShapeDtypeStruct((m, D_MODEL), F32), jax.ShapeDtypeStruct((m, D_MODEL), F32))
        out_specs = (pl.BlockSpec((tm, D_MODEL), lambda i: (i, 0)), pl.BlockSpec((tm, D_MODEL), lambda i: (i, 0)))
        scratch = []
    assert m % tm == 0
    return pl.pallas_call(
        functools.partial(_gmlp_kernel, chunked=chunked, tm=tm),
        grid=(m // tm,),
        in_specs=[pl.BlockSpec((tm, D_MODEL), lambda i: (i, 0)), full((1, D_MODEL)),
                  full((D_MODEL, 2 * D_MODEL)), full((1, D_MODEL)), ws_spec, bias_spec,
                  full((D_MODEL, D_MODEL))],
        out_specs=out_specs, out_shape=out_shape, scratch_shapes=scratch,
        compiler_params=_params(("parallel",)),
        name="gmlp_prompt" if chunked else "gmlp_sample",
    )(h, g[None, :], w_uv.astype(BF16), gv[None, :], ws, bias, w_o.astype(BF16))


def _mlp_kernel(*refs, has_attn, has_final):
    refs = list(refs)
    x_ref = refs.pop(0)
    attn_ref = refs.pop(0) if has_attn else None
    wo_ref = refs.pop(0) if has_attn else None
    g_ref, wup_ref, wdn_ref = refs.pop(0), refs.pop(0), refs.pop(0)
    fg_ref = refs.pop(0) if has_final else None
    out_ref, h_ref, xn_ref, acc_ref = refs
    j = pl.program_id(1)

    @pl.when(j == 0)
    def _():
        h = x_ref[...]
        if has_attn:
            h = h + _dot(attn_ref[...], wo_ref[...])
        h_ref[...] = h
        xn_ref[...] = _rms(h, g_ref[...]).astype(BF16)
        acc_ref[...] = jnp.zeros_like(acc_ref)

    a = _dot(xn_ref[...], wup_ref[...])
    a = jnp.square(jnp.maximum(a, 0.0)).astype(BF16)
    acc_ref[...] += _dot(a, wdn_ref[...])

    @pl.when(j == pl.num_programs(1) - 1)
    def _():
        h = h_ref[...] + acc_ref[...]
        if has_final:
            h = _rms(h, fg_ref[...])
        out_ref[...] = h


def _mlp_layer(h, g, w_up, w_down, *, attn=None, w_o=None, final_g=None):
    m = h.shape[0]
    tm = min(m, 1024)
    tf = 1024
    assert m % tm == 0 and D_FF % tf == 0
    has_attn, has_final = attn is not None, final_g is not None
    row = pl.BlockSpec((tm, D_MODEL), lambda i, j: (i, 0))
    vec = pl.BlockSpec((1, D_MODEL), lambda i, j: (0, 0))
    args, specs = [h], [row]
    if has_attn:
        args += [attn, w_o.astype(BF16)]
        specs += [row, pl.BlockSpec((D_MODEL, D_MODEL), lambda i, j: (0, 0))]
    args += [g[None, :], w_up.astype(BF16), w_down.astype(BF16)]
    specs += [vec, pl.BlockSpec((D_MODEL, tf), lambda i, j: (0, j)), pl.BlockSpec((tf, D_MODEL), lambda i, j: (j, 0))]
    if has_final:
        args.append(final_g[None, :])
        specs.append(vec)
    return pl.pallas_call(
        functools.partial(_mlp_kernel, has_attn=has_attn, has_final=has_final),
        grid=(m // tm, D_FF // tf),
        in_specs=specs, out_specs=row,
        out_shape=jax.ShapeDtypeStruct((m, D_MODEL), F32),
        scratch_shapes=[pltpu.VMEM((tm, D_MODEL), F32), pltpu.VMEM((tm, D_MODEL), BF16),
                        pltpu.VMEM((tm, D_MODEL), F32)],
        compiler_params=_params(("parallel", "arbitrary")),
        name="mlp",
    )(*args)


def _norm_matmul_kernel(x_ref, g_ref, w_ref, *out_refs):
    xn = _rms(x_ref[...], g_ref[...]).astype(BF16)
    y = _dot(xn, w_ref[...])
    wd = y.shape[1] // len(out_refs)
    for k, o_ref in enumerate(out_refs):
        o_ref[...] = y[:, k * wd:(k + 1) * wd]


def _norm_matmul(h, g, w, n_out):
    m, n = h.shape[0], w.shape[1]
    tm = min(m, 1024)
    wd = n // n_out
    assert m % tm == 0 and n % n_out == 0 and wd % LANES == 0
    outs = pl.pallas_call(
        _norm_matmul_kernel,
        grid=(m // tm,),
        in_specs=[pl.BlockSpec((tm, D_MODEL), lambda i: (i, 0)), pl.BlockSpec((1, D_MODEL), lambda i: (0, 0)),
                  pl.BlockSpec((D_MODEL, n), lambda i: (0, 0))],
        out_specs=tuple(pl.BlockSpec((tm, wd), lambda i: (i, 0)) for _ in range(n_out)),
        out_shape=tuple(jax.ShapeDtypeStruct((m, wd), F32) for _ in range(n_out)),
        compiler_params=_params(("parallel",)),
        name="norm_matmul",
    )(h, g[None, :], w.astype(BF16))
    return outs


def _compress_kernel(pt_ref, *refs, npg, channel_major):
    page_refs = refs[:npg]
    w1_ref, pe_ref, w2_ref, out_ref, z_ref, aprev_ref = refs[npg:npg + 6]
    j = pl.program_id(1)
    grp = PAGE_SIZE // CMP_STRIDE
    sec = grp * npg
    rows = N_KV * sec

    @pl.when(j == 0)
    def _():
        aprev_ref[...] = jnp.zeros_like(aprev_ref)

    ncol = 2 * KV_COLS // LANES
    if channel_major:
        tr_ref = refs[npg + 6]
        for t in range(npg):
            for c in range(ncol):
                tr_ref[t * ncol + c] = page_refs[t][0, c * LANES:(c + 1) * LANES, :].T

        def rows_of(t, c, p):
            return tr_ref[t * ncol + c, pl.ds(p, grp, stride=CMP_STRIDE), :]
    else:
        def rows_of(t, c, p):
            return page_refs[t][0, pl.ds(ncol * p + c, grp, stride=ncol * CMP_STRIDE), :]

    for t in range(npg):
        for q in range(CMP_STRIDE // 2):
            for c in range(ncol):
                xa = rows_of(t, c, 2 * q)
                xb = rows_of(t, c, 2 * q + 1)
                kv = c // (ncol // 2)
                for half in range(LANES // HEAD_DIM):
                    g = (c % (ncol // 2)) * (LANES // HEAD_DIM) + half
                    hs = slice(half * HEAD_DIM, (half + 1) * HEAD_DIM)
                    piece = jnp.concatenate([xa[:, hs], xb[:, hs]], axis=1)
                    z_ref[kv, g * sec + t * grp:g * sec + (t + 1) * grp, q * LANES:(q + 1) * LANES] = piece

    rid = lax.broadcasted_iota(jnp.int32, (rows, CMP_HIDDEN), 0) % sec
    for kv in range(2):
        z = z_ref[kv]
        za = (z + pe_ref[kv, 0:1, :]).astype(BF16)
        zb = (z + pe_ref[kv, 1:2, :]).astype(BF16)
        a = _dot(za, w1_ref[kv, :, :CMP_HIDDEN])
        b = _dot(zb, w1_ref[kv, :, CMP_HIDDEN:])
        a_prev = jnp.where(rid == 0, pltpu.roll(aprev_ref[kv], rows - (sec - 1), 0), pltpu.roll(a, 1, 0))
        aprev_ref[kv] = a
        hid = jax.nn.gelu(a_prev + b).astype(BF16)
        o = _dot(hid, w2_ref[kv])
        out_ref[0, kv] = jnp.concatenate([o[g * sec:(g + 1) * sec] for g in range(N_KV)], axis=1)


def _compress(pages, page_table, pe_k, w1_k, w2_k, pe_v, w1_v, w2_v, *, channel_major):
    n_seq, n_pages = page_table.shape
    npg = PAGES_PER_STEP
    assert n_pages % npg == 0
    grp = PAGE_SIZE // CMP_STRIDE
    half = CMP_STRIDE * HEAD_DIM

    def split(w1):
        return jnp.concatenate([w1[:half], w1[half:]], axis=1)

    w1 = jnp.stack([split(w1_k), split(w1_v)]).astype(BF16)
    pe = jnp.stack([pe_k.reshape(2, half), pe_v.reshape(2, half)])
    w2 = jnp.stack([w2_k, w2_v]).astype(BF16)

    ncol = 2 * KV_COLS // LANES
    scratch = [pltpu.VMEM((2, N_KV * grp * npg, half), F32), pltpu.VMEM((2, N_KV * grp * npg, CMP_HIDDEN), F32)]
    if channel_major:
        scratch.append(pltpu.VMEM((npg * ncol, PAGE_SIZE, LANES), F32))

    def page_spec(t):
        return pl.BlockSpec((1, ncol * PAGE_SIZE, LANES), lambda s, j, pt: (pt[s * n_pages + j * npg + t], 0, 0))

    const = lambda shape: pl.BlockSpec(shape, lambda s, j, pt: (0,) * len(shape))
    return pl.pallas_call(
        functools.partial(_compress_kernel, npg=npg, channel_major=channel_major),
        grid_spec=pltpu.PrefetchScalarGridSpec(
            num_scalar_prefetch=1,
            grid=(n_seq, n_pages // npg),
            in_specs=[page_spec(t) for t in range(npg)] + [const((2, half, 2 * CMP_HIDDEN)), const((2, 2, half)),
                                                           const((2, CMP_HIDDEN, HEAD_DIM))],
            out_specs=pl.BlockSpec((1, 2, grp * npg, KV_COLS), lambda s, j, pt: (s, 0, j, 0)),
            scratch_shapes=scratch),
        out_shape=jax.ShapeDtypeStruct((n_seq, 2, n_pages * grp, KV_COLS), F32),
        compiler_params=_params(("parallel", "arbitrary")),
        name="compress",
    )(page_table.reshape(-1), *([pages] * npg), w1, pe, w2)


def _selection_map(n_rows, n_blk, n_cols):
    i = np.arange(n_rows)[:, None] - 1
    j = np.arange(n_cols)[None, :]
    lo = np.maximum(i * CMP_STRIDE, j * SEL_LEN)
    hi = np.minimum(i * CMP_STRIDE + CMP_LEN, (j + 1) * SEL_LEN)
    m = np.maximum(hi - lo, 0) // CMP_STRIDE
    m = np.where((i >= 0) & (j < n_blk), m, 0)
    return jnp.asarray(m, dtype=BF16)


def _top_blocks_along_rows(score, forced, n_rows):
    ridx = lax.broadcasted_iota(jnp.int32, score.shape, 0)
    work = jnp.where(forced, -3e38, score)
    for _ in range(N_SEL - 3):
        mx = jnp.max(work, axis=0, keepdims=True)
        first = jnp.min(jnp.where(work == mx, ridx, n_rows), axis=0, keepdims=True)
        work = jnp.where(ridx == first, -3e38, work)
    return work < -1e38


def _query_columns(zq):
    qt = (zq * (SCALE * LOG2E)).T
    return jnp.concatenate([qt[r * HEAD_DIM:(r + 1) * HEAD_DIM] for r in range(Q_PER_KV)], axis=1).astype(BF16)


def _nsa_prompt_kernel(zq_ref, zg_ref, bg_ref, kc_ref, vct_ref, smapt_ref, ka_ref, vat_ref, kw_ref, vwt_ref,
                       out_ref, s_scr, p_scr, m_scr, acc_scr, mx_scr):
    n = pl.program_id(2)
    s0 = n * Q_BLOCK
    cols = Q_PER_KV * Q_BLOCK
    qst = _query_columns(zq_ref[...])
    i_col = lax.broadcasted_iota(jnp.int32, (1, cols), 1) & (Q_BLOCK - 1)
    qpos = s0 + i_col

    n_c = kc_ref.shape[2]
    sc = _dot(kc_ref[0, 0], qst)
    cidx = lax.broadcasted_iota(jnp.int32, (n_c, cols), 0)
    lim = jnp.maximum((qpos - (CMP_STRIDE - 1)) // CMP_STRIDE, 0)
    scm = jnp.where((cidx >= 1) & (cidx <= lim), sc, NEG_INF)
    mx_c = jnp.where(lim >= 1, jnp.max(scm, axis=0, keepdims=True), -NEG_INF)
    p = jnp.exp2(scm - mx_c)
    acc_c = _dot(vct_ref[0, 0], p.astype(BF16))
    l_c = acc_c[HEAD_DIM:HEAD_DIM + 1]
    o_c = acc_c[:HEAD_DIM] * jnp.where(l_c > 0.0, 1.0 / l_c, 0.0)
    l32 = jnp.sum(p, axis=0, keepdims=True)
    pn = p * jnp.where(l32 > 0.0, 1.0 / l32, 0.0)
    psum = pn[:, 0:Q_BLOCK]
    for r in range(1, Q_PER_KV):
        psum = psum + pn[:, r * Q_BLOCK:(r + 1) * Q_BLOCK]
    p_hi = psum.astype(BF16)
    p_lo = (psum - p_hi.astype(F32)).astype(BF16)
    imp_t = _dot(smapt_ref[...], p_hi) + _dot(smapt_ref[...], p_lo)

    wk = WINDOW + Q_BLOCK
    w0 = pl.multiple_of(s0, Q_BLOCK)
    flag_row = lax.broadcasted_iota(jnp.int32, (LANES - HEAD_DIM, cols), 0) == 0
    qw = jnp.concatenate([qst, jnp.where(flag_row, NEG_INF, 0.0).astype(BF16)], axis=0)
    sw = _dot(kw_ref[0, 0, pl.ds(w0, wk), :], qw)
    rr = lax.broadcasted_iota(jnp.int32, (Q_BLOCK, cols), 0)
    swm = jnp.concatenate([jnp.where(rr > i_col, sw[:Q_BLOCK], NEG_INF), sw[Q_BLOCK:WINDOW],
                           jnp.where(rr <= i_col, sw[WINDOW:], NEG_INF)], axis=0)
    pw = jnp.exp2(swm - jnp.max(swm, axis=0, keepdims=True)).astype(BF16)
    acc_w = _dot(vwt_ref[0, 0, :, pl.ds(w0, wk)], pw)
    o_w = acc_w[:HEAD_DIM] / acc_w[HEAD_DIM:HEAD_DIM + 1]

    nb = imp_t.shape[0]
    jt = lax.broadcasted_iota(jnp.int32, (nb, Q_BLOCK), 0)
    cur = (s0 + lax.broadcasted_iota(jnp.int32, (nb, Q_BLOCK), 1)) // SEL_LEN
    forced = (jt == 0) | (jt == cur) | (jt == cur - 1)
    score = jnp.where(forced, FORCE_SCORE, jnp.where(jt <= cur, imp_t, -1.0))
    sel = _top_blocks_along_rows(score, forced, nb)
    bias_t = jnp.where(sel & (score >= 0.0), 0.0, NEG_INF).astype(BF16)
    qaug_t = jnp.concatenate([jnp.concatenate([bias_t] * Q_PER_KV, axis=1), qst,
                              jnp.zeros((KA_COLS - nb - HEAD_DIM, cols), BF16)], axis=0)

    def scores(t):
        k0 = pl.multiple_of(t * SEL_TILE, SEL_TILE)
        return _dot(ka_ref[0, 0, pl.ds(k0, SEL_TILE), :], qaug_t)

    def weighted(pt, t):
        k0 = pl.multiple_of(t * SEL_TILE, SEL_TILE)
        return _dot(vat_ref[0, 0, :, pl.ds(k0, SEL_TILE)], pt)

    def put_scores(slot, t):
        s = scores(t)
        s_scr[slot] = s
        mx_scr[slot] = jnp.max(s, axis=0, keepdims=True)

    def stage(t, cur, causal, prefetch):
        prv = 1 - cur
        pv = weighted(p_scr[prv], jnp.maximum(t - 1, 0))
        s = s_scr[cur]
        if causal:
            kpos = t * SEL_TILE + lax.broadcasted_iota(jnp.int32, (SEL_TILE, cols), 0)
            s = jnp.where(kpos <= qpos, s, NEG_INF)
            mx = jnp.max(s, axis=0, keepdims=True)
        else:
            mx = mx_scr[cur]
        m_prev = m_scr[...]
        m_new = jnp.maximum(m_prev, mx)
        p_scr[cur] = jnp.exp2(s - m_new).astype(BF16)
        m_scr[...] = m_new
        acc_scr[...] = (acc_scr[...] + pv) * jnp.exp2(m_prev - m_new)
        if prefetch:
            put_scores(prv, t + 1)

    def trip(i, carry):
        stage(2 * i, 0, False, True)
        stage(2 * i + 1, 1, False, True)
        return carry

    n_pairs = s0 // (2 * SEL_TILE)
    put_scores(0, 0)
    p_scr[1] = jnp.zeros((SEL_TILE, cols), BF16)
    m_scr[...] = jnp.full((1, cols), NEG_INF, F32)
    acc_scr[...] = jnp.zeros((V_ROWS, cols), F32)
    lax.fori_loop(0, n_pairs, trip, 0)
    stage(2 * n_pairs, 0, True, True)
    stage(2 * n_pairs + 1, 1, True, False)
    acc_s = acc_scr[...] + weighted(p_scr[1], 2 * n_pairs + 1)
    o_s = acc_s[:HEAD_DIM] / acc_s[HEAD_DIM:HEAD_DIM + 1]

    gate_t = (1.0 / (1.0 + jnp.exp(-(zg_ref[...] + bg_ref[...])))).T
    outs = []
    for r in range(Q_PER_KV):
        cs = slice(r * Q_BLOCK, (r + 1) * Q_BLOCK)
        outs.append(gate_t[r:r + 1] * o_c[:, cs] + gate_t[Q_PER_KV + r:Q_PER_KV + r + 1] * o_s[:, cs]
                    + gate_t[2 * Q_PER_KV + r:2 * Q_PER_KV + r + 1] * o_w[:, cs])
    out_ref[...] = jnp.concatenate(outs, axis=0).T.astype(BF16)


def _kv_layouts(rows, b, t, left_pad):
    k = rows[:, :KV_COLS].reshape(b, t, N_KV, HEAD_DIM).astype(BF16)
    v = rows[:, KV_COLS:].reshape(b, t, N_KV, HEAD_DIM).astype(BF16)
    kk = jnp.transpose(k, (0, 2, 1, 3))
    vt = jnp.concatenate([jnp.transpose(v, (0, 2, 3, 1)), jnp.ones((b, N_KV, 1, t), BF16),
                          jnp.zeros((b, N_KV, V_ROWS - HEAD_DIM - 1, t), BF16)], axis=2)
    if left_pad:
        kk = jnp.pad(kk, ((0, 0), (0, 0), (left_pad, 0), (0, LANES - HEAD_DIM)))
        flag = (np.arange(left_pad + t)[:, None] < left_pad) & (np.arange(LANES)[None, :] == HEAD_DIM)
        kk = kk + jnp.asarray(flag, BF16)
        vt = jnp.pad(vt, ((0, 0), (0, 0), (0, 0), (left_pad, 0)))
    return kk, vt


def _nsa_prompt(z, bg_pad, kvc, sel_rows, win_rows, b, t):
    nq = t // Q_BLOCK
    n_c = kvc.shape[2]
    n_blk = t // SEL_LEN
    assert t % (2 * SEL_TILE) == 0 and N_SEL <= n_blk <= LANES
    cols = Q_PER_KV * Q_BLOCK
    kc = jnp.transpose(kvc[:, 0].reshape(b, n_c, N_KV, HEAD_DIM).astype(BF16), (0, 2, 1, 3))
    vc = kvc[:, 1].reshape(b, n_c, N_KV, HEAD_DIM).astype(BF16)
    vct = jnp.concatenate([jnp.transpose(vc, (0, 2, 3, 1)), jnp.ones((b, N_KV, 1, n_c), BF16),
                           jnp.zeros((b, N_KV, V_ROWS - HEAD_DIM - 1, n_c), BF16)], axis=2)
    smapt = _selection_map(n_c, n_blk, LANES).T
    ks, vat = _kv_layouts(sel_rows, b, t, 0)
    key_in_blk = np.arange(t)[:, None] // SEL_LEN == np.arange(LANES)[None, :]
    ka = jnp.concatenate([jnp.broadcast_to(jnp.asarray(key_in_blk, BF16), (b, N_KV, t, LANES)), ks,
                          jnp.zeros((b, N_KV, t, KA_COLS - LANES - HEAD_DIM), BF16)], axis=3)
    kw, vwt = _kv_layouts(win_rows, b, t, WINDOW)
    per_bg = lambda shape: pl.BlockSpec((1, 1) + shape, lambda bb, g, n: (bb, g, 0, 0))
    return pl.pallas_call(
        _nsa_prompt_kernel,
        grid=(b, N_KV, nq),
        in_specs=[pl.BlockSpec((Q_BLOCK, Q_PER_KV * HEAD_DIM), lambda bb, g, n: (bb * nq + n, g)),
                  pl.BlockSpec((Q_BLOCK, LANES), lambda bb, g, n: (bb * nq + n, D_MODEL // LANES + g)),
                  pl.BlockSpec((1, LANES), lambda bb, g, n: (0, g)),
                  per_bg((n_c, HEAD_DIM)), per_bg((V_ROWS, n_c)),
                  pl.BlockSpec((LANES, n_c), lambda bb, g, n: (0, 0)),
                  per_bg((t, KA_COLS)), per_bg((V_ROWS, t)),
                  per_bg((WINDOW + t, LANES)), per_bg((V_ROWS, WINDOW + t))],
        out_specs=pl.BlockSpec((Q_BLOCK, Q_PER_KV * HEAD_DIM), lambda bb, g, n: (bb * nq + n, g)),
        out_shape=jax.ShapeDtypeStruct((b * t, D_MODEL), BF16),
        scratch_shapes=[pltpu.VMEM((2, SEL_TILE, cols), F32), pltpu.VMEM((2, SEL_TILE, cols), BF16),
                        pltpu.VMEM((1, cols), F32), pltpu.VMEM((V_ROWS, cols), F32),
                        pltpu.VMEM((2, 1, cols), F32)],
        compiler_params=_params(("parallel", "parallel", "arbitrary")),
        name="nsa_prompt",
    )(z, z, bg_pad, kc, vct, smapt, ka, vat, kw, vwt)


SROWS = 32
SBLK = 256


def _samp_cmpwin_kernel(qbd_ref, kvc_ref, smap_ref, win_ref, wnew_ref, oc_ref, ow_ref, imp_ref, *, qpos):
    qf = qbd_ref[0] * SCALE
    qb = qf.astype(BF16)
    kc = kvc_ref[0, 0].astype(BF16)
    vc = kvc_ref[0, 1].astype(BF16)
    n_c = kc.shape[0]
    sc = _dot_nt(qb, kc)
    cidx = lax.broadcasted_iota(jnp.int32, (SROWS, n_c), 1)
    mc = (cidx >= 1) & (cidx * CMP_STRIDE + (CMP_STRIDE - 1) <= qpos)
    scm = jnp.where(mc, sc, NEG_INF)
    p = jnp.where(mc, jnp.exp(scm - jnp.max(scm, axis=1, keepdims=True)), 0.0)
    l = jnp.sum(p, axis=1, keepdims=True)
    inv = jnp.where(l > 0.0, 1.0 / l, 0.0)
    oc_ref[0] = _dot(p.astype(BF16), vc) * inv
    pn = p * inv
    psum = pn[0:8] + pn[8:16] + pn[16:24] + pn[24:32]
    p_hi = psum.astype(BF16)
    p_lo = (psum - p_hi.astype(F32)).astype(BF16)
    imp_ref[0] = _dot(p_hi, smap_ref[...]) + _dot(p_lo, smap_ref[...])
    win = win_ref[0]
    n_w = win.shape[1]
    sw = _dot(qb, win[:KV_COLS].astype(BF16))
    widx = lax.broadcasted_iota(jnp.int32, (SROWS, n_w), 1)
    dist = n_w - widx
    mw = (dist < WINDOW) & (qpos - dist >= 0)
    swm = jnp.where(mw, sw, NEG_INF)
    wnew = wnew_ref[0]
    s_new = jnp.sum(qb.astype(F32) * wnew[:, :KV_COLS].astype(BF16).astype(F32), axis=1, keepdims=True)
    m = jnp.maximum(jnp.max(swm, axis=1, keepdims=True), s_new)
    pw = jnp.exp(swm - m)
    p_new = jnp.exp(s_new - m)
    lw = jnp.sum(pw, axis=1, keepdims=True) + p_new
    ow = _dot_nt(pw.astype(BF16), win[KV_COLS:].astype(BF16)) + p_new * wnew[:, KV_COLS:]
    ow_ref[0] = ow / lw


def _samp_topk_kernel(imp_ref, bias_ref, *, cur):
    imp = imp_ref[...]
    jidx = lax.broadcasted_iota(jnp.int32, imp.shape, 1)
    forced = (jidx == 0) | (jidx == cur) | (jidx == cur - 1)
    score = jnp.where(forced, FORCE_SCORE, jnp.where(jidx <= cur, imp, -1.0))
    work = score
    sel = jnp.zeros(imp.shape, F32)
    for _ in range(N_SEL):
        mx = jnp.max(work, axis=1, keepdims=True)
        first = jnp.min(jnp.where(work == mx, jidx, SBLK), axis=1, keepdims=True)
        hit = jidx == first
        sel = jnp.where(hit, 1.0, sel)
        work = jnp.where(hit, -3e38, work)
    bias_ref[...] = jnp.where((sel > 0.0) & (score >= 0.0), 0.0, NEG_INF)


def _samp_sel_kernel(pt_ref, *refs, npg):
    page_refs = refs[:npg]
    (qbd_ref, bias_ref, onehot_ref, knew_ref, oc_ref, ow_ref, zg_ref, bg_ref, out_ref) = refs[npg:]
    n_past_blk = npg * PAGE_SIZE // SEL_LEN
    qb = (qbd_ref[0] * SCALE).astype(BF16)
    bias8 = bias_ref[0][:, :n_past_blk].astype(BF16)
    bias32 = jnp.concatenate([bias8] * Q_PER_KV, axis=0)
    s = jnp.concatenate([_dot(qb, page_refs[t][0, :KV_COLS, :].astype(BF16)) for t in range(npg)], axis=1)
    s = s + _dot(bias32, onehot_ref[...])
    knew = knew_ref[0]
    s_new = jnp.sum(qb.astype(F32) * knew[:, :KV_COLS].astype(BF16).astype(F32), axis=1, keepdims=True)
    m = jnp.maximum(jnp.max(s, axis=1, keepdims=True), s_new)
    p = jnp.exp(s - m)
    p_new = jnp.exp(s_new - m)
    pv = p_new * knew[:, KV_COLS:]
    for t in range(npg):
        pv = pv + _dot_nt(p[:, t * PAGE_SIZE:(t + 1) * PAGE_SIZE].astype(BF16),
                          page_refs[t][0, KV_COLS:, :].astype(BF16))
    os_ = pv / (jnp.sum(p, axis=1, keepdims=True) + p_new)
    gate = 1.0 / (1.0 + jnp.exp(-(zg_ref[0] + bg_ref[0])))
    o = gate[:, 0:1] * oc_ref[0] + gate[:, 1:2] * os_ + gate[:, 2:3] * ow_ref[0]
    gsel = lax.broadcasted_iota(jnp.int32, (SROWS, HEAD_DIM), 0) & 7
    res = jnp.zeros((SROWS, HEAD_DIM), F32)
    for g in range(N_KV):
        res = res + jnp.where(gsel == g, o[:, g * HEAD_DIM:(g + 1) * HEAD_DIM], 0.0)
    out_ref[0] = res


def _nsa_sample(z, b_g, kvc, cache_sel, page_table, sel_new, win_state, win_new, past_len):
    n_seq, n_pages = page_table.shape
    qpos = past_len
    n_blk = -(-(past_len + 1) // SEL_LEN)
    assert n_blk <= SBLK
    n_c = kvc.shape[2]
    n_w = win_state.shape[2]
    q4 = z[:, :D_MODEL].reshape(n_seq, N_KV, Q_PER_KV, HEAD_DIM)
    eye = jnp.eye(N_KV, dtype=F32)
    qbd = jnp.transpose(q4[:, :, :, None, :] * eye[None, :, None, :, None], (0, 2, 1, 3, 4))
    qbd = jnp.pad(qbd, ((0, 0), (0, 0), (0, 8 - N_KV), (0, 0), (0, 0))).reshape(n_seq, SROWS, KV_COLS)
    def gate_rows(a):
        n = a.shape[0]
        a = jnp.transpose(a.reshape(n, N_KV, Q_PER_KV, 3), (0, 2, 1, 3))
        return jnp.pad(a, ((0, 0), (0, 0), (0, 8 - N_KV), (0, LANES - 3))).reshape(n, SROWS, LANES)

    zg = gate_rows(z[:, D_MODEL:D_MODEL + 3 * N_HEADS])
    bg = gate_rows(b_g[None, :])
    smap = _selection_map(n_c, n_blk, SBLK)
    per_seq = lambda shape: pl.BlockSpec((1,) + shape, lambda s: (s,) + (0,) * len(shape))
    o_c, o_w, imp = pl.pallas_call(
        functools.partial(_samp_cmpwin_kernel, qpos=qpos),
        grid=(n_seq,),
        in_specs=[per_seq((SROWS, KV_COLS)), per_seq((2, n_c, KV_COLS)),
                  pl.BlockSpec((n_c, SBLK), lambda s: (0, 0)), per_seq((2 * KV_COLS, n_w)),
                  per_seq((1, 2 * KV_COLS))],
        out_specs=(per_seq((SROWS, KV_COLS)), per_seq((SROWS, KV_COLS)), per_seq((8, SBLK))),
        out_shape=(jax.ShapeDtypeStruct((n_seq, SROWS, KV_COLS), F32),
                   jax.ShapeDtypeStruct((n_seq, SROWS, KV_COLS), F32),
                   jax.ShapeDtypeStruct((n_seq, 8, SBLK), F32)),
        compiler_params=_params(("parallel",)),
        name="nsa_sample_cmpwin",
    )(qbd, kvc, smap, win_state, win_new[:, None, :])
    bias = pl.pallas_call(
        functools.partial(_samp_topk_kernel, cur=qpos // SEL_LEN),
        out_shape=jax.ShapeDtypeStruct((n_seq * 8, SBLK), F32),
        name="nsa_sample_topk",
    )(imp.reshape(n_seq * 8, SBLK)).reshape(n_seq, 8, SBLK)

    n_past_blk = past_len // SEL_LEN
    keys = n_pages * PAGE_SIZE
    assert n_past_blk == n_blk - 1
    onehot = jnp.asarray(np.arange(keys)[None, :] // SEL_LEN == np.arange(n_past_blk)[:, None], BF16)

    def page_spec(t):
        return pl.BlockSpec((1, 2 * KV_COLS, PAGE_SIZE), lambda s, pt: (pt[s * n_pages + t], 0, 0))

    seq_blk = lambda shape: pl.BlockSpec((1,) + shape, lambda s, pt: (s,) + (0,) * len(shape))
    out = pl.pallas_call(
        functools.partial(_samp_sel_kernel, npg=n_pages),
        grid_spec=pltpu.PrefetchScalarGridSpec(
            num_scalar_prefetch=1,
            grid=(n_seq,),
            in_specs=[page_spec(t) for t in range(n_pages)] + [
                seq_blk((SROWS, KV_COLS)), seq_blk((8, SBLK)),
                pl.BlockSpec((n_past_blk, keys), lambda s, pt: (0, 0)), seq_blk((1, 2 * KV_COLS)),
                seq_blk((SROWS, KV_COLS)), seq_blk((SROWS, KV_COLS)), seq_blk((SROWS, LANES)),
                pl.BlockSpec((1, SROWS, LANES), lambda s, pt: (0, 0, 0))],
            out_specs=seq_blk((SROWS, HEAD_DIM))),
        out_shape=jax.ShapeDtypeStruct((n_seq, SROWS, HEAD_DIM), F32),
        compiler_params=_params(("parallel",)),
        name="nsa_sample_sel",
    )(page_table.reshape(-1), *([cache_sel] * n_pages), qbd, bias, onehot, sel_new[:, None, :], o_c, o_w, zg, bg)
    out = out.reshape(n_seq, Q_PER_KV, 8, HEAD_DIM)[:, :, :N_KV]
    return jnp.transpose(out, (0, 2, 1, 3)).reshape(n_seq, D_MODEL).astype(BF16)


def kernel(x_prompt, x_sample, cache_cmp_kv, cache_sel_kv, state_win_kv, page_table, a_norm_g, a_w_uv, a_v_norm_g, a_w_s, a_b_s, a_w_o, kv_norm_g, w_kv, cmp_pe_k, cmp_w1_k, cmp_w2_k, cmp_pe_v, cmp_w1_v, cmp_w2_v, b_norm_g, b_w_qg, b_b_g, b_w_o, mlp_norm_g, mlp_w_up, mlp_w_down, final_norm_g):
    b, t, _ = x_prompt.shape
    n_seq, dec_seq, _ = x_sample.shape
    n_pool, page, _, _, _ = cache_cmp_kv.shape
    n_pages = page_table.shape[1]
    past_len = n_pages * page
    win_buf = state_win_kv.shape[1]
    depth = mlp_norm_g.shape[0]
    n_a = a_norm_g.shape[0]
    assert dec_seq == 1 and page == PAGE_SIZE and t % PAGE_SIZE == 0
    cmp_w = (cmp_pe_k, cmp_w1_k, cmp_w2_k, cmp_pe_v, cmp_w1_v, cmp_w2_v)
    kv_shape = (2, N_KV, HEAD_DIM)
    gate_w = 3 * N_HEADS

    hp = x_prompt.reshape(b * t, D_MODEL)
    hs = x_sample.reshape(n_seq, D_MODEL)
    a_v_rows = []
    attn_p = attn_s = None
    for layer in range(depth):
        last = layer == depth - 1
        fin = final_norm_g if last else None
        if layer < n_a:
            i = layer
            hp = _gmlp_layer(hp, a_norm_g[i], a_w_uv[i], a_v_norm_g[i], a_w_s[i], a_b_s[i], a_w_o[i], chunked=True)
            hs, v_s = _gmlp_layer(hs, a_norm_g[i], a_w_uv[i], a_v_norm_g[i], a_w_s[i], a_b_s[i], a_w_o[i],
                                  chunked=False)
            a_v_rows.append(v_s.reshape(n_seq, 1, D_MODEL))
            hp = _mlp_layer(hp, mlp_norm_g[layer], mlp_w_up[layer], mlp_w_down[layer], final_g=fin)
            hs = _mlp_layer(hs, mlp_norm_g[layer], mlp_w_up[layer], mlp_w_down[layer], final_g=fin)
            continue
        if layer == n_a:
            cmp_p, sel_p, win_p = _norm_matmul(hp, kv_norm_g, w_kv, 3)
            cmp_s, sel_s, win_s = _norm_matmul(hs, kv_norm_g, w_kv, 3)
            ncol = 2 * KV_COLS // LANES
            kvc_p = _compress(cmp_p.reshape(b * t // PAGE_SIZE, ncol * PAGE_SIZE, LANES),
                              jnp.arange(b * t // PAGE_SIZE, dtype=jnp.int32).reshape(b, t // PAGE_SIZE), *cmp_w,
                              channel_major=False)

            def channel_major(a):
                return jnp.transpose(a, (0, 2, 3, 4, 1)).reshape(a.shape[0], 2 * KV_COLS, a.shape[1])

            kvc_s = _compress(channel_major(cache_cmp_kv), page_table, *cmp_w, channel_major=True)
            cache_sel = channel_major(cache_sel_kv)
            win_state = channel_major(state_win_kv)
        j = layer - n_a
        wg = b_w_qg[j][:, D_MODEL:].reshape(D_MODEL, N_KV, Q_PER_KV, 3)
        wg = jnp.transpose(wg, (0, 1, 3, 2)).reshape(D_MODEL, N_KV, 3 * Q_PER_KV)
        wg = jnp.pad(wg, ((0, 0), (0, 0), (0, LANES - 3 * Q_PER_KV))).reshape(D_MODEL, N_KV * LANES)
        bg = jnp.transpose(b_b_g[j].reshape(N_KV, Q_PER_KV, 3), (0, 2, 1)).reshape(N_KV, 3 * Q_PER_KV)
        bg = jnp.pad(bg, ((0, 0), (0, LANES - 3 * Q_PER_KV))).reshape(1, N_KV * LANES)
        (z_p,) = _norm_matmul(hp, b_norm_g[j], jnp.concatenate([b_w_qg[j][:, :D_MODEL], wg], axis=1), 1)
        attn_p = _nsa_prompt(z_p, bg, kvc_p, sel_p, win_p, b, t)
        w_s_pad = jnp.pad(b_w_qg[j], ((0, 0), (0, LANES - gate_w % LANES)))
        (z_s,) = _norm_matmul(hs, b_norm_g[j], w_s_pad, 1)
        attn_s = _nsa_sample(z_s, b_b_g[j], kvc_s, cache_sel, page_table, sel_s, win_state, win_s, past_len)
        hp = _mlp_layer(hp, mlp_norm_g[layer], mlp_w_up[layer], mlp_w_down[layer], attn=attn_p, w_o=b_w_o[j],
                        final_g=fin)
        hs = _mlp_layer(hs, mlp_norm_g[layer], mlp_w_up[layer], mlp_w_down[layer], attn=attn_s, w_o=b_w_o[j],
                        final_g=fin)

    y_prompt = hp.reshape(b, t, D_MODEL)
    y_sample = hs.reshape(n_seq, 1, D_MODEL)
    nw = min(WINDOW, t)
    win_kv_prompt = win_p.reshape(b, t, *kv_shape)[:, t - nw:]
    win_kv_sample = jnp.concatenate([state_win_kv, win_s.reshape(n_seq, 1, *kv_shape)], axis=1)[:, dec_seq:]
    return (y_prompt, y_sample, cmp_p.reshape(b, t, *kv_shape), sel_p.reshape(b, t, *kv_shape), win_kv_prompt,
            cmp_s.reshape(n_seq, 1, *kv_shape), sel_s.reshape(n_seq, 1, *kv_shape), win_kv_sample,
            jnp.stack(a_v_rows, axis=0))
```

```python
import functools

import numpy as np
import jax
import jax.numpy as jnp
from jax import lax
from jax.experimental import pallas as pl
from jax.experimental.pallas import tpu as pltpu

F32 = jnp.float32
BF16 = jnp.bfloat16

D_MODEL = 1024
D_FF = 4 * D_MODEL
CHUNK = 128
A_GROUP_DIM = 128
A_GROUPS = D_MODEL // A_GROUP_DIM
HEAD_DIM = 64
N_HEADS = D_MODEL // HEAD_DIM
N_KV = 4
Q_PER_KV = N_HEADS // N_KV
KV_COLS = N_KV * HEAD_DIM
CMP_STRIDE = 16
CMP_LEN = 2 * CMP_STRIDE
CMP_HIDDEN = 4 * HEAD_DIM
SEL_LEN = 64
N_SEL = 16
WINDOW = 512
Q_BLOCK = 256
PAGE_SIZE = 128
EPS = 1e-6
NEG_INF = -1e30
FORCE_SCORE = 1e6
SCALE = HEAD_DIM ** -0.5
LOG2E = 1.4426950408889634
LANES = 128
V_ROWS = 80
KA_COLS = 256
SEL_TILE = 512
PAGES_PER_STEP = 16
VMEM_LIMIT = 56 * 1024 * 1024


def _params(sem):
    return pltpu.CompilerParams(dimension_semantics=sem, vmem_limit_bytes=VMEM_LIMIT)


def _rms(x, g):
    return x * lax.rsqrt(jnp.mean(x * x, axis=-1, keepdims=True) + EPS) * g


def _dot(a, b):
    return jnp.dot(a, b, preferred_element_type=F32)


def _dot_nt(a, b):
    return lax.dot_general(a, b, (((1,), (1,)), ((), ())), preferred_element_type=F32)


def _gmlp_kernel(x_ref, g_ref, wuv_ref, gv_ref, ws_ref, bias_ref, wo_ref, out_ref, *rest, chunked, tm):
    x = x_ref[...]
    xn = _rms(x, g_ref[...]).astype(BF16)
    uv = _dot(xn, wuv_ref[...])
    u = uv[:, :D_MODEL]
    v = _rms(uv[:, D_MODEL:], gv_ref[...])
    if chunked:
        gated_ref = rest[0]
        row = lax.broadcasted_iota(jnp.int32, (CHUNK, CHUNK), 0)
        col = lax.broadcasted_iota(jnp.int32, (CHUNK, CHUNK), 1)
        for g in range(A_GROUPS):
            cs = slice(g * A_GROUP_DIM, (g + 1) * A_GROUP_DIM)
            w = jnp.where(row >= col, ws_ref[g], 0.0).astype(BF16)
            for c in range(tm // CHUNK):
                rs = slice(c * CHUNK, (c + 1) * CHUNK)
                s = _dot(w, v[rs, cs].astype(BF16)) + bias_ref[:, cs]
                gated_ref[rs, cs] = (u[rs, cs] * s).astype(BF16)
        gated = gated_ref[...]
    else:
        v_ref = rest[0]
        v_ref[...] = v
        gated = (u * (v * ws_ref[...] + bias_ref[...])).astype(BF16)
    out_ref[...] = x + _dot(gated, wo_ref[...])


def _gmlp_layer(h, g, w_uv, gv, w_s, b_s, w_o, *, chunked):
    m = h.shape[0]
    full = lambda shape: pl.BlockSpec(shape, lambda i: (0,) * len(shape))
    if chunked:
        tm = 512
        ws = w_s
        bias = jnp.repeat(b_s.T, A_GROUP_DIM, axis=1)
        ws_spec = full((A_GROUPS, CHUNK, CHUNK))
        bias_spec = full((CHUNK, D_MODEL))
        out_shape = jax.ShapeDtypeStruct((m, D_MODEL), F32)
        out_specs = pl.BlockSpec((tm, D_MODEL), lambda i: (i, 0))
        scratch = [pltpu.VMEM((tm, D_MODEL), BF16)]
    else:
        tm = m
        ws = jnp.repeat(w_s[:, 0, 0], A_GROUP_DIM)[None, :]
        bias = jnp.repeat(b_s[:, 0], A_GROUP_DIM)[None, :]
        ws_spec = full((1, D_MODEL))
        bias_spec = full((1, D_MODEL))
        out_shape = (jax.ShapeDtypeStruct((m, D_MODEL), F32), jax.ShapeDtypeStruct((m, D_MODEL), F32))
        out_specs = (pl.BlockSpec((tm, D_MODEL), lambda i: (i, 0)), pl.BlockSpec((tm, D_MODEL), lambda i: (i, 0)))
        scratch = []
    assert m % tm == 0
    return pl.pallas_call(
        functools.partial(_gmlp_kernel, chunked=chunked, tm=tm),
        grid=(m // tm,),
        in_specs=[pl.BlockSpec((tm, D_MODEL), lambda i: (i, 0)), full((1, D_MODEL)),
                  full((D_MODEL, 2 * D_MODEL)), full((1, D_MODEL)), ws_spec, bias_spec,
                  full((D_MODEL, D_MODEL))],
        out_specs=out_specs, out_shape=out_shape, scratch_shapes=scratch,
        compiler_params=_params(("parallel",)),
        name="gmlp_prompt" if chunked else "gmlp_sample",
    )(h, g[None, :], w_uv.astype(BF16), gv[None, :], ws, bias, w_o.astype(BF16))


def _mlp_kernel(*refs, has_attn, has_final):
    refs = list(refs)
    x_ref = refs.pop(0)
    attn_ref = refs.pop(0) if has_attn else None
    wo_ref = refs.pop(0) if has_attn else None
    g_ref, wup_ref, wdn_ref = refs.pop(0), refs.pop(0), refs.pop(0)
    fg_ref = refs.pop(0) if has_final else None
    out_ref, h_ref, xn_ref, acc_ref = refs
    j = pl.program_id(1)

    @pl.when(j == 0)
    def _():
        h = x_ref[...]
        if has_attn:
            h = h + _dot(attn_ref[...], wo_ref[...])
        h_ref[...] = h
        xn_ref[...] = _rms(h, g_ref[...]).astype(BF16)
        acc_ref[...] = jnp.zeros_like(acc_ref)

    a = _dot(xn_ref[...], wup_ref[...])
    a = jnp.square(jnp.maximum(a, 0.0)).astype(BF16)
    acc_ref[...] += _dot(a, wdn_ref[...])

    @pl.when(j == pl.num_programs(1) - 1)
    def _():
        h = h_ref[...] + acc_ref[...]
        if has_final:
            h = _rms(h, fg_ref[...])
        out_ref[...] = h


def _mlp_layer(h, g, w_up, w_down, *, attn=None, w_o=None, final_g=None):
    m = h.shape[0]
    tm = min(m, 1024)
    tf = 1024
    assert m % tm == 0 and D_FF % tf == 0
    has_attn, has_final = attn is not None, final_g is not None
    row = pl.BlockSpec((tm, D_MODEL), lambda i, j: (i, 0))
    vec = pl.BlockSpec((1, D_MODEL), lambda i, j: (0, 0))
    args, specs = [h], [row]
    if has_attn:
        args += [attn, w_o.astype(BF16)]
        specs += [row, pl.BlockSpec((D_MODEL, D_MODEL), lambda i, j: (0, 0))]
    args += [g[None, :], w_up.astype(BF16), w_down.astype(BF16)]
    specs += [vec, pl.BlockSpec((D_MODEL, tf), lambda i, j: (0, j)), pl.BlockSpec((tf, D_MODEL), lambda i, j: (j, 0))]
    if has_final:
        args.append(final_g[None, :])
        specs.append(vec)
    return pl.pallas_call(
        functools.partial(_mlp_kernel, has_attn=has_attn, has_final=has_final),
        grid=(m // tm, D_FF // tf),
        in_specs=specs, out_specs=row,
        out_shape=jax.ShapeDtypeStruct((m, D_MODEL), F32),
        scratch_shapes=[pltpu.VMEM((tm, D_MODEL), F32), pltpu.VMEM((tm, D_MODEL), BF16),
                        pltpu.VMEM((tm, D_MODEL), F32)],
        compiler_params=_params(("parallel", "arbitrary")),
        name="mlp",
    )(*args)


def _norm_matmul_kernel(x_ref, g_ref, w_ref, *out_refs):
    xn = _rms(x_ref[...], g_ref[...]).astype(BF16)
    y = _dot(xn, w_ref[...])
    wd = y.shape[1] // len(out_refs)
    for k, o_ref in enumerate(out_refs):
        o_ref[...] = y[:, k * wd:(k + 1) * wd]


def _norm_matmul(h, g, w, n_out):
    m, n = h.shape[0], w.shape[1]
    tm = min(m, 1024)
    wd = n // n_out
    assert m % tm == 0 and n % n_out == 0 and wd % LANES == 0
    outs = pl.pallas_call(
        _norm_matmul_kernel,
        grid=(m // tm,),
        in_specs=[pl.BlockSpec((tm, D_MODEL), lambda i: (i, 0)), pl.BlockSpec((1, D_MODEL), lambda i: (0, 0)),
                  pl.BlockSpec((D_MODEL, n), lambda i: (0, 0))],
        out_specs=tuple(pl.BlockSpec((tm, wd), lambda i: (i, 0)) for _ in range(n_out)),
        out_shape=tuple(jax.ShapeDtypeStruct((m, wd), F32) for _ in range(n_out)),
        compiler_params=_params(("parallel",)),
        name="norm_matmul",
    )(h, g[None, :], w.astype(BF16))
    return outs


def _compress_kernel(pt_ref, *refs, npg, channel_major):
    page_refs = refs[:npg]
    w1_ref, pe_ref, w2_ref, out_ref, z_ref, aprev_ref = refs[npg:npg + 6]
    j = pl.program_id(1)
    grp = PAGE_SIZE // CMP_STRIDE
    sec = grp * npg
    rows = N_KV * sec

    @pl.when(j == 0)
    def _():
        aprev_ref[...] = jnp.zeros_like(aprev_ref)

    ncol = 2 * KV_COLS // LANES
    if channel_major:
        tr_ref = refs[npg + 6]
        for t in range(npg):
            for c in range(ncol):
                tr_ref[t * ncol + c] = page_refs[t][0, c * LANES:(c + 1) * LANES, :].T

        def rows_of(t, c, p):
            return tr_ref[t * ncol + c, pl.ds(p, grp, stride=CMP_STRIDE), :]
    else:
        def rows_of(t, c, p):
            return page_refs[t][0, pl.ds(ncol * p + c, grp, stride=ncol * CMP_STRIDE), :]

    for t in range(npg):
        for q in range(CMP_STRIDE // 2):
            for c in range(ncol):
                xa = rows_of(t, c, 2 * q)
                xb = rows_of(t, c, 2 * q + 1)
                kv = c // (ncol // 2)
                for half in range(LANES // HEAD_DIM):
                    g = (c % (ncol // 2)) * (LANES // HEAD_DIM) + half
                    hs = slice(half * HEAD_DIM, (half + 1) * HEAD_DIM)
                    piece = jnp.concatenate([xa[:, hs], xb[:, hs]], axis=1)
                    z_ref[kv, g * sec + t * grp:g * sec + (t + 1) * grp, q * LANES:(q + 1) * LANES] = piece

    rid = lax.broadcasted_iota(jnp.int32, (rows, CMP_HIDDEN), 0) % sec
    for kv in range(2):
        z = z_ref[kv]
        za = (z + pe_ref[kv, 0:1, :]).astype(BF16)
        zb = (z + pe_ref[kv, 1:2, :]).astype(BF16)
        a = _dot(za, w1_ref[kv, :, :CMP_HIDDEN])
        b = _dot(zb, w1_ref[kv, :, CMP_HIDDEN:])
        a_prev = jnp.where(rid == 0, pltpu.roll(aprev_ref[kv], rows - (sec - 1), 0), pltpu.roll(a, 1, 0))
        aprev_ref[kv] = a
        hid = jax.nn.gelu(a_prev + b).astype(BF16)
        o = _dot(hid, w2_ref[kv])
        out_ref[0, kv] = jnp.concatenate([o[g * sec:(g + 1) * sec] for g in range(N_KV)], axis=1)


def _compress(pages, page_table, pe_k, w1_k, w2_k, pe_v, w1_v, w2_v, *, channel_major):
    n_seq, n_pages = page_table.shape
    npg = PAGES_PER_STEP
    assert n_pages % npg == 0
    grp = PAGE_SIZE // CMP_STRIDE
    half = CMP_STRIDE * HEAD_DIM

    def split(w1):
        return jnp.concatenate([w1[:half], w1[half:]], axis=1)

    w1 = jnp.stack([split(w1_k), split(w1_v)]).astype(BF16)
    pe = jnp.stack([pe_k.reshape(2, half), pe_v.reshape(2, half)])
    w2 = jnp.stack([w2_k, w2_v]).astype(BF16)

    ncol = 2 * KV_COLS // LANES
    scratch = [pltpu.VMEM((2, N_KV * grp * npg, half), F32), pltpu.VMEM((2, N_KV * grp * npg, CMP_HIDDEN), F32)]
    if channel_major:
        scratch.append(pltpu.VMEM((npg * ncol, PAGE_SIZE, LANES), F32))

    def page_spec(t):
        return pl.BlockSpec((1, ncol * PAGE_SIZE, LANES), lambda s, j, pt: (pt[s * n_pages + j * npg + t], 0, 0))

    const = lambda shape: pl.BlockSpec(shape, lambda s, j, pt: (0,) * len(shape))
    return pl.pallas_call(
        functools.partial(_compress_kernel, npg=npg, channel_major=channel_major),
        grid_spec=pltpu.PrefetchScalarGridSpec(
            num_scalar_prefetch=1,
            grid=(n_seq, n_pages // npg),
            in_specs=[page_spec(t) for t in range(npg)] + [const((2, half, 2 * CMP_HIDDEN)), const((2, 2, half)),
                                                           const((2, CMP_HIDDEN, HEAD_DIM))],
            out_specs=pl.BlockSpec((1, 2, grp * npg, KV_COLS), lambda s, j, pt: (s, 0, j, 0)),
            scratch_shapes=scratch),
        out_shape=jax.ShapeDtypeStruct((n_seq, 2, n_pages * grp, KV_COLS), F32),
        compiler_params=_params(("parallel", "arbitrary")),
        name="compress",
    )(page_table.reshape(-1), *([pages] * npg), w1, pe, w2)


def _selection_map(n_rows, n_blk, n_cols):
    i = np.arange(n_rows)[:, None] - 1
    j = np.arange(n_cols)[None, :]
    lo = np.maximum(i * CMP_STRIDE, j * SEL_LEN)
    hi = np.minimum(i * CMP_STRIDE + CMP_LEN, (j + 1) * SEL_LEN)
    m = np.maximum(hi - lo, 0) // CMP_STRIDE
    m = np.where((i >= 0) & (j < n_blk), m, 0)
    return jnp.asarray(m, dtype=BF16)


def _top_blocks_along_rows(score, forced, n_rows):
    ridx = lax.broadcasted_iota(jnp.int32, score.shape, 0)
    work = jnp.where(forced, -3e38, score)
    for _ in range(N_SEL - 3):
        mx = jnp.max(work, axis=0, keepdims=True)
        first = jnp.min(jnp.where(work == mx, ridx, n_rows), axis=0, keepdims=True)
        work = jnp.where(ridx == first, -3e38, work)
    return work < -1e38


def _query_columns(zq):
    qt = (zq * (SCALE * LOG2E)).T
    return jnp.concatenate([qt[r * HEAD_DIM:(r + 1) * HEAD_DIM] for r in range(Q_PER_KV)], axis=1).astype(BF16)


def _nsa_prompt_kernel(zq_ref, zg_ref, bg_ref, kc_ref, vct_ref, smapt_ref, ka_ref, vat_ref, kw_ref, vwt_ref,
                       out_ref, s_scr, p_scr, m_scr, acc_scr, mx_scr):
    n = pl.program_id(2)
    s0 = n * Q_BLOCK
    cols = Q_PER_KV * Q_BLOCK
    qst = _query_columns(zq_ref[...])
    i_col = lax.broadcasted_iota(jnp.int32, (1, cols), 1) & (Q_BLOCK - 1)
    qpos = s0 + i_col

    n_c = kc_ref.shape[2]
    sc = _dot(kc_ref[0, 0], qst)
    cidx = lax.broadcasted_iota(jnp.int32, (n_c, cols), 0)
    lim = jnp.maximum((qpos - (CMP_STRIDE - 1)) // CMP_STRIDE, 0)
    scm = jnp.where((cidx >= 1) & (cidx <= lim), sc, NEG_INF)
    mx_c = jnp.where(lim >= 1, jnp.max(scm, axis=0, keepdims=True), -NEG_INF)
    p = jnp.exp2(scm - mx_c)
    acc_c = _dot(vct_ref[0, 0], p.astype(BF16))
    l_c = acc_c[HEAD_DIM:HEAD_DIM + 1]
    o_c = acc_c[:HEAD_DIM] * jnp.where(l_c > 0.0, 1.0 / l_c, 0.0)
    l32 = jnp.sum(p, axis=0, keepdims=True)
    pn = p * jnp.where(l32 > 0.0, 1.0 / l32, 0.0)
    psum = pn[:, 0:Q_BLOCK]
    for r in range(1, Q_PER_KV):
        psum = psum + pn[:, r * Q_BLOCK:(r + 1) * Q_BLOCK]
    p_hi = psum.astype(BF16)
    p_lo = (psum - p_hi.astype(F32)).astype(BF16)
    imp_t = _dot(smapt_ref[...], p_hi) + _dot(smapt_ref[...], p_lo)

    wk = WINDOW + Q_BLOCK
    w0 = pl.multiple_of(s0, Q_BLOCK)
    flag_row = lax.broadcasted_iota(jnp.int32, (LANES - HEAD_DIM, cols), 0) == 0
    qw = jnp.concatenate([qst, jnp.where(flag_row, NEG_INF, 0.0).astype(BF16)], axis=0)
    sw = _dot(kw_ref[0, 0, pl.ds(w0, wk), :], qw)
    rr = lax.broadcasted_iota(jnp.int32, (Q_BLOCK, cols), 0)
    swm = jnp.concatenate([jnp.where(rr > i_col, sw[:Q_BLOCK], NEG_INF), sw[Q_BLOCK:WINDOW],
                           jnp.where(rr <= i_col, sw[WINDOW:], NEG_INF)], axis=0)
    pw = jnp.exp2(swm - jnp.max(swm, axis=0, keepdims=True)).astype(BF16)
    acc_w = _dot(vwt_ref[0, 0, :, pl.ds(w0, wk)], pw)
    o_w = acc_w[:HEAD_DIM] / acc_w[HEAD_DIM:HEAD_DIM + 1]

    nb = imp_t.shape[0]
    jt = lax.broadcasted_iota(jnp.int32, (nb, Q_BLOCK), 0)
    cur = (s0 + lax.broadcasted_iota(jnp.int32, (nb, Q_BLOCK), 1)) // SEL_LEN
    forced = (jt == 0) | (jt == cur) | (jt == cur - 1)
    score = jnp.where(forced, FORCE_SCORE, jnp.where(jt <= cur, imp_t, -1.0))
    sel = _top_blocks_along_rows(score, forced, nb)
    bias_t = jnp.where(sel & (score >= 0.0), 0.0, NEG_INF).astype(BF16)
    qaug_t = jnp.concatenate([jnp.concatenate([bias_t] * Q_PER_KV, axis=1), qst,
                              jnp.zeros((KA_COLS - nb - HEAD_DIM, cols), BF16)], axis=0)

    def scores(t):
        k0 = pl.multiple_of(t * SEL_TILE, SEL_TILE)
        return _dot(ka_ref[0, 0, pl.ds(k0, SEL_TILE), :], qaug_t)

    def weighted(pt, t):
        k0 = pl.multiple_of(t * SEL_TILE, SEL_TILE)
        return _dot(vat_ref[0, 0, :, pl.ds(k0, SEL_TILE)], pt)

    def put_scores(slot, t):
        s = scores(t)
        s_scr[slot] = s
        mx_scr[slot] = jnp.max(s, axis=0, keepdims=True)

    def stage(t, cur, causal, prefetch):
        prv = 1 - cur
        pv = weighted(p_scr[prv], jnp.maximum(t - 1, 0))
        s = s_scr[cur]
        if causal:
            kpos = t * SEL_TILE + lax.broadcasted_iota(jnp.int32, (SEL_TILE, cols), 0)
            s = jnp.where(kpos <= qpos, s, NEG_INF)
            mx = jnp.max(s, axis=0, keepdims=True)
        else:
            mx = mx_scr[cur]
        m_prev = m_scr[...]
        m_new = jnp.maximum(m_prev, mx)
        p_scr[cur] = jnp.exp2(s - m_new).astype(BF16)
        m_scr[...] = m_new
        acc_scr[...] = (acc_scr[...] + pv) * jnp.exp2(m_prev - m_new)
        if prefetch:
            put_scores(prv, t + 1)

    def trip(i, carry):
        stage(2 * i, 0, False, True)
        stage(2 * i + 1, 1, False, True)
        return carry

    def finish(last_slot, last_tile):
        acc_s = acc_scr[...] + weighted(p_scr[last_slot], last_tile)
        o_s = acc_s[:HEAD_DIM] / acc_s[HEAD_DIM:HEAD_DIM + 1]
        gate_t = (1.0 / (1.0 + jnp.exp(-(zg_ref[...] + bg_ref[...])))).T
        outs = []
        for r in range(Q_PER_KV):
            cs = slice(r * Q_BLOCK, (r + 1) * Q_BLOCK)
            outs.append(gate_t[r:r + 1] * o_c[:, cs] + gate_t[Q_PER_KV + r:Q_PER_KV + r + 1] * o_s[:, cs]
                        + gate_t[2 * Q_PER_KV + r:2 * Q_PER_KV + r + 1] * o_w[:, cs])
        out_ref[...] = jnp.concatenate(outs, axis=0).T.astype(BF16)

    t_last = s0 // SEL_TILE
    n_pairs = t_last // 2
    put_scores(0, 0)
    p_scr[1] = jnp.zeros((SEL_TILE, cols), BF16)
    m_scr[...] = jnp.full((1, cols), NEG_INF, F32)
    acc_scr[...] = jnp.zeros((V_ROWS, cols), F32)
    lax.fori_loop(0, n_pairs, trip, 0)

    @pl.when(t_last % 2 == 1)
    def _():
        stage(2 * n_pairs, 0, False, True)
        stage(2 * n_pairs + 1, 1, True, False)
        finish(1, 2 * n_pairs + 1)

    @pl.when(t_last % 2 == 0)
    def _():
        stage(2 * n_pairs, 0, True, False)
        finish(0, 2 * n_pairs)


def _kv_layouts(rows, b, t, left_pad):
    k = rows[:, :KV_COLS].reshape(b, t, N_KV, HEAD_DIM).astype(BF16)
    v = rows[:, KV_COLS:].reshape(b, t, N_KV, HEAD_DIM).astype(BF16)
    kk = jnp.transpose(k, (0, 2, 1, 3))
    vt = jnp.concatenate([jnp.transpose(v, (0, 2, 3, 1)), jnp.ones((b, N_KV, 1, t), BF16),
                          jnp.zeros((b, N_KV, V_ROWS - HEAD_DIM - 1, t), BF16)], axis=2)
    if left_pad:
        kk = jnp.pad(kk, ((0, 0), (0, 0), (left_pad, 0), (0, LANES - HEAD_DIM)))
        flag = (np.arange(left_pad + t)[:, None] < left_pad) & (np.arange(LANES)[None, :] == HEAD_DIM)
        kk = kk + jnp.asarray(flag, BF16)
        vt = jnp.pad(vt, ((0, 0), (0, 0), (0, 0), (left_pad, 0)))
    return kk, vt


def _nsa_prompt(z, bg_pad, kvc, sel_rows, win_rows, b, t):
    nq = t // Q_BLOCK
    n_c = kvc.shape[2]
    n_blk = t // SEL_LEN
    assert t % SEL_TILE == 0 and SEL_TILE % Q_BLOCK == 0 and N_SEL <= n_blk <= LANES
    cols = Q_PER_KV * Q_BLOCK
    kc = jnp.transpose(kvc[:, 0].reshape(b, n_c, N_KV, HEAD_DIM).astype(BF16), (0, 2, 1, 3))
    vc = kvc[:, 1].reshape(b, n_c, N_KV, HEAD_DIM).astype(BF16)
    vct = jnp.concatenate([jnp.transpose(vc, (0, 2, 3, 1)), jnp.ones((b, N_KV, 1, n_c), BF16),
                           jnp.zeros((b, N_KV, V_ROWS - HEAD_DIM - 1, n_c), BF16)], axis=2)
    smapt = _selection_map(n_c, n_blk, LANES).T
    ks, vat = _kv_layouts(sel_rows, b, t, 0)
    key_in_blk = np.arange(t)[:, None] // SEL_LEN == np.arange(LANES)[None, :]
    ka = jnp.concatenate([jnp.broadcast_to(jnp.asarray(key_in_blk, BF16), (b, N_KV, t, LANES)), ks,
                          jnp.zeros((b, N_KV, t, KA_COLS - LANES - HEAD_DIM), BF16)], axis=3)
    kw, vwt = _kv_layouts(win_rows, b, t, WINDOW)
    per_bg = lambda shape: pl.BlockSpec((1, 1) + shape, lambda bb, g, n: (bb, g, 0, 0))
    return pl.pallas_call(
        _nsa_prompt_kernel,
        grid=(b, N_KV, nq),
        in_specs=[pl.BlockSpec((Q_BLOCK, Q_PER_KV * HEAD_DIM), lambda bb, g, n: (bb * nq + n, g)),
                  pl.BlockSpec((Q_BLOCK, LANES), lambda bb, g, n: (bb * nq + n, D_MODEL // LANES + g)),
                  pl.BlockSpec((1, LANES), lambda bb, g, n: (0, g)),
                  per_bg((n_c, HEAD_DIM)), per_bg((V_ROWS, n_c)),
                  pl.BlockSpec((LANES, n_c), lambda bb, g, n: (0, 0)),
                  per_bg((t, KA_COLS)), per_bg((V_ROWS, t)),
                  per_bg((WINDOW + t, LANES)), per_bg((V_ROWS, WINDOW + t))],
        out_specs=pl.BlockSpec((Q_BLOCK, Q_PER_KV * HEAD_DIM), lambda bb, g, n: (bb * nq + n, g)),
        out_shape=jax.ShapeDtypeStruct((b * t, D_MODEL), BF16),
        scratch_shapes=[pltpu.VMEM((2, SEL_TILE, cols), F32), pltpu.VMEM((2, SEL_TILE, cols), BF16),
                        pltpu.VMEM((1, cols), F32), pltpu.VMEM((V_ROWS, cols), F32),
                        pltpu.VMEM((2, 1, cols), F32)],
        compiler_params=_params(("parallel", "parallel", "arbitrary")),
        name="nsa_prompt",
    )(z, z, bg_pad, kc, vct, smapt, ka, vat, kw, vwt)


SROWS = 32
SBLK = 256


def _samp_cmpwin_kernel(qbd_ref, kvc_ref, smap_ref, win_ref, wnew_ref, oc_ref, ow_ref, imp_ref, *, qpos):
    qf = qbd_ref[0] * SCALE
    qb = qf.astype(BF16)
    kc = kvc_ref[0, 0].astype(BF16)
    vc = kvc_ref[0, 1].astype(BF16)
    n_c = kc.shape[0]
    sc = _dot_nt(qb, kc)
    cidx = lax.broadcasted_iota(jnp.int32, (SROWS, n_c), 1)
    mc = (cidx >= 1) & (cidx * CMP_STRIDE + (CMP_STRIDE - 1) <= qpos)
    scm = jnp.where(mc, sc, NEG_INF)
    p = jnp.where(mc, jnp.exp(scm - jnp.max(scm, axis=1, keepdims=True)), 0.0)
    l = jnp.sum(p, axis=1, keepdims=True)
    inv = jnp.where(l > 0.0, 1.0 / l, 0.0)
    oc_ref[0] = _dot(p.astype(BF16), vc) * inv
    pn = p * inv
    psum = pn[0:8] + pn[8:16] + pn[16:24] + pn[24:32]
    p_hi = psum.astype(BF16)
    p_lo = (psum - p_hi.astype(F32)).astype(BF16)
    imp_ref[0] = _dot(p_hi, smap_ref[...]) + _dot(p_lo, smap_ref[...])
    win = win_ref[0]
    n_w = win.shape[1]
    sw = _dot(qb, win[:KV_COLS].astype(BF16))
    widx = lax.broadcasted_iota(jnp.int32, (SROWS, n_w), 1)
    dist = n_w - widx
    mw = (dist < WINDOW) & (qpos - dist >= 0)
    swm = jnp.where(mw, sw, NEG_INF)
    wnew = wnew_ref[0]
    s_new = jnp.sum(qb.astype(F32) * wnew[:, :KV_COLS].astype(BF16).astype(F32), axis=1, keepdims=True)
    m = jnp.maximum(jnp.max(swm, axis=1, keepdims=True), s_new)
    pw = jnp.exp(swm - m)
    p_new = jnp.exp(s_new - m)
    lw = jnp.sum(pw, axis=1, keepdims=True) + p_new
    ow = _dot_nt(pw.astype(BF16), win[KV_COLS:].astype(BF16)) + p_new * wnew[:, KV_COLS:]
    ow_ref[0] = ow / lw


def _samp_topk_kernel(imp_ref, bias_ref, *, cur):
    imp = imp_ref[...]
    jidx = lax.broadcasted_iota(jnp.int32, imp.shape, 1)
    forced = (jidx == 0) | (jidx == cur) | (jidx == cur - 1)
    score = jnp.where(forced, FORCE_SCORE, jnp.where(jidx <= cur, imp, -1.0))
    work = score
    sel = jnp.zeros(imp.shape, F32)
    for _ in range(N_SEL):
        mx = jnp.max(work, axis=1, keepdims=True)
        first = jnp.min(jnp.where(work == mx, jidx, SBLK), axis=1, keepdims=True)
        hit = jidx == first
        sel = jnp.where(hit, 1.0, sel)
        work = jnp.where(hit, -3e38, work)
    bias_ref[...] = jnp.where((sel > 0.0) & (score >= 0.0), 0.0, NEG_INF)


def _samp_sel_kernel(pt_ref, *refs, npg):
    page_refs = refs[:npg]
    (qbd_ref, bias_ref, onehot_ref, knew_ref, oc_ref, ow_ref, zg_ref, bg_ref, out_ref) = refs[npg:]
    n_past_blk = npg * PAGE_SIZE // SEL_LEN
    qb = (qbd_ref[0] * SCALE).astype(BF16)
    bias8 = bias_ref[0][:, :n_past_blk].astype(BF16)
    bias32 = jnp.concatenate([bias8] * Q_PER_KV, axis=0)
    s = jnp.concatenate([_dot(qb, page_refs[t][0, :KV_COLS, :].astype(BF16)) for t in range(npg)], axis=1)
    s = s + _dot(bias32, onehot_ref[...])
    knew = knew_ref[0]
    s_new = jnp.sum(qb.astype(F32) * knew[:, :KV_COLS].astype(BF16).astype(F32), axis=1, keepdims=True)
    m = jnp.maximum(jnp.max(s, axis=1, keepdims=True), s_new)
    p = jnp.exp(s - m)
    p_new = jnp.exp(s_new - m)
    pv = p_new * knew[:, KV_COLS:]
    for t in range(npg):
        pv = pv + _dot_nt(p[:, t * PAGE_SIZE:(t + 1) * PAGE_SIZE].astype(BF16),
                          page_refs[t][0, KV_COLS:, :].astype(BF16))
    os_ = pv / (jnp.sum(p, axis=1, keepdims=True) + p_new)
    gate = 1.0 / (1.0 + jnp.exp(-(zg_ref[0] + bg_ref[0])))
    o = gate[:, 0:1] * oc_ref[0] + gate[:, 1:2] * os_ + gate[:, 2:3] * ow_ref[0]
    gsel = lax.broadcasted_iota(jnp.int32, (SROWS, HEAD_DIM), 0) & 7
    res = jnp.zeros((SROWS, HEAD_DIM), F32)
    for g in range(N_KV):
        res = res + jnp.where(gsel == g, o[:, g * HEAD_DIM:(g + 1) * HEAD_DIM], 0.0)
    out_ref[0] = res


def _nsa_sample(z, b_g, kvc, cache_sel, page_table, sel_new, win_state, win_new, past_len):
    n_seq, n_pages = page_table.shape
    qpos = past_len
    n_blk = -(-(past_len + 1) // SEL_LEN)
    assert n_blk <= SBLK
    n_c = kvc.shape[2]
    n_w = win_state.shape[2]
    q4 = z[:, :D_MODEL].reshape(n_seq, N_KV, Q_PER_KV, HEAD_DIM)
    eye = jnp.eye(N_KV, dtype=F32)
    qbd = jnp.transpose(q4[:, :, :, None, :] * eye[None, :, None, :, None], (0, 2, 1, 3, 4))
    qbd = jnp.pad(qbd, ((0, 0), (0, 0), (0, 8 - N_KV), (0, 0), (0, 0))).reshape(n_seq, SROWS, KV_COLS)
    def gate_rows(a):
        n = a.shape[0]
        a = jnp.transpose(a.reshape(n, N_KV, Q_PER_KV, 3), (0, 2, 1, 3))
        return jnp.pad(a, ((0, 0), (0, 0), (0, 8 - N_KV), (0, LANES - 3))).reshape(n, SROWS, LANES)

    zg = gate_rows(z[:, D_MODEL:D_MODEL + 3 * N_HEADS])
    bg = gate_rows(b_g[None, :])
    smap = _selection_map(n_c, n_blk, SBLK)
    per_seq = lambda shape: pl.BlockSpec((1,) + shape, lambda s: (s,) + (0,) * len(shape))
    o_c, o_w, imp = pl.pallas_call(
        functools.partial(_samp_cmpwin_kernel, qpos=qpos),
        grid=(n_seq,),
        in_specs=[per_seq((SROWS, KV_COLS)), per_seq((2, n_c, KV_COLS)),
                  pl.BlockSpec((n_c, SBLK), lambda s: (0, 0)), per_seq((2 * KV_COLS, n_w)),
                  per_seq((1, 2 * KV_COLS))],
        out_specs=(per_seq((SROWS, KV_COLS)), per_seq((SROWS, KV_COLS)), per_seq((8, SBLK))),
        out_shape=(jax.ShapeDtypeStruct((n_seq, SROWS, KV_COLS), F32),
                   jax.ShapeDtypeStruct((n_seq, SROWS, KV_COLS), F32),
                   jax.ShapeDtypeStruct((n_seq, 8, SBLK), F32)),
        compiler_params=_params(("parallel",)),
        name="nsa_sample_cmpwin",
    )(qbd, kvc, smap, win_state, win_new[:, None, :])
    bias = pl.pallas_call(
        functools.partial(_samp_topk_kernel, cur=qpos // SEL_LEN),
        out_shape=jax.ShapeDtypeStruct((n_seq * 8, SBLK), F32),
        name="nsa_sample_topk",
    )(imp.reshape(n_seq * 8, SBLK)).reshape(n_seq, 8, SBLK)

    n_past_blk = past_len // SEL_LEN
    keys = n_pages * PAGE_SIZE
    assert n_past_blk == n_blk - 1
    onehot = jnp.asarray(np.arange(keys)[None, :] // SEL_LEN == np.arange(n_past_blk)[:, None], BF16)

    def page_spec(t):
        return pl.BlockSpec((1, 2 * KV_COLS, PAGE_SIZE), lambda s, pt: (pt[s * n_pages + t], 0, 0))

    seq_blk = lambda shape: pl.BlockSpec((1,) + shape, lambda s, pt: (s,) + (0,) * len(shape))
    out = pl.pallas_call(
        functools.partial(_samp_sel_kernel, npg=n_pages),
        grid_spec=pltpu.PrefetchScalarGridSpec(
            num_scalar_prefetch=1,
            grid=(n_seq,),
            in_specs=[page_spec(t) for t in range(n_pages)] + [
                seq_blk((SROWS, KV_COLS)), seq_blk((8, SBLK)),
                pl.BlockSpec((n_past_blk, keys), lambda s, pt: (0, 0)), seq_blk((1, 2 * KV_COLS)),
                seq_blk((SROWS, KV_COLS)), seq_blk((SROWS, KV_COLS)), seq_blk((SROWS, LANES)),
                pl.BlockSpec((1, SROWS, LANES), lambda s, pt: (0, 0, 0))],
            out_specs=seq_blk((SROWS, HEAD_DIM))),
        out_shape=jax.ShapeDtypeStruct((n_seq, SROWS, HEAD_DIM), F32),
        compiler_params=_params(("parallel",)),
        name="nsa_sample_sel",
    )(page_table.reshape(-1), *([cache_sel] * n_pages), qbd, bias, onehot, sel_new[:, None, :], o_c, o_w, zg, bg)
    out = out.reshape(n_seq, Q_PER_KV, 8, HEAD_DIM)[:, :, :N_KV]
    return jnp.transpose(out, (0, 2, 1, 3)).reshape(n_seq, D_MODEL).astype(BF16)


def kernel(x_prompt, x_sample, cache_cmp_kv, cache_sel_kv, state_win_kv, page_table, a_norm_g, a_w_uv, a_v_norm_g, a_w_s, a_b_s, a_w_o, kv_norm_g, w_kv, cmp_pe_k, cmp_w1_k, cmp_w2_k, cmp_pe_v, cmp_w1_v, cmp_w2_v, b_norm_g, b_w_qg, b_b_g, b_w_o, mlp_norm_g, mlp_w_up, mlp_w_down, final_norm_g):
    b, t, _ = x_prompt.shape
    n_seq, dec_seq, _ = x_sample.shape
    n_pool, page, _, _, _ = cache_cmp_kv.shape
    n_pages = page_table.shape[1]
    past_len = n_pages * page
    win_buf = state_win_kv.shape[1]
    depth = mlp_norm_g.shape[0]
    n_a = a_norm_g.shape[0]
    assert dec_seq == 1 and page == PAGE_SIZE and t % PAGE_SIZE == 0
    cmp_w = (cmp_pe_k, cmp_w1_k, cmp_w2_k, cmp_pe_v, cmp_w1_v, cmp_w2_v)
    kv_shape = (2, N_KV, HEAD_DIM)
    gate_w = 3 * N_HEADS

    hp = x_prompt.reshape(b * t, D_MODEL)
    hs = x_sample.reshape(n_seq, D_MODEL)
    a_v_rows = []
    attn_p = attn_s = None
    for layer in range(depth):
        last = layer == depth - 1
        fin = final_norm_g if last else None
        if layer < n_a:
            i = layer
            hp = _gmlp_layer(hp, a_norm_g[i], a_w_uv[i], a_v_norm_g[i], a_w_s[i], a_b_s[i], a_w_o[i], chunked=True)
            hs, v_s = _gmlp_layer(hs, a_norm_g[i], a_w_uv[i], a_v_norm_g[i], a_w_s[i], a_b_s[i], a_w_o[i],
                                  chunked=False)
            a_v_rows.append(v_s.reshape(n_seq, 1, D_MODEL))
            hp = _mlp_layer(hp, mlp_norm_g[layer], mlp_w_up[layer], mlp_w_down[layer], final_g=fin)
            hs = _mlp_layer(hs, mlp_norm_g[layer], mlp_w_up[layer], mlp_w_down[layer], final_g=fin)
            continue
        if layer == n_a:
            cmp_p, sel_p, win_p = _norm_matmul(hp, kv_norm_g, w_kv, 3)
            cmp_s, sel_s, win_s = _norm_matmul(hs, kv_norm_g, w_kv, 3)
            ncol = 2 * KV_COLS // LANES
            kvc_p = _compress(cmp_p.reshape(b * t // PAGE_SIZE, ncol * PAGE_SIZE, LANES),
                              jnp.arange(b * t // PAGE_SIZE, dtype=jnp.int32).reshape(b, t // PAGE_SIZE), *cmp_w,
                              channel_major=False)

            def channel_major(a):
                return jnp.transpose(a, (0, 2, 3, 4, 1)).reshape(a.shape[0], 2 * KV_COLS, a.shape[1])

            kvc_s = _compress(channel_major(cache_cmp_kv), page_table, *cmp_w, channel_major=True)
            cache_sel = channel_major(cache_sel_kv)
            win_state = channel_major(state_win_kv)
        j = layer - n_a
        wg = b_w_qg[j][:, D_MODEL:].reshape(D_MODEL, N_KV, Q_PER_KV, 3)
        wg = jnp.transpose(wg, (0, 1, 3, 2)).reshape(D_MODEL, N_KV, 3 * Q_PER_KV)
        wg = jnp.pad(wg, ((0, 0), (0, 0), (0, LANES - 3 * Q_PER_KV))).reshape(D_MODEL, N_KV * LANES)
        bg = jnp.transpose(b_b_g[j].reshape(N_KV, Q_PER_KV, 3), (0, 2, 1)).reshape(N_KV, 3 * Q_PER_KV)
        bg = jnp.pad(bg, ((0, 0), (0, LANES - 3 * Q_PER_KV))).reshape(1, N_KV * LANES)
        (z_p,) = _norm_matmul(hp, b_norm_g[j], jnp.concatenate([b_w_qg[j][:, :D_MODEL], wg], axis=1), 1)
        attn_p = _nsa_prompt(z_p, bg, kvc_p, sel_p, win_p, b, t)
        w_s_pad = jnp.pad(b_w_qg[j], ((0, 0), (0, LANES - gate_w % LANES)))
        (z_s,) = _norm_matmul(hs, b_norm_g[j], w_s_pad, 1)
        attn_s = _nsa_sample(z_s, b_b_g[j], kvc_s, cache_sel, page_table, sel_s, win_state, win_s, past_len)
        hp = _mlp_layer(hp, mlp_norm_g[layer], mlp_w_up[layer], mlp_w_down[layer], attn=attn_p, w_o=b_w_o[j],
                        final_g=fin)
        hs = _mlp_layer(hs, mlp_norm_g[layer], mlp_w_up[layer], mlp_w_down[layer], attn=attn_s, w_o=b_w_o[j],
                        final_g=fin)

    y_prompt = hp.reshape(b, t, D_MODEL)
    y_sample = hs.reshape(n_seq, 1, D_MODEL)
    nw = min(WINDOW, t)
    win_kv_prompt = win_p.reshape(b, t, *kv_shape)[:, t - nw:]
    win_kv_sample = jnp.concatenate([state_win_kv, win_s.reshape(n_seq, 1, *kv_shape)], axis=1)[:, dec_seq:]
    return (y_prompt, y_sample, cmp_p.reshape(b, t, *kv_shape), sel_p.reshape(b, t, *kv_shape), win_kv_prompt,
            cmp_s.reshape(n_seq, 1, *kv_shape), sel_s.reshape(n_seq, 1, *kv_shape), win_kv_sample,
            jnp.stack(a_v_rows, axis=0))
```

```python
import functools

import numpy as np
import jax
import jax.numpy as jnp
from jax import lax
from jax.experimental import pallas as pl
from jax.experimental.pallas import tpu as pltpu

F32 = jnp.float32
BF16 = jnp.bfloat16

D_MODEL = 1024
D_FF = 4 * D_MODEL
CHUNK = 128
A_GROUP_DIM = 128
A_GROUPS = D_MODEL // A_GROUP_DIM
HEAD_DIM = 64
N_HEADS = D_MODEL // HEAD_DIM
N_KV = 4
Q_PER_KV = N_HEADS // N_KV
KV_COLS = N_KV * HEAD_DIM
CMP_STRIDE = 16
CMP_LEN = 2 * CMP_STRIDE
CMP_HIDDEN = 4 * HEAD_DIM
SEL_LEN = 64
N_SEL = 16
WINDOW = 512
Q_BLOCK = 256
PAGE_SIZE = 128
EPS = 1e-6
NEG_INF = -1e30
FORCE_SCORE = 1e6
SCALE = HEAD_DIM ** -0.5
LOG2E = 1.4426950408889634
LANES = 128
V_ROWS = 80
KA_COLS = 256
SEL_TILE = 512
PAGES_PER_STEP = 16
VMEM_LIMIT = 56 * 1024 * 1024


def _params(sem):
    return pltpu.CompilerParams(dimension_semantics=sem, vmem_limit_bytes=VMEM_LIMIT)


def _rms(x, g):
    return x * lax.rsqrt(jnp.mean(x * x, axis=-1, keepdims=True) + EPS) * g


def _dot(a, b):
    return jnp.dot(a, b, preferred_element_type=F32)


def _dot_nt(a, b):
    return lax.dot_general(a, b, (((1,), (1,)), ((), ())), preferred_element_type=F32)


def _gmlp_kernel(x_ref, g_ref, wuv_ref, gv_ref, ws_ref, bias_ref, wo_ref, out_ref, *rest, chunked, tm):
    x = x_ref[...]
    xn = _rms(x, g_ref[...]).astype(BF16)
    uv = _dot(xn, wuv_ref[...])
    u = uv[:, :D_MODEL]
    v = _rms(uv[:, D_MODEL:], gv_ref[...])
    if chunked:
        gated_ref = rest[0]
        row = lax.broadcasted_iota(jnp.int32, (CHUNK, CHUNK), 0)
        col = lax.broadcasted_iota(jnp.int32, (CHUNK, CHUNK), 1)
        for g in range(A_GROUPS):
            cs = slice(g * A_GROUP_DIM, (g + 1) * A_GROUP_DIM)
            w = jnp.where(row >= col, ws_ref[g], 0.0).astype(BF16)
            for c in range(tm // CHUNK):
                rs = slice(c * CHUNK, (c + 1) * CHUNK)
                s = _dot(w, v[rs, cs].astype(BF16)) + bias_ref[:, cs]
                gated_ref[rs, cs] = (u[rs, cs] * s).astype(BF16)
        gated = gated_ref[...]
    else:
        v_ref = rest[0]
        v_ref[...] = v
        gated = (u * (v * ws_ref[...] + bias_ref[...])).astype(BF16)
    out_ref[...] = x + _dot(gated, wo_ref[...])


def _gmlp_layer(h, g, w_uv, gv, w_s, b_s, w_o, *, chunked):
    m = h.shape[0]
    full = lambda shape: pl.BlockSpec(shape, lambda i: (0,) * len(shape))
    if chunked:
        tm = 512
        ws = w_s
        bias = jnp.repeat(b_s.T, A_GROUP_DIM, axis=1)
        ws_spec = full((A_GROUPS, CHUNK, CHUNK))
        bias_spec = full((CHUNK, D_MODEL))
        out_shape = jax.ShapeDtypeStruct((m, D_MODEL), F32)
        out_specs = pl.BlockSpec((tm, D_MODEL), lambda i: (i, 0))
        scratch = [pltpu.VMEM((tm, D_MODEL), BF16)]
    else:
        tm = m
        ws = jnp.repeat(w_s[:, 0, 0], A_GROUP_DIM)[None, :]
        bias = jnp.repeat(b_s[:, 0], A_GROUP_DIM)[None, :]
        ws_spec = full((1, D_MODEL))
        bias_spec = full((1, D_MODEL))
        out_shape = (jax.ShapeDtypeStruct((m, D_MODEL), F32), jax.ShapeDtypeStruct((m, D_MODEL), F32))
        out_specs = (pl.BlockSpec((tm, D_MODEL), lambda i: (i, 0)), pl.BlockSpec((tm, D_MODEL), lambda i: (i, 0)))
        scratch = []
    assert m % tm == 0
    return pl.pallas_call(
        functools.partial(_gmlp_kernel, chunked=chunked, tm=tm),
        grid=(m // tm,),
        in_specs=[pl.BlockSpec((tm, D_MODEL), lambda i: (i, 0)), full((1, D_MODEL)),
                  full((D_MODEL, 2 * D_MODEL)), full((1, D_MODEL)), ws_spec, bias_spec,
                  full((D_MODEL, D_MODEL))],
        out_specs=out_specs, out_shape=out_shape, scratch_shapes=scratch,
        compiler_params=_params(("parallel",)),
        name="gmlp_prompt" if chunked else "gmlp_sample",
    )(h, g[None, :], w_uv.astype(BF16), gv[None, :], ws, bias, w_o.astype(BF16))


def _mlp_kernel(*refs, has_attn, has_final):
    refs = list(refs)
    x_ref = refs.pop(0)
    attn_ref = refs.pop(0) if has_attn else None
    wo_ref = refs.pop(0) if has_attn else None
    g_ref, wup_ref, wdn_ref = refs.pop(0), refs.pop(0), refs.pop(0)
    fg_ref = refs.pop(0) if has_final else None
    out_ref, h_ref, xn_ref, acc_ref = refs
    j = pl.program_id(1)

    @pl.when(j == 0)
    def _():
        h = x_ref[...]
        if has_attn:
            h = h + _dot(attn_ref[...], wo_ref[...])
        h_ref[...] = h
        xn_ref[...] = _rms(h, g_ref[...]).astype(BF16)
        acc_ref[...] = jnp.zeros_like(acc_ref)

    a = _dot(xn_ref[...], wup_ref[...])
    a = jnp.square(jnp.maximum(a, 0.0)).astype(BF16)
    acc_ref[...] += _dot(a, wdn_ref[...])

    @pl.when(j == pl.num_programs(1) - 1)
    def _():
        h = h_ref[...] + acc_ref[...]
        if has_final:
            h = _rms(h, fg_ref[...])
        out_ref[...] = h


def _mlp_layer(h, g, w_up, w_down, *, attn=None, w_o=None, final_g=None):
    m = h.shape[0]
    tm = min(m, 1024)
    tf = 1024
    assert m % tm == 0 and D_FF % tf == 0
    has_attn, has_final = attn is not None, final_g is not None
    row = pl.BlockSpec((tm, D_MODEL), lambda i, j: (i, 0))
    vec = pl.BlockSpec((1, D_MODEL), lambda i, j: (0, 0))
    args, specs = [h], [row]
    if has_attn:
        args += [attn, w_o.astype(BF16)]
        specs += [row, pl.BlockSpec((D_MODEL, D_MODEL), lambda i, j: (0, 0))]
    args += [g[None, :], w_up.astype(BF16), w_down.astype(BF16)]
    specs += [vec, pl.BlockSpec((D_MODEL, tf), lambda i, j: (0, j)), pl.BlockSpec((tf, D_MODEL), lambda i, j: (j, 0))]
    if has_final:
        args.append(final_g[None, :])
        specs.append(vec)
    return pl.pallas_call(
        functools.partial(_mlp_kernel, has_attn=has_attn, has_final=has_final),
        grid=(m // tm, D_FF // tf),
        in_specs=specs, out_specs=row,
        out_shape=jax.ShapeDtypeStruct((m, D_MODEL), F32),
        scratch_shapes=[pltpu.VMEM((tm, D_MODEL), F32), pltpu.VMEM((tm, D_MODEL), BF16),
                        pltpu.VMEM((tm, D_MODEL), F32)],
        compiler_params=_params(("parallel", "arbitrary")),
        name="mlp",
    )(*args)


def _norm_matmul_kernel(x_ref, g_ref, w_ref, *out_refs):
    xn = _rms(x_ref[...], g_ref[...]).astype(BF16)
    y = _dot(xn, w_ref[...])
    wd = y.shape[1] // len(out_refs)
    for k, o_ref in enumerate(out_refs):
        o_ref[...] = y[:, k * wd:(k + 1) * wd]


def _norm_matmul(h, g, w, n_out):
    m, n = h.shape[0], w.shape[1]
    tm = min(m, 1024)
    wd = n // n_out
    assert m % tm == 0 and n % n_out == 0 and wd % LANES == 0
    outs = pl.pallas_call(
        _norm_matmul_kernel,
        grid=(m // tm,),
        in_specs=[pl.BlockSpec((tm, D_MODEL), lambda i: (i, 0)), pl.BlockSpec((1, D_MODEL), lambda i: (0, 0)),
                  pl.BlockSpec((D_MODEL, n), lambda i: (0, 0))],
        out_specs=tuple(pl.BlockSpec((tm, wd), lambda i: (i, 0)) for _ in range(n_out)),
        out_shape=tuple(jax.ShapeDtypeStruct((m, wd), F32) for _ in range(n_out)),
        compiler_params=_params(("parallel",)),
        name="norm_matmul",
    )(h, g[None, :], w.astype(BF16))
    return outs


def _compress_kernel(pt_ref, *refs, npg, channel_major):
    page_refs = refs[:npg]
    w1_ref, pe_ref, w2_ref, out_ref, z_ref, aprev_ref = refs[npg:npg + 6]
    j = pl.program_id(1)
    grp = PAGE_SIZE // CMP_STRIDE
    sec = grp * npg
    rows = N_KV * sec

    @pl.when(j == 0)
    def _():
        aprev_ref[...] = jnp.zeros_like(aprev_ref)

    ncol = 2 * KV_COLS // LANES
    if channel_major:
        tr_ref = refs[npg + 6]
        for t in range(npg):
            for c in range(ncol):
                tr_ref[t * ncol + c] = page_refs[t][0, c * LANES:(c + 1) * LANES, :].T

        def rows_of(t, c, p):
            return tr_ref[t * ncol + c, pl.ds(p, grp, stride=CMP_STRIDE), :]
    else:
        def rows_of(t, c, p):
            return page_refs[t][0, pl.ds(ncol * p + c, grp, stride=ncol * CMP_STRIDE), :]

    for t in range(npg):
        for q in range(CMP_STRIDE // 2):
            for c in range(ncol):
                xa = rows_of(t, c, 2 * q)
                xb = rows_of(t, c, 2 * q + 1)
                kv = c // (ncol // 2)
                for half in range(LANES // HEAD_DIM):
                    g = (c % (ncol // 2)) * (LANES // HEAD_DIM) + half
                    hs = slice(half * HEAD_DIM, (half + 1) * HEAD_DIM)
                    piece = jnp.concatenate([xa[:, hs], xb[:, hs]], axis=1)
                    z_ref[kv, g * sec + t * grp:g * sec + (t + 1) * grp, q * LANES:(q + 1) * LANES] = piece

    rid = lax.broadcasted_iota(jnp.int32, (rows, CMP_HIDDEN), 0) % sec
    for kv in range(2):
        z = z_ref[kv]
        za = (z + pe_ref[kv, 0:1, :]).astype(BF16)
        zb = (z + pe_ref[kv, 1:2, :]).astype(BF16)
        a = _dot(za, w1_ref[kv, :, :CMP_HIDDEN])
        b = _dot(zb, w1_ref[kv, :, CMP_HIDDEN:])
        a_prev = jnp.where(rid == 0, pltpu.roll(aprev_ref[kv], rows - (sec - 1), 0), pltpu.roll(a, 1, 0))
        aprev_ref[kv] = a
        hid = jax.nn.gelu(a_prev + b).astype(BF16)
        o = _dot(hid, w2_ref[kv])
        out_ref[0, kv] = jnp.concatenate([o[g * sec:(g + 1) * sec] for g in range(N_KV)], axis=1)


def _compress(pages, page_table, pe_k, w1_k, w2_k, pe_v, w1_v, w2_v, *, channel_major):
    n_seq, n_pages = page_table.shape
    npg = PAGES_PER_STEP
    assert n_pages % npg == 0
    grp = PAGE_SIZE // CMP_STRIDE
    half = CMP_STRIDE * HEAD_DIM

    def split(w1):
        return jnp.concatenate([w1[:half], w1[half:]], axis=1)

    w1 = jnp.stack([split(w1_k), split(w1_v)]).astype(BF16)
    pe = jnp.stack([pe_k.reshape(2, half), pe_v.reshape(2, half)])
    w2 = jnp.stack([w2_k, w2_v]).astype(BF16)

    ncol = 2 * KV_COLS // LANES
    scratch = [pltpu.VMEM((2, N_KV * grp * npg, half), F32), pltpu.VMEM((2, N_KV * grp * npg, CMP_HIDDEN), F32)]
    if channel_major:
        scratch.append(pltpu.VMEM((npg * ncol, PAGE_SIZE, LANES), F32))

    def page_spec(t):
        return pl.BlockSpec((1, ncol * PAGE_SIZE, LANES), lambda s, j, pt: (pt[s * n_pages + j * npg + t], 0, 0))

    const = lambda shape: pl.BlockSpec(shape, lambda s, j, pt: (0,) * len(shape))
    return pl.pallas_call(
        functools.partial(_compress_kernel, npg=npg, channel_major=channel_major),
        grid_spec=pltpu.PrefetchScalarGridSpec(
            num_scalar_prefetch=1,
            grid=(n_seq, n_pages // npg),
            in_specs=[page_spec(t) for t in range(npg)] + [const((2, half, 2 * CMP_HIDDEN)), const((2, 2, half)),
                                                           const((2, CMP_HIDDEN, HEAD_DIM))],
            out_specs=pl.BlockSpec((1, 2, grp * npg, KV_COLS), lambda s, j, pt: (s, 0, j, 0)),
            scratch_shapes=scratch),
        out_shape=jax.ShapeDtypeStruct((n_seq, 2, n_pages * grp, KV_COLS), F32),
        compiler_params=_params(("parallel", "arbitrary")),
        name="compress",
    )(page_table.reshape(-1), *([pages] * npg), w1, pe, w2)


def _selection_map(n_rows, n_blk, n_cols):
    i = np.arange(n_rows)[:, None] - 1
    j = np.arange(n_cols)[None, :]
    lo = np.maximum(i * CMP_STRIDE, j * SEL_LEN)
    hi = np.minimum(i * CMP_STRIDE + CMP_LEN, (j + 1) * SEL_LEN)
    m = np.maximum(hi - lo, 0) // CMP_STRIDE
    m = np.where((i >= 0) & (j < n_blk), m, 0)
    return jnp.asarray(m, dtype=BF16)


def _top_blocks_along_rows(score, forced, n_rows):
    ridx = lax.broadcasted_iota(jnp.int32, score.shape, 0)
    work = jnp.where(forced, -3e38, score)
    for _ in range(N_SEL - 3):
        mx = jnp.max(work, axis=0, keepdims=True)
        first = jnp.min(jnp.where(work == mx, ridx, n_rows), axis=0, keepdims=True)
        work = jnp.where(ridx == first, -3e38, work)
    return work < -1e38


def _query_columns(zq):
    qt = (zq * (SCALE * LOG2E)).T
    return jnp.concatenate([qt[r * HEAD_DIM:(r + 1) * HEAD_DIM] for r in range(Q_PER_KV)], axis=1).astype(BF16)


def _nsa_prompt_kernel(zq_ref, zg_ref, bg_ref, kc_ref, vct_ref, smapt_ref, ka_ref, vat_ref, kw_ref, vwt_ref,
                       out_ref, s_scr, p_scr, m_scr, acc_scr, mx_scr):
    n = pl.program_id(2)
    s0 = n * Q_BLOCK
    cols = Q_PER_KV * Q_BLOCK
    qst = _query_columns(zq_ref[...])
    i_col = lax.broadcasted_iota(jnp.int32, (1, cols), 1) & (Q_BLOCK - 1)
    qpos = s0 + i_col

    n_c = kc_ref.shape[2]
    sc = _dot(kc_ref[0, 0], qst)
    cidx = lax.broadcasted_iota(jnp.int32, (n_c, cols), 0)
    lim = jnp.maximum((qpos - (CMP_STRIDE - 1)) // CMP_STRIDE, 0)
    scm = jnp.where((cidx >= 1) & (cidx <= lim), sc, NEG_INF)
    mx_c = jnp.where(lim >= 1, jnp.max(scm, axis=0, keepdims=True), -NEG_INF)
    p = jnp.exp2(scm - mx_c)
    acc_c = _dot(vct_ref[0, 0], p.astype(BF16))
    l_c = acc_c[HEAD_DIM:HEAD_DIM + 1]
    o_c = acc_c[:HEAD_DIM] * jnp.where(l_c > 0.0, 1.0 / l_c, 0.0)
    l32 = jnp.sum(p, axis=0, keepdims=True)
    pn = p * jnp.where(l32 > 0.0, 1.0 / l32, 0.0)
    psum = pn[:, 0:Q_BLOCK]
    for r in range(1, Q_PER_KV):
        psum = psum + pn[:, r * Q_BLOCK:(r + 1) * Q_BLOCK]
    p_hi = psum.astype(BF16)
    p_lo = (psum - p_hi.astype(F32)).astype(BF16)
    imp_t = _dot(smapt_ref[...], p_hi) + _dot(smapt_ref[...], p_lo)

    wk = WINDOW + Q_BLOCK
    w0 = pl.multiple_of(s0, Q_BLOCK)
    flag_row = lax.broadcasted_iota(jnp.int32, (LANES - HEAD_DIM, cols), 0) == 0
    qw = jnp.concatenate([qst, jnp.where(flag_row, NEG_INF, 0.0).astype(BF16)], axis=0)
    sw = _dot(kw_ref[0, 0, pl.ds(w0, wk), :], qw)
    rr = lax.broadcasted_iota(jnp.int32, (Q_BLOCK, cols), 0)
    swm = jnp.concatenate([jnp.where(rr > i_col, sw[:Q_BLOCK], NEG_INF), sw[Q_BLOCK:WINDOW],
                           jnp.where(rr <= i_col, sw[WINDOW:], NEG_INF)], axis=0)
    pw = jnp.exp2(swm - jnp.max(swm, axis=0, keepdims=True)).astype(BF16)
    acc_w = _dot(vwt_ref[0, 0, :, pl.ds(w0, wk)], pw)
    o_w = acc_w[:HEAD_DIM] / acc_w[HEAD_DIM:HEAD_DIM + 1]

    nb = imp_t.shape[0]
    jt = lax.broadcasted_iota(jnp.int32, (nb, Q_BLOCK), 0)
    cur = (s0 + lax.broadcasted_iota(jnp.int32, (nb, Q_BLOCK), 1)) // SEL_LEN
    forced = (jt == 0) | (jt == cur) | (jt == cur - 1)
    score = jnp.where(forced, FORCE_SCORE, jnp.where(jt <= cur, imp_t, -1.0))
    sel = _top_blocks_along_rows(score, forced, nb)
    bias_t = jnp.where(sel & (score >= 0.0), 0.0, NEG_INF).astype(BF16)
    qaug_t = jnp.concatenate([jnp.concatenate([bias_t] * Q_PER_KV, axis=1), qst,
                              jnp.zeros((KA_COLS - nb - HEAD_DIM, cols), BF16)], axis=0)

    def scores(t):
        k0 = pl.multiple_of(t * SEL_TILE, SEL_TILE)
        return _dot(ka_ref[0, 0, pl.ds(k0, SEL_TILE), :], qaug_t)

    def weighted(pt, t):
        k0 = pl.multiple_of(t * SEL_TILE, SEL_TILE)
        return _dot(vat_ref[0, 0, :, pl.ds(k0, SEL_TILE)], pt)

    def put_scores(slot, t):
        s = scores(t)
        s_scr[slot] = s
        mx_scr[slot] = jnp.max(s, axis=0, keepdims=True)

    def stage(t, cur, causal, prefetch):
        prv = 1 - cur
        pv = weighted(p_scr[prv], jnp.maximum(t - 1, 0))
        s = s_scr[cur]
        if causal:
            kpos = t * SEL_TILE + lax.broadcasted_iota(jnp.int32, (SEL_TILE, cols), 0)
            s = jnp.where(kpos <= qpos, s, NEG_INF)
            mx = jnp.max(s, axis=0, keepdims=True)
        else:
            mx = mx_scr[cur]
        m_prev = m_scr[...]
        m_new = jnp.maximum(m_prev, mx)
        p_scr[cur] = jnp.exp2(s - m_new).astype(BF16)
        m_scr[...] = m_new
        acc_scr[...] = (acc_scr[...] + pv) * jnp.exp2(m_prev - m_new)
        if prefetch:
            put_scores(prv, t + 1)

    def trip(i, carry):
        stage(2 * i, 0, False, True)
        stage(2 * i + 1, 1, False, True)
        return carry

    def finish(last_slot, last_tile):
        acc_s = acc_scr[...] + weighted(p_scr[last_slot], last_tile)
        o_s = acc_s[:HEAD_DIM] / acc_s[HEAD_DIM:HEAD_DIM + 1]
        gate_t = (1.0 / (1.0 + jnp.exp(-(zg_ref[...] + bg_ref[...])))).T
        outs = []
        for r in range(Q_PER_KV):
            cs = slice(r * Q_BLOCK, (r + 1) * Q_BLOCK)
            outs.append(gate_t[r:r + 1] * o_c[:, cs] + gate_t[Q_PER_KV + r:Q_PER_KV + r + 1] * o_s[:, cs]
                        + gate_t[2 * Q_PER_KV + r:2 * Q_PER_KV + r + 1] * o_w[:, cs])
        out_ref[...] = jnp.concatenate(outs, axis=0).T.astype(BF16)

    t_last = s0 // SEL_TILE
    n_pairs = t_last // 2
    put_scores(0, 0)
    p_scr[1] = jnp.zeros((SEL_TILE, cols), BF16)
    m_scr[...] = jnp.full((1, cols), NEG_INF, F32)
    acc_scr[...] = jnp.zeros((V_ROWS, cols), F32)
    lax.fori_loop(0, n_pairs, trip, 0)

    @pl.when(t_last % 2 == 1)
    def _():
        stage(2 * n_pairs, 0, False, True)
        stage(2 * n_pairs + 1, 1, True, False)
        finish(1, 2 * n_pairs + 1)

    @pl.when(t_last % 2 == 0)
    def _():
        stage(2 * n_pairs, 0, True, False)
        finish(0, 2 * n_pairs)


def _kv_layouts(rows, b, t, left_pad):
    k = rows[:, :KV_COLS].reshape(b, t, N_KV, HEAD_DIM).astype(BF16)
    v = rows[:, KV_COLS:].reshape(b, t, N_KV, HEAD_DIM).astype(BF16)
    kk = jnp.transpose(k, (0, 2, 1, 3))
    vt = jnp.concatenate([jnp.transpose(v, (0, 2, 3, 1)), jnp.ones((b, N_KV, 1, t), BF16),
                          jnp.zeros((b, N_KV, V_ROWS - HEAD_DIM - 1, t), BF16)], axis=2)
    if left_pad:
        kk = jnp.pad(kk, ((0, 0), (0, 0), (left_pad, 0), (0, LANES - HEAD_DIM)))
        flag = (np.arange(left_pad + t)[:, None] < left_pad) & (np.arange(LANES)[None, :] == HEAD_DIM)
        kk = kk + jnp.asarray(flag, BF16)
        vt = jnp.pad(vt, ((0, 0), (0, 0), (0, 0), (left_pad, 0)))
    return kk, vt


def _nsa_prompt(z, bg_pad, kvc, sel_rows, win_rows, b, t):
    nq = t // Q_BLOCK
    n_c = kvc.shape[2]
    n_blk = t // SEL_LEN
    assert t % SEL_TILE == 0 and SEL_TILE % Q_BLOCK == 0 and N_SEL <= n_blk <= LANES
    cols = Q_PER_KV * Q_BLOCK
    kc = jnp.transpose(kvc[:, 0].reshape(b, n_c, N_KV, HEAD_DIM).astype(BF16), (0, 2, 1, 3))
    vc = kvc[:, 1].reshape(b, n_c, N_KV, HEAD_DIM).astype(BF16)
    vct = jnp.concatenate([jnp.transpose(vc, (0, 2, 3, 1)), jnp.ones((b, N_KV, 1, n_c), BF16),
                           jnp.zeros((b, N_KV, V_ROWS - HEAD_DIM - 1, n_c), BF16)], axis=2)
    smapt = _selection_map(n_c, n_blk, LANES).T
    ks, vat = _kv_layouts(sel_rows, b, t, 0)
    key_in_blk = np.arange(t)[:, None] // SEL_LEN == np.arange(LANES)[None, :]
    ka = jnp.concatenate([jnp.broadcast_to(jnp.asarray(key_in_blk, BF16), (b, N_KV, t, LANES)), ks,
                          jnp.zeros((b, N_KV, t, KA_COLS - LANES - HEAD_DIM), BF16)], axis=3)
    kw, vwt = _kv_layouts(win_rows, b, t, WINDOW)
    per_bg = lambda shape: pl.BlockSpec((1, 1) + shape, lambda bb, g, n: (bb, g, 0, 0))
    return pl.pallas_call(
        _nsa_prompt_kernel,
        grid=(b, N_KV, nq),
        in_specs=[pl.BlockSpec((Q_BLOCK, Q_PER_KV * HEAD_DIM), lambda bb, g, n: (bb * nq + n, g)),
                  pl.BlockSpec((Q_BLOCK, LANES), lambda bb, g, n: (bb * nq + n, D_MODEL // LANES + g)),
                  pl.BlockSpec((1, LANES), lambda bb, g, n: (0, g)),
                  per_bg((n_c, HEAD_DIM)), per_bg((V_ROWS, n_c)),
                  pl.BlockSpec((LANES, n_c), lambda bb, g, n: (0, 0)),
                  per_bg((t, KA_COLS)), per_bg((V_ROWS, t)),
                  per_bg((WINDOW + t, LANES)), per_bg((V_ROWS, WINDOW + t))],
        out_specs=pl.BlockSpec((Q_BLOCK, Q_PER_KV * HEAD_DIM), lambda bb, g, n: (bb * nq + n, g)),
        out_shape=jax.ShapeDtypeStruct((b * t, D_MODEL), BF16),
        scratch_shapes=[pltpu.VMEM((2, SEL_TILE, cols), F32), pltpu.VMEM((2, SEL_TILE, cols), BF16),
                        pltpu.VMEM((1, cols), F32), pltpu.VMEM((V_ROWS, cols), F32),
                        pltpu.VMEM((2, 1, cols), F32)],
        compiler_params=_params(("parallel", "parallel", "arbitrary")),
        name="nsa_prompt",
    )(z, z, bg_pad, kc, vct, smapt, ka, vat, kw, vwt)


SROWS = 32
SBLK = 256


def _samp_cmpwin_kernel(qbd_ref, kvc_ref, smap_ref, win_ref, wnew_ref, oc_ref, ow_ref, imp_ref, *, qpos, group):
    for u in range(group):
        _samp_cmpwin_one(u, qbd_ref, kvc_ref, smap_ref, win_ref, wnew_ref, oc_ref, ow_ref, imp_ref, qpos)


def _samp_cmpwin_one(u, qbd_ref, kvc_ref, smap_ref, win_ref, wnew_ref, oc_ref, ow_ref, imp_ref, qpos):
    qf = qbd_ref[u] * SCALE
    qb = qf.astype(BF16)
    kc = kvc_ref[u, 0].astype(BF16)
    vc = kvc_ref[u, 1].astype(BF16)
    n_c = kc.shape[0]
    sc = _dot_nt(qb, kc)
    cidx = lax.broadcasted_iota(jnp.int32, (SROWS, n_c), 1)
    mc = (cidx >= 1) & (cidx * CMP_STRIDE + (CMP_STRIDE - 1) <= qpos)
    scm = jnp.where(mc, sc, NEG_INF)
    p = jnp.where(mc, jnp.exp(scm - jnp.max(scm, axis=1, keepdims=True)), 0.0)
    l = jnp.sum(p, axis=1, keepdims=True)
    inv = jnp.where(l > 0.0, 1.0 / l, 0.0)
    oc_ref[u] = _dot(p.astype(BF16), vc) * inv
    pn = p * inv
    psum = pn[0:8] + pn[8:16] + pn[16:24] + pn[24:32]
    p_hi = psum.astype(BF16)
    p_lo = (psum - p_hi.astype(F32)).astype(BF16)
    imp_ref[u] = _dot(p_hi, smap_ref[...]) + _dot(p_lo, smap_ref[...])
    win = win_ref[u]
    n_w = win.shape[1]
    sw = _dot(qb, win[:KV_COLS].astype(BF16))
    widx = lax.broadcasted_iota(jnp.int32, (SROWS, n_w), 1)
    dist = n_w - widx
    mw = (dist < WINDOW) & (qpos - dist >= 0)
    swm = jnp.where(mw, sw, NEG_INF)
    wnew = wnew_ref[u]
    s_new = jnp.sum(qb.astype(F32) * wnew[:, :KV_COLS].astype(BF16).astype(F32), axis=1, keepdims=True)
    m = jnp.maximum(jnp.max(swm, axis=1, keepdims=True), s_new)
    pw = jnp.exp(swm - m)
    p_new = jnp.exp(s_new - m)
    lw = jnp.sum(pw, axis=1, keepdims=True) + p_new
    ow = _dot_nt(pw.astype(BF16), win[KV_COLS:].astype(BF16)) + p_new * wnew[:, KV_COLS:]
    ow_ref[u] = ow / lw


def _samp_topk_kernel(imp_ref, bias_ref, *, cur):
    imp = imp_ref[...]
    jidx = lax.broadcasted_iota(jnp.int32, imp.shape, 1)
    forced = (jidx == 0) | (jidx == cur) | (jidx == cur - 1)
    score = jnp.where(forced, FORCE_SCORE, jnp.where(jidx <= cur, imp, -1.0))
    work = score
    sel = jnp.zeros(imp.shape, F32)
    for _ in range(N_SEL):
        mx = jnp.max(work, axis=1, keepdims=True)
        first = jnp.min(jnp.where(work == mx, jidx, SBLK), axis=1, keepdims=True)
        hit = jidx == first
        sel = jnp.where(hit, 1.0, sel)
        work = jnp.where(hit, -3e38, work)
    bias_ref[...] = jnp.where((sel > 0.0) & (score >= 0.0), 0.0, NEG_INF)


def _samp_sel_kernel(pt_ref, *refs, npg):
    page_refs = refs[:npg]
    (qbd_ref, bias_ref, onehot_ref, knew_ref, oc_ref, ow_ref, zg_ref, bg_ref, out_ref) = refs[npg:]
    n_past_blk = npg * PAGE_SIZE // SEL_LEN
    qb = (qbd_ref[0] * SCALE).astype(BF16)
    bias8 = bias_ref[0][:, :n_past_blk].astype(BF16)
    bias32 = jnp.concatenate([bias8] * Q_PER_KV, axis=0)
    s = jnp.concatenate([_dot(qb, page_refs[t][0, :KV_COLS, :].astype(BF16)) for t in range(npg)], axis=1)
    s = s + _dot(bias32, onehot_ref[...])
    knew = knew_ref[0]
    s_new = jnp.sum(qb.astype(F32) * knew[:, :KV_COLS].astype(BF16).astype(F32), axis=1, keepdims=True)
    m = jnp.maximum(jnp.max(s, axis=1, keepdims=True), s_new)
    p = jnp.exp(s - m)
    p_new = jnp.exp(s_new - m)
    pv = p_new * knew[:, KV_COLS:]
    for t in range(npg):
        pv = pv + _dot_nt(p[:, t * PAGE_SIZE:(t + 1) * PAGE_SIZE].astype(BF16),
                          page_refs[t][0, KV_COLS:, :].astype(BF16))
    os_ = pv / (jnp.sum(p, axis=1, keepdims=True) + p_new)
    gate = 1.0 / (1.0 + jnp.exp(-(zg_ref[0] + bg_ref[0])))
    o = gate[:, 0:1] * oc_ref[0] + gate[:, 1:2] * os_ + gate[:, 2:3] * ow_ref[0]
    gsel = lax.broadcasted_iota(jnp.int32, (SROWS, HEAD_DIM), 0) & 7
    res = jnp.zeros((SROWS, HEAD_DIM), F32)
    for g in range(N_KV):
        res = res + jnp.where(gsel == g, o[:, g * HEAD_DIM:(g + 1) * HEAD_DIM], 0.0)
    out_ref[0] = res


def _nsa_sample(z, b_g, kvc, cache_sel, page_table, sel_new, win_state, win_new, past_len):
    n_seq, n_pages = page_table.shape
    qpos = past_len
    n_blk = -(-(past_len + 1) // SEL_LEN)
    assert n_blk <= SBLK
    n_c = kvc.shape[2]
    n_w = win_state.shape[2]
    q4 = z[:, :D_MODEL].reshape(n_seq, N_KV, Q_PER_KV, HEAD_DIM)
    eye = jnp.eye(N_KV, dtype=F32)
    qbd = jnp.transpose(q4[:, :, :, None, :] * eye[None, :, None, :, None], (0, 2, 1, 3, 4))
    qbd = jnp.pad(qbd, ((0, 0), (0, 0), (0, 8 - N_KV), (0, 0), (0, 0))).reshape(n_seq, SROWS, KV_COLS)
    def gate_rows(a):
        n = a.shape[0]
        a = jnp.transpose(a.reshape(n, N_KV, Q_PER_KV, 3), (0, 2, 1, 3))
        return jnp.pad(a, ((0, 0), (0, 0), (0, 8 - N_KV), (0, LANES - 3))).reshape(n, SROWS, LANES)

    zg = gate_rows(z[:, D_MODEL:D_MODEL + 3 * N_HEADS])
    bg = gate_rows(b_g[None, :])
    smap = _selection_map(n_c, n_blk, SBLK)
    group = 4 if n_seq % 4 == 0 else 1
    per_seq = lambda shape: pl.BlockSpec((group,) + shape, lambda s: (s,) + (0,) * len(shape))
    o_c, o_w, imp = pl.pallas_call(
        functools.partial(_samp_cmpwin_kernel, qpos=qpos, group=group),
        grid=(n_seq // group,),
        in_specs=[per_seq((SROWS, KV_COLS)), per_seq((2, n_c, KV_COLS)),
                  pl.BlockSpec((n_c, SBLK), lambda s: (0, 0)), per_seq((2 * KV_COLS, n_w)),
                  per_seq((1, 2 * KV_COLS))],
        out_specs=(per_seq((SROWS, KV_COLS)), per_seq((SROWS, KV_COLS)), per_seq((8, SBLK))),
        out_shape=(jax.ShapeDtypeStruct((n_seq, SROWS, KV_COLS), F32),
                   jax.ShapeDtypeStruct((n_seq, SROWS, KV_COLS), F32),
                   jax.ShapeDtypeStruct((n_seq, 8, SBLK), F32)),
        compiler_params=_params(("parallel",)),
        name="nsa_sample_cmpwin",
    )(qbd, kvc, smap, win_state, win_new[:, None, :])
    bias = pl.pallas_call(
        functools.partial(_samp_topk_kernel, cur=qpos // SEL_LEN),
        out_shape=jax.ShapeDtypeStruct((n_seq * 8, SBLK), F32),
        name="nsa_sample_topk",
    )(imp.reshape(n_seq * 8, SBLK)).reshape(n_seq, 8, SBLK)

    n_past_blk = past_len // SEL_LEN
    keys = n_pages * PAGE_SIZE
    assert n_past_blk == n_blk - 1
    onehot = jnp.asarray(np.arange(keys)[None, :] // SEL_LEN == np.arange(n_past_blk)[:, None], BF16)

    def page_spec(t):
        return pl.BlockSpec((1, 2 * KV_COLS, PAGE_SIZE), lambda s, pt: (pt[s * n_pages + t], 0, 0))

    seq_blk = lambda shape: pl.BlockSpec((1,) + shape, lambda s, pt: (s,) + (0,) * len(shape))
    out = pl.pallas_call(
        functools.partial(_samp_sel_kernel, npg=n_pages),
        grid_spec=pltpu.PrefetchScalarGridSpec(
            num_scalar_prefetch=1,
            grid=(n_seq,),
            in_specs=[page_spec(t) for t in range(n_pages)] + [
                seq_blk((SROWS, KV_COLS)), seq_blk((8, SBLK)),
                pl.BlockSpec((n_past_blk, keys), lambda s, pt: (0, 0)), seq_blk((1, 2 * KV_COLS)),
                seq_blk((SROWS, KV_COLS)), seq_blk((SROWS, KV_COLS)), seq_blk((SROWS, LANES)),
                pl.BlockSpec((1, SROWS, LANES), lambda s, pt: (0, 0, 0))],
            out_specs=seq_blk((SROWS, HEAD_DIM))),
        out_shape=jax.ShapeDtypeStruct((n_seq, SROWS, HEAD_DIM), F32),
        compiler_params=_params(("parallel",)),
        name="nsa_sample_sel",
    )(page_table.reshape(-1), *([cache_sel] * n_pages), qbd, bias, onehot, sel_new[:, None, :], o_c, o_w, zg, bg)
    out = out.reshape(n_seq, Q_PER_KV, 8, HEAD_DIM)[:, :, :N_KV]
    return jnp.transpose(out, (0, 2, 1, 3)).reshape(n_seq, D_MODEL).astype(BF16)


def kernel(x_prompt, x_sample, cache_cmp_kv, cache_sel_kv, state_win_kv, page_table, a_norm_g, a_w_uv, a_v_norm_g, a_w_s, a_b_s, a_w_o, kv_norm_g, w_kv, cmp_pe_k, cmp_w1_k, cmp_w2_k, cmp_pe_v, cmp_w1_v, cmp_w2_v, b_norm_g, b_w_qg, b_b_g, b_w_o, mlp_norm_g, mlp_w_up, mlp_w_down, final_norm_g):
    b, t, _ = x_prompt.shape
    n_seq, dec_seq, _ = x_sample.shape
    n_pool, page, _, _, _ = cache_cmp_kv.shape
    n_pages = page_table.shape[1]
    past_len = n_pages * page
    win_buf = state_win_kv.shape[1]
    depth = mlp_norm_g.shape[0]
    n_a = a_norm_g.shape[0]
    assert dec_seq == 1 and page == PAGE_SIZE and t % PAGE_SIZE == 0
    cmp_w = (cmp_pe_k, cmp_w1_k, cmp_w2_k, cmp_pe_v, cmp_w1_v, cmp_w2_v)
    kv_shape = (2, N_KV, HEAD_DIM)
    gate_w = 3 * N_HEADS

    hp = x_prompt.reshape(b * t, D_MODEL)
    hs = x_sample.reshape(n_seq, D_MODEL)
    a_v_rows = []
    attn_p = attn_s = None
    for layer in range(depth):
        last = layer == depth - 1
        fin = final_norm_g if last else None
        if layer < n_a:
            i = layer
            hp = _gmlp_layer(hp, a_norm_g[i], a_w_uv[i], a_v_norm_g[i], a_w_s[i], a_b_s[i], a_w_o[i], chunked=True)
            hs, v_s = _gmlp_layer(hs, a_norm_g[i], a_w_uv[i], a_v_norm_g[i], a_w_s[i], a_b_s[i], a_w_o[i],
                                  chunked=False)
            a_v_rows.append(v_s.reshape(n_seq, 1, D_MODEL))
            hp = _mlp_layer(hp, mlp_norm_g[layer], mlp_w_up[layer], mlp_w_down[layer], final_g=fin)
            hs = _mlp_layer(hs, mlp_norm_g[layer], mlp_w_up[layer], mlp_w_down[layer], final_g=fin)
            continue
        if layer == n_a:
            cmp_p, sel_p, win_p = _norm_matmul(hp, kv_norm_g, w_kv, 3)
            cmp_s, sel_s, win_s = _norm_matmul(hs, kv_norm_g, w_kv, 3)
            ncol = 2 * KV_COLS // LANES
            kvc_p = _compress(cmp_p.reshape(b * t // PAGE_SIZE, ncol * PAGE_SIZE, LANES),
                              jnp.arange(b * t // PAGE_SIZE, dtype=jnp.int32).reshape(b, t // PAGE_SIZE), *cmp_w,
                              channel_major=False)

            def channel_major(a):
                return jnp.transpose(a, (0, 2, 3, 4, 1)).reshape(a.shape[0], 2 * KV_COLS, a.shape[1])

            kvc_s = _compress(channel_major(cache_cmp_kv), page_table, *cmp_w, channel_major=True)
            cache_sel = channel_major(cache_sel_kv)
            win_state = channel_major(state_win_kv)
        j = layer - n_a
        wg = b_w_qg[j][:, D_MODEL:].reshape(D_MODEL, N_KV, Q_PER_KV, 3)
        wg = jnp.transpose(wg, (0, 1, 3, 2)).reshape(D_MODEL, N_KV, 3 * Q_PER_KV)
        wg = jnp.pad(wg, ((0, 0), (0, 0), (0, LANES - 3 * Q_PER_KV))).reshape(D_MODEL, N_KV * LANES)
        bg = jnp.transpose(b_b_g[j].reshape(N_KV, Q_PER_KV, 3), (0, 2, 1)).reshape(N_KV, 3 * Q_PER_KV)
        bg = jnp.pad(bg, ((0, 0), (0, LANES - 3 * Q_PER_KV))).reshape(1, N_KV * LANES)
        (z_p,) = _norm_matmul(hp, b_norm_g[j], jnp.concatenate([b_w_qg[j][:, :D_MODEL], wg], axis=1), 1)
        attn_p = _nsa_prompt(z_p, bg, kvc_p, sel_p, win_p, b, t)
        w_s_pad = jnp.pad(b_w_qg[j], ((0, 0), (0, LANES - gate_w % LANES)))
        (z_s,) = _norm_matmul(hs, b_norm_g[j], w_s_pad, 1)
        attn_s = _nsa_sample(z_s, b_b_g[j], kvc_s, cache_sel, page_table, sel_s, win_state, win_s, past_len)
        hp = _mlp_layer(hp, mlp_norm_g[layer], mlp_w_up[layer], mlp_w_down[layer], attn=attn_p, w_o=b_w_o[j],
                        final_g=fin)
        hs = _mlp_layer(hs, mlp_norm_g[layer], mlp_w_up[layer], mlp_w_down[layer], attn=attn_s, w_o=b_w_o[j],
                        final_g=fin)

    y_prompt = hp.reshape(b, t, D_MODEL)
    y_sample = hs.reshape(n_seq, 1, D_MODEL)
    nw = min(WINDOW, t)
    win_kv_prompt = win_p.reshape(b, t, *kv_shape)[:, t - nw:]
    win_kv_sample = jnp.concatenate([state_win_kv, win_s.reshape(n_seq, 1, *kv_shape)], axis=1)[:, dec_seq:]
    return (y_prompt, y_sample, cmp_p.reshape(b, t, *kv_shape), sel_p.reshape(b, t, *kv_shape), win_kv_prompt,
            cmp_s.reshape(n_seq, 1, *kv_shape), sel_s.reshape(n_seq, 1, *kv_shape), win_kv_sample,
            jnp.stack(a_v_rows, axis=0))
```

```python
import functools

import numpy as np
import jax
import jax.numpy as jnp
from jax import lax
from jax.experimental import pallas as pl
from jax.experimental.pallas import tpu as pltpu

F32 = jnp.float32
BF16 = jnp.bfloat16

D_MODEL = 1024
D_FF = 4 * D_MODEL
CHUNK = 128
A_GROUP_DIM = 128
A_GROUPS = D_MODEL // A_GROUP_DIM
HEAD_DIM = 64
N_HEADS = D_MODEL // HEAD_DIM
N_KV = 4
Q_PER_KV = N_HEADS // N_KV
KV_COLS = N_KV * HEAD_DIM
CMP_STRIDE = 16
CMP_LEN = 2 * CMP_STRIDE
CMP_HIDDEN = 4 * HEAD_DIM
SEL_LEN = 64
N_SEL = 16
WINDOW = 512
Q_BLOCK = 256
PAGE_SIZE = 128
EPS = 1e-6
NEG_INF = -1e30
FORCE_SCORE = 1e6
SCALE = HEAD_DIM ** -0.5
LOG2E = 1.4426950408889634
LANES = 128
V_ROWS = 80
KA_COLS = 256
SEL_TILE = 512
PAGES_PER_STEP = 16
VMEM_LIMIT = 56 * 1024 * 1024


def _params(sem):
    return pltpu.CompilerParams(dimension_semantics=sem, vmem_limit_bytes=VMEM_LIMIT)


def _rms(x, g):
    return x * lax.rsqrt(jnp.mean(x * x, axis=-1, keepdims=True) + EPS) * g


def _dot(a, b):
    return jnp.dot(a, b, preferred_element_type=F32)


def _dot_nt(a, b):
    return lax.dot_general(a, b, (((1,), (1,)), ((), ())), preferred_element_type=F32)


def _gmlp_kernel(x_ref, g_ref, wuv_ref, gv_ref, ws_ref, bias_ref, wo_ref, out_ref, *rest, chunked, tm):
    x = x_ref[...]
    xn = _rms(x, g_ref[...]).astype(BF16)
    uv = _dot(xn, wuv_ref[...])
    u = uv[:, :D_MODEL]
    v = _rms(uv[:, D_MODEL:], gv_ref[...])
    if chunked:
        gated_ref = rest[0]
        row = lax.broadcasted_iota(jnp.int32, (CHUNK, CHUNK), 0)
        col = lax.broadcasted_iota(jnp.int32, (CHUNK, CHUNK), 1)
        for g in range(A_GROUPS):
            cs = slice(g * A_GROUP_DIM, (g + 1) * A_GROUP_DIM)
            w = jnp.where(row >= col, ws_ref[g], 0.0).astype(BF16)
            for c in range(tm // CHUNK):
                rs = slice(c * CHUNK, (c + 1) * CHUNK)
                s = _dot(w, v[rs, cs].astype(BF16)) + bias_ref[:, cs]
                gated_ref[rs, cs] = (u[rs, cs] * s).astype(BF16)
        gated = gated_ref[...]
    else:
        v_ref = rest[0]
        v_ref[...] = v
        gated = (u * (v * ws_ref[...] + bias_ref[...])).astype(BF16)
    out_ref[...] = x + _dot(gated, wo_ref[...])


def _gmlp_layer(h, g, w_uv, gv, w_s, b_s, w_o, *, chunked):
    m = h.shape[0]
    full = lambda shape: pl.BlockSpec(shape, lambda i: (0,) * len(shape))
    if chunked:
        tm = 512
        ws = w_s
        bias = jnp.repeat(b_s.T, A_GROUP_DIM, axis=1)
        ws_spec = full((A_GROUPS, CHUNK, CHUNK))
        bias_spec = full((CHUNK, D_MODEL))
        out_shape = jax.ShapeDtypeStruct((m, D_MODEL), F32)
        out_specs = pl.BlockSpec((tm, D_MODEL), lambda i: (i, 0))
        scratch = [pltpu.VMEM((tm, D_MODEL), BF16)]
    else:
        tm = m
        ws = jnp.repeat(w_s[:, 0, 0], A_GROUP_DIM)[None, :]
        bias = jnp.repeat(b_s[:, 0], A_GROUP_DIM)[None, :]
        ws_spec = full((1, D_MODEL))
        bias_spec = full((1, D_MODEL))
        out_shape = (jax.ShapeDtypeStruct((m, D_MODEL), F32), jax.ShapeDtypeStruct((m, D_MODEL), F32))
        out_specs = (pl.BlockSpec((tm, D_MODEL), lambda i: (i, 0)), pl.BlockSpec((tm, D_MODEL), lambda i: (i, 0)))
        scratch = []
    assert m % tm == 0
    return pl.pallas_call(
        functools.partial(_gmlp_kernel, chunked=chunked, tm=tm),
        grid=(m // tm,),
        in_specs=[pl.BlockSpec((tm, D_MODEL), lambda i: (i, 0)), full((1, D_MODEL)),
                  full((D_MODEL, 2 * D_MODEL)), full((1, D_MODEL)), ws_spec, bias_spec,
                  full((D_MODEL, D_MODEL))],
        out_specs=out_specs, out_shape=out_shape, scratch_shapes=scratch,
        compiler_params=_params(("parallel",)),
        name="gmlp_prompt" if chunked else "gmlp_sample",
    )(h, g[None, :], w_uv.astype(BF16), gv[None, :], ws, bias, w_o.astype(BF16))


def _mlp_kernel(*refs, has_attn, has_final):
    refs = list(refs)
    x_ref = refs.pop(0)
    attn_ref = refs.pop(0) if has_attn else None
    wo_ref = refs.pop(0) if has_attn else None
    g_ref, wup_ref, wdn_ref = refs.pop(0), refs.pop(0), refs.pop(0)
    fg_ref = refs.pop(0) if has_final else None
    out_ref, h_ref, xn_ref, acc_ref = refs
    j = pl.program_id(1)

    @pl.when(j == 0)
    def _():
        h = x_ref[...]
        if has_attn:
            h = h + _dot(attn_ref[...], wo_ref[...])
        h_ref[...] = h
        xn_ref[...] = _rms(h, g_ref[...]).astype(BF16)
        acc_ref[...] = jnp.zeros_like(acc_ref)

    a = _dot(xn_ref[...], wup_ref[...])
    a = jnp.square(jnp.maximum(a, 0.0)).astype(BF16)
    acc_ref[...] += _dot(a, wdn_ref[...])

    @pl.when(j == pl.num_programs(1) - 1)
    def _():
        h = h_ref[...] + acc_ref[...]
        if has_final:
            h = _rms(h, fg_ref[...])
        out_ref[...] = h


def _mlp_resident_kernel(*refs, has_attn, has_final, tf):
    refs = list(refs)
    x_ref = refs.pop(0)
    attn_ref = refs.pop(0) if has_attn else None
    wo_ref = refs.pop(0) if has_attn else None
    g_ref, wup_ref, wdn_ref = refs.pop(0), refs.pop(0), refs.pop(0)
    fg_ref = refs.pop(0) if has_final else None
    (out_ref,) = refs
    h = x_ref[...]
    if has_attn:
        h = h + _dot(attn_ref[...], wo_ref[...])
    xn = _rms(h, g_ref[...]).astype(BF16)
    acc = None
    for c in range(D_FF // tf):
        a = _dot(xn, wup_ref[:, c * tf:(c + 1) * tf])
        a = jnp.square(jnp.maximum(a, 0.0)).astype(BF16)
        part = _dot(a, wdn_ref[c * tf:(c + 1) * tf, :])
        acc = part if acc is None else acc + part
    h = h + acc
    if has_final:
        h = _rms(h, fg_ref[...])
    out_ref[...] = h


def _mlp_layer(h, g, w_up, w_down, *, attn=None, w_o=None, final_g=None):
    m = h.shape[0]
    has_attn, has_final = attn is not None, final_g is not None
    if m >= 512:
        tm, tf = 512, 1024
        assert m % tm == 0 and D_FF % tf == 0
        row = pl.BlockSpec((tm, D_MODEL), lambda i: (i, 0))
        once = lambda shape: pl.BlockSpec(shape, lambda i: (0,) * len(shape), pipeline_mode=pl.Buffered(1))
        args, specs = [h], [row]
        if has_attn:
            args += [attn, w_o.astype(BF16)]
            specs += [row, once((D_MODEL, D_MODEL))]
        args += [g[None, :], w_up.astype(BF16), w_down.astype(BF16)]
        specs += [once((1, D_MODEL)), once((D_MODEL, D_FF)), once((D_FF, D_MODEL))]
        if has_final:
            args.append(final_g[None, :])
            specs.append(once((1, D_MODEL)))
        return pl.pallas_call(
            functools.partial(_mlp_resident_kernel, has_attn=has_attn, has_final=has_final, tf=tf),
            grid=(m // tm,), in_specs=specs, out_specs=row,
            out_shape=jax.ShapeDtypeStruct((m, D_MODEL), F32),
            compiler_params=_params(("parallel",)),
            name="mlp_resident",
        )(*args)
    tm = min(m, 1024)
    tf = 1024
    assert m % tm == 0 and D_FF % tf == 0
    row = pl.BlockSpec((tm, D_MODEL), lambda i, j: (i, 0))
    vec = pl.BlockSpec((1, D_MODEL), lambda i, j: (0, 0))
    args, specs = [h], [row]
    if has_attn:
        args += [attn, w_o.astype(BF16)]
        specs += [row, pl.BlockSpec((D_MODEL, D_MODEL), lambda i, j: (0, 0))]
    args += [g[None, :], w_up.astype(BF16), w_down.astype(BF16)]
    specs += [vec, pl.BlockSpec((D_MODEL, tf), lambda i, j: (0, j)), pl.BlockSpec((tf, D_MODEL), lambda i, j: (j, 0))]
    if has_final:
        args.append(final_g[None, :])
        specs.append(vec)
    return pl.pallas_call(
        functools.partial(_mlp_kernel, has_attn=has_attn, has_final=has_final),
        grid=(m // tm, D_FF // tf),
        in_specs=specs, out_specs=row,
        out_shape=jax.ShapeDtypeStruct((m, D_MODEL), F32),
        scratch_shapes=[pltpu.VMEM((tm, D_MODEL), F32), pltpu.VMEM((tm, D_MODEL), BF16),
                        pltpu.VMEM((tm, D_MODEL), F32)],
        compiler_params=_params(("parallel", "arbitrary")),
        name="mlp",
    )(*args)


def _norm_matmul_kernel(x_ref, g_ref, w_ref, *out_refs):
    xn = _rms(x_ref[...], g_ref[...]).astype(BF16)
    y = _dot(xn, w_ref[...])
    wd = y.shape[1] // len(out_refs)
    for k, o_ref in enumerate(out_refs):
        o_ref[...] = y[:, k * wd:(k + 1) * wd]


def _norm_matmul(h, g, w, n_out):
    m, n = h.shape[0], w.shape[1]
    tm = min(m, 1024)
    wd = n // n_out
    assert m % tm == 0 and n % n_out == 0 and wd % LANES == 0
    outs = pl.pallas_call(
        _norm_matmul_kernel,
        grid=(m // tm,),
        in_specs=[pl.BlockSpec((tm, D_MODEL), lambda i: (i, 0)), pl.BlockSpec((1, D_MODEL), lambda i: (0, 0)),
                  pl.BlockSpec((D_MODEL, n), lambda i: (0, 0))],
        out_specs=tuple(pl.BlockSpec((tm, wd), lambda i: (i, 0)) for _ in range(n_out)),
        out_shape=tuple(jax.ShapeDtypeStruct((m, wd), F32) for _ in range(n_out)),
        compiler_params=_params(("parallel",)),
        name="norm_matmul",
    )(h, g[None, :], w.astype(BF16))
    return outs


def _compress_kernel(pt_ref, *refs, npg, channel_major):
    page_refs = refs[:npg]
    w1_ref, pe_ref, w2_ref, out_ref, z_ref, aprev_ref = refs[npg:npg + 6]
    j = pl.program_id(1)
    grp = PAGE_SIZE // CMP_STRIDE
    sec = grp * npg
    rows = N_KV * sec

    @pl.when(j == 0)
    def _():
        aprev_ref[...] = jnp.zeros_like(aprev_ref)

    ncol = 2 * KV_COLS // LANES
    if channel_major:
        tr_ref = refs[npg + 6]
        for t in range(npg):
            for c in range(ncol):
                tr_ref[t * ncol + c] = page_refs[t][0, c * LANES:(c + 1) * LANES, :].T

        def rows_of(t, c, p):
            return tr_ref[t * ncol + c, pl.ds(p, grp, stride=CMP_STRIDE), :]
    else:
        def rows_of(t, c, p):
            return page_refs[t][0, pl.ds(ncol * p + c, grp, stride=ncol * CMP_STRIDE), :]

    for t in range(npg):
        for q in range(CMP_STRIDE // 2):
            for c in range(ncol):
                xa = rows_of(t, c, 2 * q)
                xb = rows_of(t, c, 2 * q + 1)
                kv = c // (ncol // 2)
                for half in range(LANES // HEAD_DIM):
                    g = (c % (ncol // 2)) * (LANES // HEAD_DIM) + half
                    hs = slice(half * HEAD_DIM, (half + 1) * HEAD_DIM)
                    piece = jnp.concatenate([xa[:, hs], xb[:, hs]], axis=1)
                    z_ref[kv, g * sec + t * grp:g * sec + (t + 1) * grp, q * LANES:(q + 1) * LANES] = piece

    rid = lax.broadcasted_iota(jnp.int32, (rows, CMP_HIDDEN), 0) % sec
    for kv in range(2):
        z = z_ref[kv]
        za = (z + pe_ref[kv, 0:1, :]).astype(BF16)
        zb = (z + pe_ref[kv, 1:2, :]).astype(BF16)
        a = _dot(za, w1_ref[kv, :, :CMP_HIDDEN])
        b = _dot(zb, w1_ref[kv, :, CMP_HIDDEN:])
        a_prev = jnp.where(rid == 0, pltpu.roll(aprev_ref[kv], rows - (sec - 1), 0), pltpu.roll(a, 1, 0))
        aprev_ref[kv] = a
        hid = jax.nn.gelu(a_prev + b).astype(BF16)
        o = _dot(hid, w2_ref[kv])
        out_ref[0, kv] = jnp.concatenate([o[g * sec:(g + 1) * sec] for g in range(N_KV)], axis=1)


def _compress(pages, page_table, pe_k, w1_k, w2_k, pe_v, w1_v, w2_v, *, channel_major):
    n_seq, n_pages = page_table.shape
    npg = PAGES_PER_STEP
    assert n_pages % npg == 0
    grp = PAGE_SIZE // CMP_STRIDE
    half = CMP_STRIDE * HEAD_DIM

    def split(w1):
        return jnp.concatenate([w1[:half], w1[half:]], axis=1)

    w1 = jnp.stack([split(w1_k), split(w1_v)]).astype(BF16)
    pe = jnp.stack([pe_k.reshape(2, half), pe_v.reshape(2, half)])
    w2 = jnp.stack([w2_k, w2_v]).astype(BF16)

    ncol = 2 * KV_COLS // LANES
    scratch = [pltpu.VMEM((2, N_KV * grp * npg, half), F32), pltpu.VMEM((2, N_KV * grp * npg, CMP_HIDDEN), F32)]
    if channel_major:
        scratch.append(pltpu.VMEM((npg * ncol, PAGE_SIZE, LANES), F32))

    def page_spec(t):
        return pl.BlockSpec((1, ncol * PAGE_SIZE, LANES), lambda s, j, pt: (pt[s * n_pages + j * npg + t], 0, 0))

    const = lambda shape: pl.BlockSpec(shape, lambda s, j, pt: (0,) * len(shape))
    return pl.pallas_call(
        functools.partial(_compress_kernel, npg=npg, channel_major=channel_major),
        grid_spec=pltpu.PrefetchScalarGridSpec(
            num_scalar_prefetch=1,
            grid=(n_seq, n_pages // npg),
            in_specs=[page_spec(t) for t in range(npg)] + [const((2, half, 2 * CMP_HIDDEN)), const((2, 2, half)),
                                                           const((2, CMP_HIDDEN, HEAD_DIM))],
            out_specs=pl.BlockSpec((1, 2, grp * npg, KV_COLS), lambda s, j, pt: (s, 0, j, 0)),
            scratch_shapes=scratch),
        out_shape=jax.ShapeDtypeStruct((n_seq, 2, n_pages * grp, KV_COLS), F32),
        compiler_params=_params(("parallel", "arbitrary")),
        name="compress",
    )(page_table.reshape(-1), *([pages] * npg), w1, pe, w2)


def _selection_map(n_rows, n_blk, n_cols):
    i = np.arange(n_rows)[:, None] - 1
    j = np.arange(n_cols)[None, :]
    lo = np.maximum(i * CMP_STRIDE, j * SEL_LEN)
    hi = np.minimum(i * CMP_STRIDE + CMP_LEN, (j + 1) * SEL_LEN)
    m = np.maximum(hi - lo, 0) // CMP_STRIDE
    m = np.where((i >= 0) & (j < n_blk), m, 0)
    return jnp.asarray(m, dtype=BF16)


def _top_blocks_along_rows(score, forced, n_rows):
    ridx = lax.broadcasted_iota(jnp.int32, score.shape, 0)
    work = jnp.where(forced, -3e38, score)
    for _ in range(N_SEL - 3):
        mx = jnp.max(work, axis=0, keepdims=True)
        first = jnp.min(jnp.where(work == mx, ridx, n_rows), axis=0, keepdims=True)
        work = jnp.where(ridx == first, -3e38, work)
    return work < -1e38


def _query_columns(zq):
    qt = (zq * (SCALE * LOG2E)).T
    return jnp.concatenate([qt[r * HEAD_DIM:(r + 1) * HEAD_DIM] for r in range(Q_PER_KV)], axis=1).astype(BF16)


def _nsa_prompt_kernel(zq_ref, zg_ref, bg_ref, kc_ref, vct_ref, smapt_ref, ka_ref, vat_ref, kw_ref, vwt_ref,
                       out_ref, s_scr, p_scr, m_scr, acc_scr, mx_scr):
    n = pl.program_id(2)
    s0 = n * Q_BLOCK
    cols = Q_PER_KV * Q_BLOCK
    qst = _query_columns(zq_ref[...])
    i_col = lax.broadcasted_iota(jnp.int32, (1, cols), 1) & (Q_BLOCK - 1)
    qpos = s0 + i_col

    n_c = kc_ref.shape[2]
    sc = _dot(kc_ref[0, 0], qst)
    cidx = lax.broadcasted_iota(jnp.int32, (n_c, cols), 0)
    lim = jnp.maximum((qpos - (CMP_STRIDE - 1)) // CMP_STRIDE, 0)
    scm = jnp.where((cidx >= 1) & (cidx <= lim), sc, NEG_INF)
    mx_c = jnp.where(lim >= 1, jnp.max(scm, axis=0, keepdims=True), -NEG_INF)
    p = jnp.exp2(scm - mx_c)
    acc_c = _dot(vct_ref[0, 0], p.astype(BF16))
    l_c = acc_c[HEAD_DIM:HEAD_DIM + 1]
    o_c = acc_c[:HEAD_DIM] * jnp.where(l_c > 0.0, 1.0 / l_c, 0.0)
    l32 = jnp.sum(p, axis=0, keepdims=True)
    pn = p * jnp.where(l32 > 0.0, 1.0 / l32, 0.0)
    psum = pn[:, 0:Q_BLOCK]
    for r in range(1, Q_PER_KV):
        psum = psum + pn[:, r * Q_BLOCK:(r + 1) * Q_BLOCK]
    p_hi = psum.astype(BF16)
    p_lo = (psum - p_hi.astype(F32)).astype(BF16)
    imp_t = _dot(smapt_ref[...], p_hi) + _dot(smapt_ref[...], p_lo)

    wk = WINDOW + Q_BLOCK
    w0 = pl.multiple_of(s0, Q_BLOCK)
    flag_row = lax.broadcasted_iota(jnp.int32, (LANES - HEAD_DIM, cols), 0) == 0
    qw = jnp.concatenate([qst, jnp.where(flag_row, NEG_INF, 0.0).astype(BF16)], axis=0)
    sw = _dot(kw_ref[0, 0, pl.ds(w0, wk), :], qw)
    rr = lax.broadcasted_iota(jnp.int32, (Q_BLOCK, cols), 0)
    swm = jnp.concatenate([jnp.where(rr > i_col, sw[:Q_BLOCK], NEG_INF), sw[Q_BLOCK:WINDOW],
                           jnp.where(rr <= i_col, sw[WINDOW:], NEG_INF)], axis=0)
    pw = jnp.exp2(swm - jnp.max(swm, axis=0, keepdims=True)).astype(BF16)
    acc_w = _dot(vwt_ref[0, 0, :, pl.ds(w0, wk)], pw)
    o_w = acc_w[:HEAD_DIM] / acc_w[HEAD_DIM:HEAD_DIM + 1]

    nb = imp_t.shape[0]
    jt = lax.broadcasted_iota(jnp.int32, (nb, Q_BLOCK), 0)
    cur = (s0 + lax.broadcasted_iota(jnp.int32, (nb, Q_BLOCK), 1)) // SEL_LEN
    forced = (jt == 0) | (jt == cur) | (jt == cur - 1)
    score = jnp.where(forced, FORCE_SCORE, jnp.where(jt <= cur, imp_t, -1.0))
    sel = _top_blocks_along_rows(score, forced, nb)
    bias_t = jnp.where(sel & (score >= 0.0), 0.0, NEG_INF).astype(BF16)
    qaug_t = jnp.concatenate([jnp.concatenate([bias_t] * Q_PER_KV, axis=1), qst,
                              jnp.zeros((KA_COLS - nb - HEAD_DIM, cols), BF16)], axis=0)

    def scores(t):
        k0 = pl.multiple_of(t * SEL_TILE, SEL_TILE)
        return _dot(ka_ref[0, 0, pl.ds(k0, SEL_TILE), :], qaug_t)

    def weighted(pt, t):
        k0 = pl.multiple_of(t * SEL_TILE, SEL_TILE)
        return _dot(vat_ref[0, 0, :, pl.ds(k0, SEL_TILE)], pt)

    def put_scores(slot, t):
        s = scores(t)
        s_scr[slot] = s
        mx_scr[slot] = jnp.max(s, axis=0, keepdims=True)

    def stage(t, cur, causal, prefetch):
        prv = 1 - cur
        pv = weighted(p_scr[prv], jnp.maximum(t - 1, 0))
        s = s_scr[cur]
        if causal:
            kpos = t * SEL_TILE + lax.broadcasted_iota(jnp.int32, (SEL_TILE, cols), 0)
            s = jnp.where(kpos <= qpos, s, NEG_INF)
            mx = jnp.max(s, axis=0, keepdims=True)
        else:
            mx = mx_scr[cur]
        m_prev = m_scr[...]
        m_new = jnp.maximum(m_prev, mx)
        p_scr[cur] = jnp.exp2(s - m_new).astype(BF16)
        m_scr[...] = m_new
        acc_scr[...] = (acc_scr[...] + pv) * jnp.exp2(m_prev - m_new)
        if prefetch:
            put_scores(prv, t + 1)

    def trip(i, carry):
        stage(2 * i, 0, False, True)
        stage(2 * i + 1, 1, False, True)
        return carry

    def finish(last_slot, last_tile):
        acc_s = acc_scr[...] + weighted(p_scr[last_slot], last_tile)
        o_s = acc_s[:HEAD_DIM] / acc_s[HEAD_DIM:HEAD_DIM + 1]
        gate_t = (1.0 / (1.0 + jnp.exp(-(zg_ref[...] + bg_ref[...])))).T
        outs = []
        for r in range(Q_PER_KV):
            cs = slice(r * Q_BLOCK, (r + 1) * Q_BLOCK)
            outs.append(gate_t[r:r + 1] * o_c[:, cs] + gate_t[Q_PER_KV + r:Q_PER_KV + r + 1] * o_s[:, cs]
                        + gate_t[2 * Q_PER_KV + r:2 * Q_PER_KV + r + 1] * o_w[:, cs])
        out_ref[...] = jnp.concatenate(outs, axis=0).T.astype(BF16)

    t_last = s0 // SEL_TILE
    n_pairs = t_last // 2
    put_scores(0, 0)
    p_scr[1] = jnp.zeros((SEL_TILE, cols), BF16)
    m_scr[...] = jnp.full((1, cols), NEG_INF, F32)
    acc_scr[...] = jnp.zeros((V_ROWS, cols), F32)
    lax.fori_loop(0, n_pairs, trip, 0)

    @pl.when(t_last % 2 == 1)
    def _():
        stage(2 * n_pairs, 0, False, True)
        stage(2 * n_pairs + 1, 1, True, False)
        finish(1, 2 * n_pairs + 1)

    @pl.when(t_last % 2 == 0)
    def _():
        stage(2 * n_pairs, 0, True, False)
        finish(0, 2 * n_pairs)


def _kv_layouts(rows, b, t, left_pad):
    k = rows[:, :KV_COLS].reshape(b, t, N_KV, HEAD_DIM).astype(BF16)
    v = rows[:, KV_COLS:].reshape(b, t, N_KV, HEAD_DIM).astype(BF16)
    kk = jnp.transpose(k, (0, 2, 1, 3))
    vt = jnp.concatenate([jnp.transpose(v, (0, 2, 3, 1)), jnp.ones((b, N_KV, 1, t), BF16),
                          jnp.zeros((b, N_KV, V_ROWS - HEAD_DIM - 1, t), BF16)], axis=2)
    if left_pad:
        kk = jnp.pad(kk, ((0, 0), (0, 0), (left_pad, 0), (0, LANES - HEAD_DIM)))
        flag = (np.arange(left_pad + t)[:, None] < left_pad) & (np.arange(LANES)[None, :] == HEAD_DIM)
        kk = kk + jnp.asarray(flag, BF16)
        vt = jnp.pad(vt, ((0, 0), (0, 0), (0, 0), (left_pad, 0)))
    return kk, vt


def _nsa_prompt(z, bg_pad, kvc, sel_rows, win_rows, b, t):
    nq = t // Q_BLOCK
    n_c = kvc.shape[2]
    n_blk = t // SEL_LEN
    assert t % SEL_TILE == 0 and SEL_TILE % Q_BLOCK == 0 and N_SEL <= n_blk <= LANES
    cols = Q_PER_KV * Q_BLOCK
    kc = jnp.transpose(kvc[:, 0].reshape(b, n_c, N_KV, HEAD_DIM).astype(BF16), (0, 2, 1, 3))
    vc = kvc[:, 1].reshape(b, n_c, N_KV, HEAD_DIM).astype(BF16)
    vct = jnp.concatenate([jnp.transpose(vc, (0, 2, 3, 1)), jnp.ones((b, N_KV, 1, n_c), BF16),
                           jnp.zeros((b, N_KV, V_ROWS - HEAD_DIM - 1, n_c), BF16)], axis=2)
    smapt = _selection_map(n_c, n_blk, LANES).T
    ks, vat = _kv_layouts(sel_rows, b, t, 0)
    key_in_blk = np.arange(t)[:, None] // SEL_LEN == np.arange(LANES)[None, :]
    ka = jnp.concatenate([jnp.broadcast_to(jnp.asarray(key_in_blk, BF16), (b, N_KV, t, LANES)), ks,
                          jnp.zeros((b, N_KV, t, KA_COLS - LANES - HEAD_DIM), BF16)], axis=3)
    kw, vwt = _kv_layouts(win_rows, b, t, WINDOW)
    per_bg = lambda shape: pl.BlockSpec((1, 1) + shape, lambda bb, g, n: (bb, g, 0, 0))
    return pl.pallas_call(
        _nsa_prompt_kernel,
        grid=(b, N_KV, nq),
        in_specs=[pl.BlockSpec((Q_BLOCK, Q_PER_KV * HEAD_DIM), lambda bb, g, n: (bb * nq + n, g)),
                  pl.BlockSpec((Q_BLOCK, LANES), lambda bb, g, n: (bb * nq + n, D_MODEL // LANES + g)),
                  pl.BlockSpec((1, LANES), lambda bb, g, n: (0, g)),
                  per_bg((n_c, HEAD_DIM)), per_bg((V_ROWS, n_c)),
                  pl.BlockSpec((LANES, n_c), lambda bb, g, n: (0, 0)),
                  per_bg((t, KA_COLS)), per_bg((V_ROWS, t)),
                  per_bg((WINDOW + t, LANES)), per_bg((V_ROWS, WINDOW + t))],
        out_specs=pl.BlockSpec((Q_BLOCK, Q_PER_KV * HEAD_DIM), lambda bb, g, n: (bb * nq + n, g)),
        out_shape=jax.ShapeDtypeStruct((b * t, D_MODEL), BF16),
        scratch_shapes=[pltpu.VMEM((2, SEL_TILE, cols), F32), pltpu.VMEM((2, SEL_TILE, cols), BF16),
                        pltpu.VMEM((1, cols), F32), pltpu.VMEM((V_ROWS, cols), F32),
                        pltpu.VMEM((2, 1, cols), F32)],
        compiler_params=_params(("parallel", "parallel", "arbitrary")),
        name="nsa_prompt",
    )(z, z, bg_pad, kc, vct, smapt, ka, vat, kw, vwt)


SROWS = 32
SBLK = 256


def _samp_cmpwin_kernel(qbd_ref, kvc_ref, smap_ref, win_ref, wnew_ref, oc_ref, ow_ref, imp_ref, *, qpos, group):
    for u in range(group):
        _samp_cmpwin_one(u, qbd_ref, kvc_ref, smap_ref, win_ref, wnew_ref, oc_ref, ow_ref, imp_ref, qpos)


def _samp_cmpwin_one(u, qbd_ref, kvc_ref, smap_ref, win_ref, wnew_ref, oc_ref, ow_ref, imp_ref, qpos):
    qf = qbd_ref[u] * SCALE
    qb = qf.astype(BF16)
    kc = kvc_ref[u, 0].astype(BF16)
    vc = kvc_ref[u, 1].astype(BF16)
    n_c = kc.shape[0]
    sc = _dot_nt(qb, kc)
    cidx = lax.broadcasted_iota(jnp.int32, (SROWS, n_c), 1)
    mc = (cidx >= 1) & (cidx * CMP_STRIDE + (CMP_STRIDE - 1) <= qpos)
    scm = jnp.where(mc, sc, NEG_INF)
    p = jnp.where(mc, jnp.exp(scm - jnp.max(scm, axis=1, keepdims=True)), 0.0)
    l = jnp.sum(p, axis=1, keepdims=True)
    inv = jnp.where(l > 0.0, 1.0 / l, 0.0)
    oc_ref[u] = _dot(p.astype(BF16), vc) * inv
    pn = p * inv
    psum = pn[0:8] + pn[8:16] + pn[16:24] + pn[24:32]
    p_hi = psum.astype(BF16)
    p_lo = (psum - p_hi.astype(F32)).astype(BF16)
    imp_ref[u] = _dot(p_hi, smap_ref[...]) + _dot(p_lo, smap_ref[...])
    win = win_ref[u]
    n_w = win.shape[1]
    sw = _dot(qb, win[:KV_COLS].astype(BF16))
    widx = lax.broadcasted_iota(jnp.int32, (SROWS, n_w), 1)
    dist = n_w - widx
    mw = (dist < WINDOW) & (qpos - dist >= 0)
    swm = jnp.where(mw, sw, NEG_INF)
    wnew = wnew_ref[u]
    s_new = jnp.sum(qb.astype(F32) * wnew[:, :KV_COLS].astype(BF16).astype(F32), axis=1, keepdims=True)
    m = jnp.maximum(jnp.max(swm, axis=1, keepdims=True), s_new)
    pw = jnp.exp(swm - m)
    p_new = jnp.exp(s_new - m)
    lw = jnp.sum(pw, axis=1, keepdims=True) + p_new
    ow = _dot_nt(pw.astype(BF16), win[KV_COLS:].astype(BF16)) + p_new * wnew[:, KV_COLS:]
    ow_ref[u] = ow / lw


def _samp_topk_kernel(imp_ref, bias_ref, *, cur):
    imp = imp_ref[...]
    jidx = lax.broadcasted_iota(jnp.int32, imp.shape, 1)
    forced = (jidx == 0) | (jidx == cur) | (jidx == cur - 1)
    score = jnp.where(forced, FORCE_SCORE, jnp.where(jidx <= cur, imp, -1.0))
    work = score
    sel = jnp.zeros(imp.shape, F32)
    for _ in range(N_SEL):
        mx = jnp.max(work, axis=1, keepdims=True)
        first = jnp.min(jnp.where(work == mx, jidx, SBLK), axis=1, keepdims=True)
        hit = jidx == first
        sel = jnp.where(hit, 1.0, sel)
        work = jnp.where(hit, -3e38, work)
    bias_ref[...] = jnp.where((sel > 0.0) & (score >= 0.0), 0.0, NEG_INF)


def _samp_sel_kernel(pt_ref, *refs, npg):
    page_refs = refs[:npg]
    (qbd_ref, bias_ref, onehot_ref, knew_ref, oc_ref, ow_ref, zg_ref, bg_ref, out_ref) = refs[npg:]
    n_past_blk = npg * PAGE_SIZE // SEL_LEN
    qb = (qbd_ref[0] * SCALE).astype(BF16)
    bias8 = bias_ref[0][:, :n_past_blk].astype(BF16)
    bias32 = jnp.concatenate([bias8] * Q_PER_KV, axis=0)
    s = jnp.concatenate([_dot(qb, page_refs[t][0, :KV_COLS, :].astype(BF16)) for t in range(npg)], axis=1)
    s = s + _dot(bias32, onehot_ref[...])
    knew = knew_ref[0]
    s_new = jnp.sum(qb.astype(F32) * knew[:, :KV_COLS].astype(BF16).astype(F32), axis=1, keepdims=True)
    m = jnp.maximum(jnp.max(s, axis=1, keepdims=True), s_new)
    p = jnp.exp(s - m)
    p_new = jnp.exp(s_new - m)
    pv = p_new * knew[:, KV_COLS:]
    for t in range(npg):
        pv = pv + _dot_nt(p[:, t * PAGE_SIZE:(t + 1) * PAGE_SIZE].astype(BF16),
                          page_refs[t][0, KV_COLS:, :].astype(BF16))
    os_ = pv / (jnp.sum(p, axis=1, keepdims=True) + p_new)
    gate = 1.0 / (1.0 + jnp.exp(-(zg_ref[0] + bg_ref[0])))
    o = gate[:, 0:1] * oc_ref[0] + gate[:, 1:2] * os_ + gate[:, 2:3] * ow_ref[0]
    gsel = lax.broadcasted_iota(jnp.int32, (SROWS, HEAD_DIM), 0) & 7
    res = jnp.zeros((SROWS, HEAD_DIM), F32)
    for g in range(N_KV):
        res = res + jnp.where(gsel == g, o[:, g * HEAD_DIM:(g + 1) * HEAD_DIM], 0.0)
    out_ref[0] = res


def _nsa_sample(z, b_g, kvc, cache_sel, page_table, sel_new, win_state, win_new, past_len):
    n_seq, n_pages = page_table.shape
    qpos = past_len
    n_blk = -(-(past_len + 1) // SEL_LEN)
    assert n_blk <= SBLK
    n_c = kvc.shape[2]
    n_w = win_state.shape[2]
    q4 = z[:, :D_MODEL].reshape(n_seq, N_KV, Q_PER_KV, HEAD_DIM)
    eye = jnp.eye(N_KV, dtype=F32)
    qbd = jnp.transpose(q4[:, :, :, None, :] * eye[None, :, None, :, None], (0, 2, 1, 3, 4))
    qbd = jnp.pad(qbd, ((0, 0), (0, 0), (0, 8 - N_KV), (0, 0), (0, 0))).reshape(n_seq, SROWS, KV_COLS)
    def gate_rows(a):
        n = a.shape[0]
        a = jnp.transpose(a.reshape(n, N_KV, Q_PER_KV, 3), (0, 2, 1, 3))
        return jnp.pad(a, ((0, 0), (0, 0), (0, 8 - N_KV), (0, LANES - 3))).reshape(n, SROWS, LANES)

    zg = gate_rows(z[:, D_MODEL:D_MODEL + 3 * N_HEADS])
    bg = gate_rows(b_g[None, :])
    smap = _selection_map(n_c, n_blk, SBLK)
    group = 4 if n_seq % 4 == 0 else 1
    per_seq = lambda shape: pl.BlockSpec((group,) + shape, lambda s: (s,) + (0,) * len(shape))
    o_c, o_w, imp = pl.pallas_call(
        functools.partial(_samp_cmpwin_kernel, qpos=qpos, group=group),
        grid=(n_seq // group,),
        in_specs=[per_seq((SROWS, KV_COLS)), per_seq((2, n_c, KV_COLS)),
                  pl.BlockSpec((n_c, SBLK), lambda s: (0, 0)), per_seq((2 * KV_COLS, n_w)),
                  per_seq((1, 2 * KV_COLS))],
        out_specs=(per_seq((SROWS, KV_COLS)), per_seq((SROWS, KV_COLS)), per_seq((8, SBLK))),
        out_shape=(jax.ShapeDtypeStruct((n_seq, SROWS, KV_COLS), F32),
                   jax.ShapeDtypeStruct((n_seq, SROWS, KV_COLS), F32),
                   jax.ShapeDtypeStruct((n_seq, 8, SBLK), F32)),
        compiler_params=_params(("parallel",)),
        name="nsa_sample_cmpwin",
    )(qbd, kvc, smap, win_state, win_new[:, None, :])
    bias = pl.pallas_call(
        functools.partial(_samp_topk_kernel, cur=qpos // SEL_LEN),
        out_shape=jax.ShapeDtypeStruct((n_seq * 8, SBLK), F32),
        name="nsa_sample_topk",
    )(imp.reshape(n_seq * 8, SBLK)).reshape(n_seq, 8, SBLK)

    n_past_blk = past_len // SEL_LEN
    keys = n_pages * PAGE_SIZE
    assert n_past_blk == n_blk - 1
    onehot = jnp.asarray(np.arange(keys)[None, :] // SEL_LEN == np.arange(n_past_blk)[:, None], BF16)

    def page_spec(t):
        return pl.BlockSpec((1, 2 * KV_COLS, PAGE_SIZE), lambda s, pt: (pt[s * n_pages + t], 0, 0))

    seq_blk = lambda shape: pl.BlockSpec((1,) + shape, lambda s, pt: (s,) + (0,) * len(shape))
    out = pl.pallas_call(
        functools.partial(_samp_sel_kernel, npg=n_pages),
        grid_spec=pltpu.PrefetchScalarGridSpec(
            num_scalar_prefetch=1,
            grid=(n_seq,),
            in_specs=[page_spec(t) for t in range(n_pages)] + [
                seq_blk((SROWS, KV_COLS)), seq_blk((8, SBLK)),
                pl.BlockSpec((n_past_blk, keys), lambda s, pt: (0, 0)), seq_blk((1, 2 * KV_COLS)),
                seq_blk((SROWS, KV_COLS)), seq_blk((SROWS, KV_COLS)), seq_blk((SROWS, LANES)),
                pl.BlockSpec((1, SROWS, LANES), lambda s, pt: (0, 0, 0))],
            out_specs=seq_blk((SROWS, HEAD_DIM))),
        out_shape=jax.ShapeDtypeStruct((n_seq, SROWS, HEAD_DIM), F32),
        compiler_params=_params(("parallel",)),
        name="nsa_sample_sel",
    )(page_table.reshape(-1), *([cache_sel] * n_pages), qbd, bias, onehot, sel_new[:, None, :], o_c, o_w, zg, bg)
    out = out.reshape(n_seq, Q_PER_KV, 8, HEAD_DIM)[:, :, :N_KV]
    return jnp.transpose(out, (0, 2, 1, 3)).reshape(n_seq, D_MODEL).astype(BF16)


def kernel(x_prompt, x_sample, cache_cmp_kv, cache_sel_kv, state_win_kv, page_table, a_norm_g, a_w_uv, a_v_norm_g, a_w_s, a_b_s, a_w_o, kv_norm_g, w_kv, cmp_pe_k, cmp_w1_k, cmp_w2_k, cmp_pe_v, cmp_w1_v, cmp_w2_v, b_norm_g, b_w_qg, b_b_g, b_w_o, mlp_norm_g, mlp_w_up, mlp_w_down, final_norm_g):
    b, t, _ = x_prompt.shape
    n_seq, dec_seq, _ = x_sample.shape
    n_pool, page, _, _, _ = cache_cmp_kv.shape
    n_pages = page_table.shape[1]
    past_len = n_pages * page
    win_buf = state_win_kv.shape[1]
    depth = mlp_norm_g.shape[0]
    n_a = a_norm_g.shape[0]
    assert dec_seq == 1 and page == PAGE_SIZE and t % PAGE_SIZE == 0
    cmp_w = (cmp_pe_k, cmp_w1_k, cmp_w2_k, cmp_pe_v, cmp_w1_v, cmp_w2_v)
    kv_shape = (2, N_KV, HEAD_DIM)
    gate_w = 3 * N_HEADS

    hp = x_prompt.reshape(b * t, D_MODEL)
    hs = x_sample.reshape(n_seq, D_MODEL)
    a_v_rows = []
    attn_p = attn_s = None
    for layer in range(depth):
        last = layer == depth - 1
        fin = final_norm_g if last else None
        if layer < n_a:
            i = layer
            hp = _gmlp_layer(hp, a_norm_g[i], a_w_uv[i], a_v_norm_g[i], a_w_s[i], a_b_s[i], a_w_o[i], chunked=True)
            hs, v_s = _gmlp_layer(hs, a_norm_g[i], a_w_uv[i], a_v_norm_g[i], a_w_s[i], a_b_s[i], a_w_o[i],
                                  chunked=False)
            a_v_rows.append(v_s.reshape(n_seq, 1, D_MODEL))
            hp = _mlp_layer(hp, mlp_norm_g[layer], mlp_w_up[layer], mlp_w_down[layer], final_g=fin)
            hs = _mlp_layer(hs, mlp_norm_g[layer], mlp_w_up[layer], mlp_w_down[layer], final_g=fin)
            continue
        if layer == n_a:
            cmp_p, sel_p, win_p = _norm_matmul(hp, kv_norm_g, w_kv, 3)
            cmp_s, sel_s, win_s = _norm_matmul(hs, kv_norm_g, w_kv, 3)
            ncol = 2 * KV_COLS // LANES
            kvc_p = _compress(cmp_p.reshape(b * t // PAGE_SIZE, ncol * PAGE_SIZE, LANES),
                              jnp.arange(b * t // PAGE_SIZE, dtype=jnp.int32).reshape(b, t // PAGE_SIZE), *cmp_w,
                              channel_major=False)

            def channel_major(a):
                return jnp.transpose(a, (0, 2, 3, 4, 1)).reshape(a.shape[0], 2 * KV_COLS, a.shape[1])

            kvc_s = _compress(channel_major(cache_cmp_kv), page_table, *cmp_w, channel_major=True)
            cache_sel = channel_major(cache_sel_kv)
            win_state = channel_major(state_win_kv)
        j = layer - n_a
        wg = b_w_qg[j][:, D_MODEL:].reshape(D_MODEL, N_KV, Q_PER_KV, 3)
        wg = jnp.transpose(wg, (0, 1, 3, 2)).reshape(D_MODEL, N_KV, 3 * Q_PER_KV)
        wg = jnp.pad(wg, ((0, 0), (0, 0), (0, LANES - 3 * Q_PER_KV))).reshape(D_MODEL, N_KV * LANES)
        bg = jnp.transpose(b_b_g[j].reshape(N_KV, Q_PER_KV, 3), (0, 2, 1)).reshape(N_KV, 3 * Q_PER_KV)
        bg = jnp.pad(bg, ((0, 0), (0, LANES - 3 * Q_PER_KV))).reshape(1, N_KV * LANES)
        (z_p,) = _norm_matmul(hp, b_norm_g[j], jnp.concatenate([b_w_qg[j][:, :D_MODEL], wg], axis=1), 1)
        attn_p = _nsa_prompt(z_p, bg, kvc_p, sel_p, win_p, b, t)
        w_s_pad = jnp.pad(b_w_qg[j], ((0, 0), (0, LANES - gate_w % LANES)))
        (z_s,) = _norm_matmul(hs, b_norm_g[j], w_s_pad, 1)
        attn_s = _nsa_sample(z_s, b_b_g[j], kvc_s, cache_sel, page_table, sel_s, win_state, win_s, past_len)
        hp = _mlp_layer(hp, mlp_norm_g[layer], mlp_w_up[layer], mlp_w_down[layer], attn=attn_p, w_o=b_w_o[j],
                        final_g=fin)
        hs = _mlp_layer(hs, mlp_norm_g[layer], mlp_w_up[layer], mlp_w_down[layer], attn=attn_s, w_o=b_w_o[j],
                        final_g=fin)

    y_prompt = hp.reshape(b, t, D_MODEL)
    y_sample = hs.reshape(n_seq, 1, D_MODEL)
    nw = min(WINDOW, t)
    win_kv_prompt = win_p.reshape(b, t, *kv_shape)[:, t - nw:]
    win_kv_sample = jnp.concatenate([state_win_kv, win_s.reshape(n_seq, 1, *kv_shape)], axis=1)[:, dec_seq:]
    return (y_prompt, y_sample, cmp_p.reshape(b, t, *kv_shape), sel_p.reshape(b, t, *kv_shape), win_kv_prompt,
            cmp_s.reshape(n_seq, 1, *kv_shape), sel_s.reshape(n_seq, 1, *kv_shape), win_kv_sample,
            jnp.stack(a_v_rows, axis=0))
```
